```python
import jax, jax.numpy as jnp
from jax import lax
import numpy as np


D_MODEL = 2048
BATCH = 2
SEQ = 4096
DEPTH = 2
DEC_BATCH = 32
DEC_SEQ = 4
PAST_LEN = 8192
PAGE_SIZE = 128

N_A = DEPTH // 2
N_B = DEPTH - N_A
RET_HEADS = 8
RET_DK = D_MODEL // RET_HEADS
RET_DV = 2 * RET_DK
RET_CHUNK = 128
ROPE_BASE = 10000.0
DIL_PAIRS = ((128, 1), (512, 4), (2048, 16))
N_GROUPS = len(DIL_PAIRS)
DIL_HEADS = 16
DIL_HD = D_MODEL // DIL_HEADS
D_FF = ((8 * D_MODEL // 3 + 255) // 256) * 256
EPS = 1e-6
NEG = -1e30

kernel_name = 'yoco_retention_dilated_attention_step'


def rmsnorm(x, g):
    xf = x.astype(jnp.float32)
    y = xf * lax.rsqrt(jnp.mean(xf * xf, axis=-1, keepdims=True) + EPS) * g.astype(jnp.float32)
    return y.astype(x.dtype)


def rope(x, pos):
    half = x.shape[-1] // 2
    inv = ROPE_BASE ** (-jnp.arange(half, dtype=jnp.float32) / half)
    ang = pos.astype(jnp.float32)[:, None] * inv[None, :]
    cos = jnp.cos(ang)[None, :, None, :]
    sin = jnp.sin(ang)[None, :, None, :]
    x1, x2 = x[..., :half], x[..., half:]
    return jnp.concatenate([x1 * cos - x2 * sin, x1 * sin + x2 * cos], axis=-1).astype(x.dtype)


def retention_chunks(q, k, v, s0, chunk):
    B, T, H, DK = q.shape
    DV = v.shape[-1]
    nc = T // chunk
    lg = jnp.log1p(-jnp.exp2(-5.0 - jnp.arange(H, dtype=jnp.float32)))
    i = jnp.arange(chunk, dtype=jnp.float32)
    dist = i[:, None] - i[None, :]
    dmat = jnp.where(dist >= 0, jnp.exp(jnp.maximum(dist, 0.0)[None] * lg[:, None, None]), 0.0)
    qdec = jnp.exp((i + 1.0)[:, None] * lg[None, :])
    kdec = jnp.exp((chunk - 1.0 - i)[:, None] * lg[None, :])
    cdec = jnp.exp(chunk * lg)

    def split(t):
        return jnp.moveaxis(t.reshape(B, nc, chunk, H, t.shape[-1]), 1, 0)

    def step(s, blk):
        qc, kc, vc = blk
        sc = jnp.einsum('bihd,bjhd->bhij', qc, kc) * dmat
        o = (jnp.einsum('bhij,bjhe->bihe', sc, vc)
             + jnp.einsum('bihd,bhde->bihe', qc * qdec[None, :, :, None], s))
        s = s * cdec[:, None, None] + jnp.einsum('bjhd,bjhe->bhde', kc * kdec[None, :, :, None], vc)
        return s, o

    s_fin, o = lax.scan(step, s0.astype(jnp.float32), (split(q), split(k), split(v)))
    o = jnp.moveaxis(o, 0, 1).reshape(B, T, H, DV)
    return o, s_fin


def retention_mixer(x, s0, pos, chunk, w_in, gn_g, w_out):
    B, T, _ = x.shape
    qk_w = RET_HEADS * RET_DK
    v_w = RET_HEADS * RET_DV
    proj = x @ w_in
    q = proj[..., :qk_w].reshape(B, T, RET_HEADS, RET_DK)
    k = proj[..., qk_w:2 * qk_w].reshape(B, T, RET_HEADS, RET_DK)
    v = proj[..., 2 * qk_w:2 * qk_w + v_w].reshape(B, T, RET_HEADS, RET_DV)
    g = proj[..., 2 * qk_w + v_w:]
    q = rope(q, pos)
    k = rope(k, pos) * (RET_DK ** -0.5)
    o, s_fin = retention_chunks(q, k, v, s0, chunk)
    of = o.astype(jnp.float32)
    mu = jnp.mean(of, axis=-1, keepdims=True)
    var = jnp.mean(jnp.square(of - mu), axis=-1, keepdims=True)
    on = ((of - mu) * lax.rsqrt(var + EPS)).reshape(B, T, v_w) * gn_g.astype(jnp.float32)
    y = (jax.nn.silu(g) * on.astype(x.dtype)) @ w_out
    return y, s_fin


def dilated_prompt(q, k, v, window, dilation):
    B, S, H, Dh = q.shape
    blk = window // dilation
    span = blk * dilation
    s_pad = -(-S // span) * span
    pad = ((0, 0), (0, s_pad - S), (0, 0), (0, 0))

    def to_blocks(t):
        t = jnp.pad(t, pad).reshape(B, s_pad // dilation, dilation, H, Dh)
        t = t.transpose(0, 2, 1, 3, 4)
        return t.reshape(B, dilation, -1, blk, H, Dh)

    def with_prev(t):
        prev = jnp.pad(t, ((0, 0), (0, 0), (1, 0), (0, 0), (0, 0), (0, 0)))[:, :, :-1]
        return jnp.concatenate([prev, t], axis=3)

    qb = to_blocks(q)
    kk = with_prev(to_blocks(k))
    vv = with_prev(to_blocks(v))
    nb = qb.shape[2]
    a = jnp.arange(blk)
    b = jnp.arange(2 * blk)
    dist = blk + a[:, None] - b[None, :]
    band = (dist >= 0) & (dist <= blk)
    mask = band[None] & ((jnp.arange(nb)[:, None, None] > 0) | (b >= blk)[None, None, :])
    s = jnp.einsum('brnqhd,brnkhd->brnhqk', qb, kk).astype(jnp.float32) * (Dh ** -0.5)
    s = jnp.where(mask[None, None, :, None], s, NEG)
    m = jnp.max(s, axis=-1, keepdims=True)
    p = jnp.exp(s - m)
    den = jnp.sum(p, axis=-1)
    o = jnp.einsum('brnhqk,brnkhd->brnqhd', (p / den[..., None]).astype(vv.dtype), vv)
    lse = m[..., 0] + jnp.log(den)
    o = o.reshape(B, dilation, -1, H, Dh).transpose(0, 2, 1, 3, 4).reshape(B, s_pad, H, Dh)[:, :S]
    lse = lse.transpose(0, 1, 2, 4, 3).reshape(B, dilation, -1, H).transpose(0, 2, 1, 3).reshape(B, s_pad, H)[:, :S]
    return o, lse


def dilated_sample(q, kk, vv, window, dilation):
    T = q.shape[1]
    L = kk.shape[1] - T
    nk = window // dilation + 1
    idx = L + jnp.arange(T)[:, None] - dilation * jnp.arange(nk)[None, :]
    valid = idx >= 0
    gi = jnp.maximum(idx, 0)
    kg = jnp.take(kk, gi, axis=1)
    vg = jnp.take(vv, gi, axis=1)
    s = jnp.einsum('bthd,btkhd->bthk', q, kg).astype(jnp.float32) * (q.shape[-1] ** -0.5)
    s = jnp.where(valid[None, :, None, :], s, NEG)
    m = jnp.max(s, axis=-1, keepdims=True)
    p = jnp.exp(s - m)
    den = jnp.sum(p, axis=-1)
    o = jnp.einsum('bthk,btkhd->bthd', (p / den[..., None]).astype(vg.dtype), vg)
    lse = m[..., 0] + jnp.log(den)
    return o, lse


def dilated_mixer(xn, k_list, v_list, w_q, w_o, sample):
    B, T, _ = xn.shape
    q = (xn @ w_q).reshape(B, T, N_GROUPS, DIL_HEADS, DIL_HD)
    outs, lses = [], []
    for gi, (win, dil) in enumerate(DIL_PAIRS):
        if sample:
            o, lse = dilated_sample(q[:, :, gi], k_list[gi], v_list[gi], win, dil)
        else:
            o, lse = dilated_prompt(q[:, :, gi], k_list[gi], v_list[gi], win, dil)
        outs.append(o)
        lses.append(lse)
    w = jax.nn.softmax(jnp.stack(lses, axis=0), axis=0)
    o = jnp.sum(w[..., None] * jnp.stack(outs, axis=0).astype(jnp.float32), axis=0)
    return o.reshape(B, T, DIL_HEADS * DIL_HD).astype(xn.dtype) @ w_o


def shared_kv(h, g, w_kv):
    B, T, _ = h.shape
    kv = (rmsnorm(h, g) @ w_kv).reshape(B, T, N_GROUPS, 2, DIL_HEADS, DIL_HD)
    return [kv[:, :, i, 0] for i in range(N_GROUPS)], [kv[:, :, i, 1] for i in range(N_GROUPS)]


def swiglu(x, w1, w3, w2):
    return (jax.nn.silu(x @ w1) * (x @ w3)) @ w2


def setup_inputs(seed: int = 0) -> dict:
    key = jax.random.key(seed)
    k = jax.random.split(key, 22)
    f32 = jnp.float32

    def nrm(kk, shape, scale=1.0):
        return jax.random.normal(kk, shape, f32) * scale

    ret_in = 2 * RET_HEADS * RET_DK + 2 * RET_HEADS * RET_DV
    dil_w = DIL_HEADS * DIL_HD

    def cshape(w):
        return (DEC_BATCH, min(w, PAST_LEN), DIL_HEADS, DIL_HD)

    w0, w1, w2 = DIL_PAIRS[0][0], DIL_PAIRS[1][0], DIL_PAIRS[2][0]
    return {
        'x_prompt': nrm(k[0], (BATCH, SEQ, D_MODEL)),
        'x_sample': nrm(k[1], (DEC_BATCH, DEC_SEQ, D_MODEL)),
        'state_ret': nrm(k[2], (N_A, DEC_BATCH, RET_HEADS, RET_DK, RET_DV), 0.02),
        'cache_k_w128': nrm(k[3], cshape(w0)),
        'cache_v_w128': nrm(k[4], cshape(w0)),
        'cache_k_w512': nrm(k[5], cshape(w1)),
        'cache_v_w512': nrm(k[6], cshape(w1)),
        'cache_k_w2048': nrm(k[7], cshape(w2)),
        'cache_v_w2048': nrm(k[8], cshape(w2)),
        'norm_mix': 1.0 + nrm(k[9], (DEPTH, D_MODEL), 0.02),
        'norm_ffn': 1.0 + nrm(k[10], (DEPTH, D_MODEL), 0.02),
        'ret_w_in': nrm(k[11], (N_A, D_MODEL, ret_in), D_MODEL ** -0.5),
        'ret_gn': 1.0 + nrm(k[12], (N_A, RET_HEADS * RET_DV), 0.02),
        'ret_w_out': nrm(k[13], (N_A, RET_HEADS * RET_DV, D_MODEL), (RET_HEADS * RET_DV) ** -0.5),
        'kv_norm': 1.0 + nrm(k[14], (D_MODEL,), 0.02),
        'w_kv': nrm(k[15], (D_MODEL, N_GROUPS * 2 * dil_w), D_MODEL ** -0.5),
        'dil_w_q': nrm(k[16], (N_B, D_MODEL, N_GROUPS * dil_w), D_MODEL ** -0.5),
        'dil_w_o': nrm(k[17], (N_B, dil_w, D_MODEL), dil_w ** -0.5),
        'ffn_w1': nrm(k[18], (DEPTH, D_MODEL, D_FF), D_MODEL ** -0.5),
        'ffn_w3': nrm(k[19], (DEPTH, D_MODEL, D_FF), D_MODEL ** -0.5),
        'ffn_w2': nrm(k[20], (DEPTH, D_FF, D_MODEL), D_FF ** -0.5),
        'norm_final': 1.0 + nrm(k[21], (D_MODEL,), 0.02),
    }


def reference(x_prompt, x_sample, state_ret, cache_k_w128, cache_v_w128, cache_k_w512, cache_v_w512,
              cache_k_w2048, cache_v_w2048, norm_mix, norm_ffn, ret_w_in, ret_gn, ret_w_out,
              kv_norm, w_kv, dil_w_q, dil_w_o, ffn_w1, ffn_w3, ffn_w2, norm_final):
    h_p, h_s = x_prompt, x_sample
    Bp, Tp, _ = x_prompt.shape
    Bs, Ts, _ = x_sample.shape
    pos_p = jnp.arange(Tp)
    pos_s = PAST_LEN + jnp.arange(Ts)
    chunk_p = RET_CHUNK if Tp % RET_CHUNK == 0 else Tp
    bufs_k = [cache_k_w128, cache_k_w512, cache_k_w2048]
    bufs_v = [cache_v_w128, cache_v_w512, cache_v_w2048]
    ret_p, ret_s = [], []
    kp = vp = ks = vs = None
    new_kp = new_vp = new_ks = new_vs = None
    for layer in range(DEPTH):
        if layer < N_A:
            s0 = jnp.zeros((Bp, RET_HEADS, RET_DK, RET_DV), jnp.float32)
            o_p, sp = retention_mixer(rmsnorm(h_p, norm_mix[layer]), s0, pos_p, chunk_p,
                                      ret_w_in[layer], ret_gn[layer], ret_w_out[layer])
            o_s, ss = retention_mixer(rmsnorm(h_s, norm_mix[layer]), state_ret[layer], pos_s, Ts,
                                      ret_w_in[layer], ret_gn[layer], ret_w_out[layer])
            h_p = h_p + o_p
            h_s = h_s + o_s
            ret_p.append(sp)
            ret_s.append(ss)
        else:
            if layer == N_A:
                kp, vp = shared_kv(h_p, kv_norm, w_kv)
                knew, vnew = shared_kv(h_s, kv_norm, w_kv)
                ks = [jnp.concatenate([bufs_k[i], knew[i].astype(bufs_k[i].dtype)], axis=1) for i in range(N_GROUPS)]
                vs = [jnp.concatenate([bufs_v[i], vnew[i].astype(bufs_v[i].dtype)], axis=1) for i in range(N_GROUPS)]
                new_kp = [kp[i][:, -min(DIL_PAIRS[i][0], Tp):] for i in range(N_GROUPS)]
                new_vp = [vp[i][:, -min(DIL_PAIRS[i][0], Tp):] for i in range(N_GROUPS)]
                new_ks = [ks[i][:, Ts:] for i in range(N_GROUPS)]
                new_vs = [vs[i][:, Ts:] for i in range(N_GROUPS)]
            j = layer - N_A
            h_p = h_p + dilated_mixer(rmsnorm(h_p, norm_mix[layer]), kp, vp, dil_w_q[j], dil_w_o[j], False)
            h_s = h_s + dilated_mixer(rmsnorm(h_s, norm_mix[layer]), ks, vs, dil_w_q[j], dil_w_o[j], True)
        h_p = h_p + swiglu(rmsnorm(h_p, norm_ffn[layer]), ffn_w1[layer], ffn_w3[layer], ffn_w2[layer])
        h_s = h_s + swiglu(rmsnorm(h_s, norm_ffn[layer]), ffn_w1[layer], ffn_w3[layer], ffn_w2[layer])
    y_prompt = rmsnorm(h_p, norm_final)
    y_sample = rmsnorm(h_s, norm_final)
    state_ret_prompt = jnp.stack(ret_p, axis=0)
    state_ret_sample = jnp.stack(ret_s, axis=0)
    return (y_prompt, y_sample, state_ret_prompt, state_ret_sample,
            new_kp[0], new_vp[0], new_kp[1], new_vp[1], new_kp[2], new_vp[2],
            new_ks[0], new_vs[0], new_ks[1], new_vs[1], new_ks[2], new_vs[2])
```

```python
import functools

import numpy as np
import jax
import jax.numpy as jnp
from jax import lax
from jax.experimental import pallas as pl
from jax.experimental.pallas import tpu as pltpu

F32 = jnp.float32
BF16 = jnp.bfloat16

RET_HEADS = 8
RET_CHUNK = 128
ROPE_BASE = 10000.0
DIL_PAIRS = ((128, 1), (512, 4), (2048, 16))
N_GROUPS = len(DIL_PAIRS)
DIL_HEADS = 16
PAST_LEN = 8192
EPS = 1e-6
NEG = -1e30

LANES = 128
SAMPLE_PAD = 16
VMEM_LIMIT = 56 * 1024 * 1024


def _cparams(*sem):
    return pltpu.CompilerParams(dimension_semantics=sem, vmem_limit_bytes=VMEM_LIMIT)


def _linear_kernel(*refs, n_w, has_gain, has_res, swiglu, norm_rows):
    it = iter(refs)
    x_ref = next(it)
    g_ref = next(it) if has_gain else None
    w_refs = [next(it) for _ in range(n_w)]
    r_ref = next(it) if has_res else None
    o_ref = next(it)
    xn_ref = next(it) if has_gain else None

    if has_gain:
        @pl.when(pl.program_id(1) == 0)
        def _():
            gain = g_ref[...]

            def body(c, carry):
                rows = pl.ds(pl.multiple_of(c * norm_rows, norm_rows), norm_rows)
                xf = x_ref[rows, :].astype(F32)
                ms = jnp.mean(xf * xf, axis=-1, keepdims=True)
                xn_ref[rows, :] = (xf * lax.rsqrt(ms + EPS) * gain).astype(BF16)
                return carry

            lax.fori_loop(0, x_ref.shape[0] // norm_rows, body, 0)

        xn = xn_ref[...]
    else:
        xn = x_ref[...].astype(BF16)

    a = jnp.dot(xn, w_refs[0][...].astype(BF16), preferred_element_type=F32)
    if swiglu:
        b = jnp.dot(xn, w_refs[1][...].astype(BF16), preferred_element_type=F32)
        a = (a * jax.nn.sigmoid(a)) * b
    if has_res:
        a = a + r_ref[...]
    o_ref[...] = a.astype(o_ref.dtype)


def _linear(x, ws, *, gain=None, residual=None, swiglu=False, out_dtype, tm, tn, name):
    M, K = x.shape
    N = ws[0].shape[1]
    tm = min(tm, M)
    assert M % tm == 0 and N % tn == 0
    has_gain = gain is not None
    has_res = residual is not None
    norm_rows = min(64, tm)
    in_specs = [pl.BlockSpec((tm, K), lambda i, j: (i, 0))]
    args = [x]
    if has_gain:
        in_specs.append(pl.BlockSpec((1, K), lambda i, j: (0, 0)))
        args.append(gain.reshape(1, K).astype(F32))
    for w in ws:
        in_specs.append(pl.BlockSpec((K, tn), lambda i, j: (0, j)))
        args.append(w)
    if has_res:
        in_specs.append(pl.BlockSpec((tm, tn), lambda i, j: (i, j)))
        args.append(residual)
    scratch = [pltpu.VMEM((tm, K), BF16)] if has_gain else []
    kern = functools.partial(_linear_kernel, n_w=len(ws), has_gain=has_gain, has_res=has_res,
                             swiglu=swiglu, norm_rows=norm_rows)
    return pl.pallas_call(
        kern,
        grid=(M // tm, N // tn),
        in_specs=in_specs,
        out_specs=pl.BlockSpec((tm, tn), lambda i, j: (i, j)),
        out_shape=jax.ShapeDtypeStruct((M, N), out_dtype),
        scratch_shapes=scratch,
        compiler_params=_cparams("parallel", "arbitrary"),
        name=name,
    )(*args)


def _rmsnorm_kernel(x_ref, g_ref, o_ref):
    xf = x_ref[...]
    ms = jnp.mean(xf * xf, axis=-1, keepdims=True)
    o_ref[...] = xf * lax.rsqrt(ms + EPS) * g_ref[...]


def _rmsnorm(x, gain, *, tm, name):
    M, K = x.shape
    tm = min(tm, M)
    return pl.pallas_call(
        _rmsnorm_kernel,
        grid=(M // tm,),
        in_specs=[pl.BlockSpec((tm, K), lambda i: (i, 0)), pl.BlockSpec((1, K), lambda i: (0, 0))],
        out_specs=pl.BlockSpec((tm, K), lambda i: (i, 0)),
        out_shape=jax.ShapeDtypeStruct((M, K), F32),
        compiler_params=_cparams("parallel"),
        name=name,
    )(x, gain.reshape(1, K))


def _retention_kernel(q_ref, k_ref, v_ref, g_ref, cos_ref, sin_ref, dmat_ref, qdec_ref, kdec_ref,
                      cdec_ref, gn_ref, s0_ref, o_ref, s_ref, *, dk):
    @pl.when(pl.program_id(2) == 0)
    def _():
        s_ref[...] = s0_ref[...]

    cos = cos_ref[...]
    sin = sin_ref[...]
    half = dk // 2

    def rope(x):
        x1 = x[:, :half]
        x2 = x[:, half:]
        return jnp.concatenate([x1 * cos - x2 * sin, x1 * sin + x2 * cos], axis=-1)

    q = rope(q_ref[...].astype(F32))
    k = rope(k_ref[...].astype(F32)) * (dk ** -0.5)
    v = v_ref[...]
    s = s_ref[0, 0]

    sc = lax.dot_general(q.astype(BF16), k.astype(BF16), (((1,), (1,)), ((), ())),
                         preferred_element_type=F32) * dmat_ref[0]
    o = (jnp.dot(sc.astype(BF16), v, preferred_element_type=F32)
         + jnp.dot((q * qdec_ref[0]).astype(BF16), s.astype(BF16), preferred_element_type=F32))
    kd = (k * kdec_ref[0]).astype(BF16)
    s_ref[0, 0] = s * cdec_ref[0] + lax.dot_general(kd, v, (((0,), (0,)), ((), ())),
                                                   preferred_element_type=F32)

    mu = jnp.mean(o, axis=-1, keepdims=True)
    var = jnp.mean(jnp.square(o - mu), axis=-1, keepdims=True)
    on = (o - mu) * lax.rsqrt(var + EPS) * gn_ref[...]
    gate = g_ref[...].astype(F32)
    o_ref[...] = ((gate * jax.nn.sigmoid(gate)) * on).astype(o_ref.dtype)


def _retention(proj, s0, gn_g, *, batch, seq, chunk, chunk_true, pos, name):
    H = RET_HEADS
    dk, dv = s0.shape[-2], s0.shape[-1]
    nc = seq // chunk
    lg = jnp.log1p(-jnp.exp2(-5.0 - jnp.arange(H, dtype=F32)))
    i = jnp.arange(chunk, dtype=F32)
    dist = i[:, None] - i[None, :]
    dmat = jnp.where(dist >= 0, jnp.exp(jnp.maximum(dist, 0.0)[None] * lg[:, None, None]), 0.0)
    qdec = jnp.exp((i + 1.0)[None, :, None] * lg[:, None, None])
    kdec = jnp.exp((chunk_true - 1.0 - i)[None, :, None] * lg[:, None, None])
    cdec = jnp.exp(chunk_true * lg).reshape(H, 1, 1)
    half = dk // 2
    inv = ROPE_BASE ** (-jnp.arange(half, dtype=F32) / half)
    ang = pos.astype(F32)[:, None] * inv[None, :]
    cos, sin = jnp.cos(ang), jnp.sin(ang)

    qb, vb = (H * dk) // dk, (2 * H * dk) // dv
    gb = vb + H
    row = lambda b, h, c: b * nc + c
    in_specs = [
        pl.BlockSpec((chunk, dk), lambda b, h, c: (row(b, h, c), h)),
        pl.BlockSpec((chunk, dk), lambda b, h, c: (row(b, h, c), qb + h)),
        pl.BlockSpec((chunk, dv), lambda b, h, c: (row(b, h, c), vb + h)),
        pl.BlockSpec((chunk, dv), lambda b, h, c: (row(b, h, c), gb + h)),
        pl.BlockSpec((chunk, half), lambda b, h, c: (c, 0)),
        pl.BlockSpec((chunk, half), lambda b, h, c: (c, 0)),
        pl.BlockSpec((1, chunk, chunk), lambda b, h, c: (h, 0, 0)),
        pl.BlockSpec((1, chunk, 1), lambda b, h, c: (h, 0, 0)),
        pl.BlockSpec((1, chunk, 1), lambda b, h, c: (h, 0, 0)),
        pl.BlockSpec((1, 1, 1), lambda b, h, c: (h, 0, 0)),
        pl.BlockSpec((1, dv), lambda b, h, c: (0, h)),
        pl.BlockSpec((1, 1, dk, dv), lambda b, h, c: (b, h, 0, 0)),
    ]
    out_specs = [
        pl.BlockSpec((chunk, dv), lambda b, h, c: (row(b, h, c), h)),
        pl.BlockSpec((1, 1, dk, dv), lambda b, h, c: (b, h, 0, 0)),
    ]
    out_shape = [jax.ShapeDtypeStruct((batch * seq, H * dv), BF16),
                 jax.ShapeDtypeStruct((batch, H, dk, dv), F32)]
    return pl.pallas_call(
        functools.partial(_retention_kernel, dk=dk),
        grid=(batch, H, nc),
        in_specs=in_specs,
        out_specs=out_specs,
        out_shape=out_shape,
        compiler_params=_cparams("parallel", "parallel", "arbitrary"),
        name=name,
    )(proj, proj, proj, proj, cos, sin, dmat, qdec, kdec, cdec, gn_g.reshape(1, H * dv), s0)


def _dilated_prompt_kernel(*refs, heads, hd, merge):
    if merge:
        (q_ref, k_ref, v_ref, o0_ref, o1_ref, l0_ref, l1_ref, o_ref, kp_ref, vp_ref) = refs
    else:
        (q_ref, k_ref, v_ref, o_ref, l_ref, kp_ref, vp_ref) = refs
    n = pl.program_id(2)
    blk = q_ref.shape[1]

    @pl.when(n == 0)
    def _():
        kp_ref[...] = jnp.zeros_like(kp_ref)
        vp_ref[...] = jnp.zeros_like(vp_ref)

    row = lax.broadcasted_iota(jnp.int32, (blk, blk), 0)
    col = lax.broadcasted_iota(jnp.int32, (blk, blk), 1)
    mask_c = col <= row
    mask_p = (col - row) >= jnp.where(n > 0, 0, blk)
    lane = lax.broadcasted_iota(jnp.int32, (blk, LANES), 1)
    scale = hd ** -0.5
    nt = (((1,), (1,)), ((), ()))
    lse_tile = jnp.zeros((blk, LANES), F32)

    for h in range(heads):
        cs = slice(h * hd, (h + 1) * hd)
        qh = q_ref[0, :, cs]
        kc = k_ref[0, :, cs].astype(BF16)
        vc = v_ref[0, :, cs].astype(BF16)
        kp = kp_ref[:, cs]
        vp = vp_ref[:, cs]
        s_c = jnp.where(mask_c, lax.dot_general(qh, kc, nt, preferred_element_type=F32) * scale, NEG)
        s_p = jnp.where(mask_p, lax.dot_general(qh, kp, nt, preferred_element_type=F32) * scale, NEG)
        m = jnp.maximum(jnp.max(s_c, axis=-1, keepdims=True), jnp.max(s_p, axis=-1, keepdims=True))
        p_c = jnp.exp(s_c - m)
        p_p = jnp.exp(s_p - m)
        den = jnp.sum(p_c, axis=-1, keepdims=True) + jnp.sum(p_p, axis=-1, keepdims=True)
        inv = 1.0 / den
        o = (jnp.dot((p_c * inv).astype(BF16), vc, preferred_element_type=F32)
             + jnp.dot((p_p * inv).astype(BF16), vp, preferred_element_type=F32))
        lse = m + jnp.log(den)
        if merge:
            l0 = l0_ref[0, :, h:h + 1]
            l1 = l1_ref[0, :, h:h + 1]
            mx = jnp.maximum(jnp.maximum(l0, l1), lse)
            e0 = jnp.exp(l0 - mx)
            e1 = jnp.exp(l1 - mx)
            e2 = jnp.exp(lse - mx)
            tot = 1.0 / (e0 + e1 + e2)
            o = ((e0 * tot) * o0_ref[0, :, cs].astype(F32) + (e1 * tot) * o1_ref[0, :, cs].astype(F32)
                 + (e2 * tot) * o)
        else:
            lse_tile = jnp.where(lane == h, lse, lse_tile)
        o_ref[0, :, cs] = o.astype(o_ref.dtype)

    if not merge:
        l_ref[0] = lse_tile
    kp_ref[...] = k_ref[0].astype(BF16)
    vp_ref[...] = v_ref[0].astype(BF16)


def _dilated_prompt(q, kv, gi, *, batch, seq, prev=None, name):
    win, dil = DIL_PAIRS[gi]
    blk = win // dil
    H = DIL_HEADS
    width = q.shape[1] // N_GROUPS
    hd = width // H
    assert seq % (blk * dil) == 0
    sd = seq // dil
    nb = sd // blk
    qv = q.reshape(batch, sd, dil * q.shape[1])
    kvv = kv.reshape(batch, sd, dil * kv.shape[1])
    qpr, kpr = N_GROUPS, 2 * N_GROUPS
    in_specs = [
        pl.BlockSpec((1, blk, width), lambda b, r, n: (b, n, r * qpr + gi)),
        pl.BlockSpec((1, blk, width), lambda b, r, n: (b, n, r * kpr + 2 * gi)),
        pl.BlockSpec((1, blk, width), lambda b, r, n: (b, n, r * kpr + 2 * gi + 1)),
    ]
    args = [qv, kvv, kvv]
    o_spec = pl.BlockSpec((1, blk, width), lambda b, r, n: (b, n, r))
    l_spec = pl.BlockSpec((1, blk, LANES), lambda b, r, n: (b, n, r))
    o_shape = jax.ShapeDtypeStruct((batch, sd, dil * width), BF16)
    l_shape = jax.ShapeDtypeStruct((batch, sd, dil * LANES), F32)
    merge = prev is not None
    if merge:
        (o0, l0), (o1, l1) = prev
        in_specs += [o_spec, o_spec, l_spec, l_spec]
        args += [o0.reshape(o_shape.shape), o1.reshape(o_shape.shape),
                 l0.reshape(l_shape.shape), l1.reshape(l_shape.shape)]
        out_specs, out_shape = o_spec, o_shape
    else:
        out_specs, out_shape = [o_spec, l_spec], [o_shape, l_shape]
    res = pl.pallas_call(
        functools.partial(_dilated_prompt_kernel, heads=H, hd=hd, merge=merge),
        grid=(batch, dil, nb),
        in_specs=in_specs,
        out_specs=out_specs,
        out_shape=out_shape,
        scratch_shapes=[pltpu.VMEM((blk, width), BF16), pltpu.VMEM((blk, width), BF16)],
        compiler_params=_cparams("parallel", "parallel", "arbitrary"),
        name=name,
    )(*args)
    if merge:
        return res.reshape(batch * seq, width)
    return res[0].reshape(batch * seq, width), res[1].reshape(batch * seq, LANES)


def _dilated_sample_kernel(q_ref, kn_ref, vn_ref, k0_ref, v0_ref, k1_ref, v1_ref, k2_ref, v2_ref,
                           o_ref, *, ts, hd):
    caches = ((k0_ref, v0_ref), (k1_ref, v1_ref), (k2_ref, v2_ref))
    scale = hd ** -0.5
    for t in range(ts):
        outs, lses = [], []
        for gi, (win, dil) in enumerate(DIL_PAIRS):
            kc_ref, vc_ref = caches[gi]
            rows = kc_ref.shape[1]
            q = q_ref[0, t, gi]
            r = t % dil if dil > 1 else 0
            kc = kc_ref[0, :, r]
            vc = vc_ref[0, :, r]
            s_c = jnp.sum(kc * q[None], axis=-1, keepdims=True) * scale
            if dil == 1:
                cidx = lax.broadcasted_iota(jnp.int32, s_c.shape, 0)
                s_c = jnp.where(cidx >= t, s_c, NEG)
                new_rows = list(range(t + 1))
            else:
                new_rows = [t]
            s_n = [jnp.sum(kn_ref[0, u, gi] * q, axis=-1, keepdims=True) * scale for u in new_rows]
            m = jnp.max(s_c, axis=0)
            for sn in s_n:
                m = jnp.maximum(m, sn)
            p_c = jnp.exp(s_c - m[None])
            p_n = [jnp.exp(sn - m) for sn in s_n]
            den = jnp.sum(p_c, axis=0)
            for pn in p_n:
                den = den + pn
            inv = 1.0 / den
            o = jnp.sum((p_c * inv[None]) * vc, axis=0)
            for u, pn in zip(new_rows, p_n):
                o = o + (pn * inv) * vn_ref[0, u, gi]
            outs.append(o)
            lses.append(m + jnp.log(den))
        mx = jnp.maximum(jnp.maximum(lses[0], lses[1]), lses[2])
        es = [jnp.exp(l - mx) for l in lses]
        tot = 1.0 / (es[0] + es[1] + es[2])
        o_ref[0, t] = (es[0] * tot) * outs[0] + (es[1] * tot) * outs[1] + (es[2] * tot) * outs[2]


def _dilated_sample(q, knew, vnew, caches_k, caches_v, *, name):
    B, ts, G, H, hd = q.shape
    small = pl.BlockSpec((1, ts, G, H, hd), lambda b: (b, 0, 0, 0, 0))
    in_specs = [small, small, small]
    args = [q, knew, vnew]
    for gi, (win, dil) in enumerate(DIL_PAIRS):
        L = caches_k[gi].shape[1]
        assert L == win and L % dil == 0 and (dil == 1 or ts <= dil)
        nres = min(dil, ts)
        spec = pl.BlockSpec((1, L // dil, nres, H, hd), lambda b: (b, 0, 0, 0, 0))
        for c in (caches_k[gi], caches_v[gi]):
            in_specs.append(spec)
            args.append(c.reshape(B, L // dil, dil, H, hd))
    return pl.pallas_call(
        functools.partial(_dilated_sample_kernel, ts=ts, hd=hd),
        grid=(B,),
        in_specs=in_specs,
        out_specs=pl.BlockSpec((1, ts, H, hd), lambda b: (b, 0, 0, 0)),
        out_shape=jax.ShapeDtypeStruct((B, ts, H, hd), F32),
        compiler_params=_cparams("parallel"),
        name=name,
    )(*args)


def _shift_kernel(*refs, n, ts):
    caches, news, outs, sem = refs[:n], refs[n:2 * n], refs[2 * n:3 * n], refs[3 * n]
    B = caches[0].shape[0]

    def copies(b):
        cps = []
        for i in range(n):
            L = caches[i].shape[1]
            cps.append(pltpu.make_async_copy(caches[i].at[b, pl.ds(ts, L - ts)],
                                             outs[i].at[b, pl.ds(0, L - ts)], sem.at[i, 0]))
            cps.append(pltpu.make_async_copy(news[i].at[b], outs[i].at[b, pl.ds(L - ts, ts)],
                                             sem.at[i, 1]))
        return cps

    def start(b, carry):
        for cp in copies(b):
            cp.start()
        return carry

    def wait(b, carry):
        for cp in copies(b):
            cp.wait()
        return carry

    lax.fori_loop(0, B, start, 0)
    lax.fori_loop(0, B, wait, 0)


def _shift_caches(caches, news, *, name):
    n = len(caches)
    ts = news[0].shape[1]
    any_spec = pl.BlockSpec(memory_space=pl.ANY)
    return pl.pallas_call(
        functools.partial(_shift_kernel, n=n, ts=ts),
        in_specs=[any_spec] * (2 * n),
        out_specs=[any_spec] * n,
        out_shape=[jax.ShapeDtypeStruct(c.shape, c.dtype) for c in caches],
        scratch_shapes=[pltpu.SemaphoreType.DMA((n, 2))],
        name=name,
    )(*caches, *news)


def _swiglu_ffn(h, gain, w1, w3, w2, *, tm, tag):
    act = _linear(h, [w1, w3], gain=gain, swiglu=True, out_dtype=BF16, tm=tm, tn=256,
                  name=f"ffn_up_{tag}")
    return _linear(act, [w2], residual=h, out_dtype=F32, tm=tm, tn=256, name=f"ffn_down_{tag}")


def kernel(x_prompt, x_sample, state_ret, cache_k_w128, cache_v_w128, cache_k_w512, cache_v_w512,
           cache_k_w2048, cache_v_w2048, norm_mix, norm_ffn, ret_w_in, ret_gn, ret_w_out,
           kv_norm, w_kv, dil_w_q, dil_w_o, ffn_w1, ffn_w3, ffn_w2, norm_final):
    Bp, Tp, D = x_prompt.shape
    Bs, Ts, _ = x_sample.shape
    H, hd = DIL_HEADS, D // DIL_HEADS
    caches_k = [cache_k_w128, cache_k_w512, cache_k_w2048]
    caches_v = [cache_v_w128, cache_v_w512, cache_v_w2048]
    TM = 1024

    h_p = x_prompt.reshape(Bp * Tp, D)
    h_s = jnp.pad(x_sample, ((0, 0), (0, SAMPLE_PAD - Ts), (0, 0))).reshape(Bs * SAMPLE_PAD, D)

    chunk_p = RET_CHUNK if Tp % RET_CHUNK == 0 else Tp
    proj_p = _linear(h_p, [ret_w_in[0]], gain=norm_mix[0], out_dtype=BF16, tm=TM, tn=512, name="ret_in_p")
    proj_s = _linear(h_s, [ret_w_in[0]], gain=norm_mix[0], out_dtype=BF16, tm=TM, tn=512, name="ret_in_s")
    s0_p = jnp.zeros((Bp,) + state_ret.shape[2:], F32)
    g_p, sp = _retention(proj_p, s0_p, ret_gn[0], batch=Bp, seq=Tp, chunk=chunk_p, chunk_true=chunk_p,
                         pos=jnp.arange(Tp), name="retention_p")
    g_s, ss = _retention(proj_s, state_ret[0], ret_gn[0], batch=Bs, seq=SAMPLE_PAD, chunk=SAMPLE_PAD,
                         chunk_true=Ts, pos=PAST_LEN + jnp.arange(SAMPLE_PAD), name="retention_s")
    h_p = _linear(g_p, [ret_w_out[0]], residual=h_p, out_dtype=F32, tm=TM, tn=512, name="ret_out_p")
    h_s = _linear(g_s, [ret_w_out[0]], residual=h_s, out_dtype=F32, tm=TM, tn=512, name="ret_out_s")
    h_p = _swiglu_ffn(h_p, norm_ffn[0], ffn_w1[0], ffn_w3[0], ffn_w2[0], tm=TM, tag="p0")
    h_s = _swiglu_ffn(h_s, norm_ffn[0], ffn_w1[0], ffn_w3[0], ffn_w2[0], tm=TM, tag="s0")

    kv_p = _linear(h_p, [w_kv], gain=kv_norm, out_dtype=F32, tm=TM, tn=512, name="kv_p")
    kv_s = _linear(h_s, [w_kv], gain=kv_norm, out_dtype=F32, tm=TM, tn=512, name="kv_s")
    kv_p6 = kv_p.reshape(Bp, Tp, N_GROUPS, 2, H, hd)
    kv_s6 = kv_s.reshape(Bs, SAMPLE_PAD, N_GROUPS, 2, H, hd)[:, :Ts]
    new_kp = [kv_p6[:, -min(DIL_PAIRS[i][0], Tp):, i, 0] for i in range(N_GROUPS)]
    new_vp = [kv_p6[:, -min(DIL_PAIRS[i][0], Tp):, i, 1] for i in range(N_GROUPS)]
    knew, vnew = kv_s6[:, :, :, 0], kv_s6[:, :, :, 1]
    shifted = _shift_caches(
        [c for pair in zip(caches_k, caches_v) for c in pair],
        [a[:, :, i] for i in range(N_GROUPS) for a in (knew, vnew)], name="window_shift")

    q_p = _linear(h_p, [dil_w_q[0]], gain=norm_mix[1], out_dtype=BF16, tm=TM, tn=512, name="dil_q_p")
    q_s = _linear(h_s, [dil_w_q[0]], gain=norm_mix[1], out_dtype=F32, tm=TM, tn=512, name="dil_q_s")
    br0 = _dilated_prompt(q_p, kv_p, 0, batch=Bp, seq=Tp, name="dilated_p0")
    br1 = _dilated_prompt(q_p, kv_p, 1, batch=Bp, seq=Tp, name="dilated_p1")
    a_p = _dilated_prompt(q_p, kv_p, 2, batch=Bp, seq=Tp, prev=(br0, br1), name="dilated_p2")
    q_s5 = q_s.reshape(Bs, SAMPLE_PAD, N_GROUPS, H, hd)[:, :Ts]
    a_s = _dilated_sample(q_s5, knew, vnew, caches_k, caches_v, name="dilated_s")
    a_s = jnp.pad(a_s.reshape(Bs, Ts, D), ((0, 0), (0, SAMPLE_PAD - Ts), (0, 0))).reshape(Bs * SAMPLE_PAD, D)
    h_p = _linear(a_p, [dil_w_o[0]], residual=h_p, out_dtype=F32, tm=TM, tn=512, name="dil_o_p")
    h_s = _linear(a_s, [dil_w_o[0]], residual=h_s, out_dtype=F32, tm=TM, tn=512, name="dil_o_s")
    h_p = _swiglu_ffn(h_p, norm_ffn[1], ffn_w1[1], ffn_w3[1], ffn_w2[1], tm=TM, tag="p1")
    h_s = _swiglu_ffn(h_s, norm_ffn[1], ffn_w1[1], ffn_w3[1], ffn_w2[1], tm=TM, tag="s1")

    y_p = _rmsnorm(h_p, norm_final, tm=512, name="final_norm_p").reshape(Bp, Tp, D)
    y_s = _rmsnorm(h_s, norm_final, tm=512, name="final_norm_s").reshape(Bs, SAMPLE_PAD, D)[:, :Ts]
    return (y_p, y_s, sp[None], ss[None],
            new_kp[0], new_vp[0], new_kp[1], new_vp[1], new_kp[2], new_vp[2],
            shifted[0], shifted[1], shifted[2], shifted[3], shifted[4], shifted[5])
```

```python
import functools

import numpy as np
import jax
import jax.numpy as jnp
from jax import lax
from jax.experimental import pallas as pl
from jax.experimental.pallas import tpu as pltpu

F32 = jnp.float32
BF16 = jnp.bfloat16

RET_HEADS = 8
RET_CHUNK = 128
ROPE_BASE = 10000.0
DIL_PAIRS = ((128, 1), (512, 4), (2048, 16))
N_GROUPS = len(DIL_PAIRS)
DIL_HEADS = 16
PAST_LEN = 8192
EPS = 1e-6
NEG = -1e30

LANES = 128
SAMPLE_PAD = 16
VMEM_LIMIT = 56 * 1024 * 1024


def _cparams(*sem):
    return pltpu.CompilerParams(dimension_semantics=sem, vmem_limit_bytes=VMEM_LIMIT)


def _linear_kernel(*refs, n_w, has_gain, has_res, swiglu, norm_rows):
    it = iter(refs)
    x_ref = next(it)
    g_ref = next(it) if has_gain else None
    w_refs = [next(it) for _ in range(n_w)]
    r_ref = next(it) if has_res else None
    o_ref = next(it)
    xn_ref = next(it) if has_gain else None

    if has_gain:
        @pl.when(pl.program_id(1) == 0)
        def _():
            gain = g_ref[...]

            def body(c, carry):
                rows = pl.ds(pl.multiple_of(c * norm_rows, norm_rows), norm_rows)
                xf = x_ref[rows, :].astype(F32)
                ms = jnp.mean(xf * xf, axis=-1, keepdims=True)
                xn_ref[rows, :] = (xf * lax.rsqrt(ms + EPS) * gain).astype(BF16)
                return carry

            lax.fori_loop(0, x_ref.shape[0] // norm_rows, body, 0)

        xn = xn_ref[...]
    else:
        xn = x_ref[...].astype(BF16)

    a = jnp.dot(xn, w_refs[0][...].astype(BF16), preferred_element_type=F32)
    if swiglu:
        b = jnp.dot(xn, w_refs[1][...].astype(BF16), preferred_element_type=F32)
        a = (a * jax.nn.sigmoid(a)) * b
    if has_res:
        a = a + r_ref[...]
    o_ref[...] = a.astype(o_ref.dtype)


def _linear(x, ws, *, gain=None, residual=None, swiglu=False, out_dtype, tm, tn, name):
    M, K = x.shape
    N = ws[0].shape[1]
    tm = min(tm, M)
    assert M % tm == 0 and N % tn == 0
    has_gain = gain is not None
    has_res = residual is not None
    norm_rows = min(64, tm)
    in_specs = [pl.BlockSpec((tm, K), lambda i, j: (i, 0))]
    args = [x]
    if has_gain:
        in_specs.append(pl.BlockSpec((1, K), lambda i, j: (0, 0)))
        args.append(gain.reshape(1, K).astype(F32))
    for w in ws:
        in_specs.append(pl.BlockSpec((K, tn), lambda i, j: (0, j)))
        args.append(w)
    if has_res:
        in_specs.append(pl.BlockSpec((tm, tn), lambda i, j: (i, j)))
        args.append(residual)
    scratch = [pltpu.VMEM((tm, K), BF16)] if has_gain else []
    kern = functools.partial(_linear_kernel, n_w=len(ws), has_gain=has_gain, has_res=has_res,
                             swiglu=swiglu, norm_rows=norm_rows)
    return pl.pallas_call(
        kern,
        grid=(M // tm, N // tn),
        in_specs=in_specs,
        out_specs=pl.BlockSpec((tm, tn), lambda i, j: (i, j)),
        out_shape=jax.ShapeDtypeStruct((M, N), out_dtype),
        scratch_shapes=scratch,
        compiler_params=_cparams("parallel", "arbitrary"),
        name=name,
    )(*args)


def _rmsnorm_kernel(x_ref, g_ref, o_ref):
    xf = x_ref[...]
    ms = jnp.mean(xf * xf, axis=-1, keepdims=True)
    o_ref[...] = xf * lax.rsqrt(ms + EPS) * g_ref[...]


def _rmsnorm(x, gain, *, tm, name):
    M, K = x.shape
    tm = min(tm, M)
    return pl.pallas_call(
        _rmsnorm_kernel,
        grid=(M // tm,),
        in_specs=[pl.BlockSpec((tm, K), lambda i: (i, 0)), pl.BlockSpec((1, K), lambda i: (0, 0))],
        out_specs=pl.BlockSpec((tm, K), lambda i: (i, 0)),
        out_shape=jax.ShapeDtypeStruct((M, K), F32),
        compiler_params=_cparams("parallel"),
        name=name,
    )(x, gain.reshape(1, K))


def _retention_kernel(q_ref, k_ref, v_ref, g_ref, cos_ref, sin_ref, dmat_ref, qdec_ref, kdec_ref,
                      cdec_ref, gn_ref, s0_ref, o_ref, s_ref, *, dk):
    @pl.when(pl.program_id(2) == 0)
    def _():
        s_ref[...] = s0_ref[...]

    cos = cos_ref[...]
    sin = sin_ref[...]
    half = dk // 2

    def rope(x):
        x1 = x[:, :half]
        x2 = x[:, half:]
        return jnp.concatenate([x1 * cos - x2 * sin, x1 * sin + x2 * cos], axis=-1)

    q = rope(q_ref[...].astype(F32))
    k = rope(k_ref[...].astype(F32)) * (dk ** -0.5)
    v = v_ref[...]
    s = s_ref[0, 0]

    sc = lax.dot_general(q.astype(BF16), k.astype(BF16), (((1,), (1,)), ((), ())),
                         preferred_element_type=F32) * dmat_ref[0]
    o = (jnp.dot(sc.astype(BF16), v, preferred_element_type=F32)
         + jnp.dot((q * qdec_ref[0]).astype(BF16), s.astype(BF16), preferred_element_type=F32))
    kd = (k * kdec_ref[0]).astype(BF16)
    s_ref[0, 0] = s * cdec_ref[0] + lax.dot_general(kd, v, (((0,), (0,)), ((), ())),
                                                   preferred_element_type=F32)

    mu = jnp.mean(o, axis=-1, keepdims=True)
    var = jnp.mean(jnp.square(o - mu), axis=-1, keepdims=True)
    on = (o - mu) * lax.rsqrt(var + EPS) * gn_ref[...]
    gate = g_ref[...].astype(F32)
    o_ref[...] = ((gate * jax.nn.sigmoid(gate)) * on).astype(o_ref.dtype)


def _retention(proj, s0, gn_g, *, batch, seq, chunk, chunk_true, pos, name):
    H = RET_HEADS
    dk, dv = s0.shape[-2], s0.shape[-1]
    nc = seq // chunk
    lg = jnp.log1p(-jnp.exp2(-5.0 - jnp.arange(H, dtype=F32)))
    i = jnp.arange(chunk, dtype=F32)
    dist = i[:, None] - i[None, :]
    dmat = jnp.where(dist >= 0, jnp.exp(jnp.maximum(dist, 0.0)[None] * lg[:, None, None]), 0.0)
    qdec = jnp.exp((i + 1.0)[None, :, None] * lg[:, None, None])
    kdec = jnp.exp((chunk_true - 1.0 - i)[None, :, None] * lg[:, None, None])
    cdec = jnp.exp(chunk_true * lg).reshape(H, 1, 1)
    half = dk // 2
    inv = ROPE_BASE ** (-jnp.arange(half, dtype=F32) / half)
    ang = pos.astype(F32)[:, None] * inv[None, :]
    cos, sin = jnp.cos(ang), jnp.sin(ang)

    qb, vb = (H * dk) // dk, (2 * H * dk) // dv
    gb = vb + H
    row = lambda b, h, c: b * nc + c
    in_specs = [
        pl.BlockSpec((chunk, dk), lambda b, h, c: (row(b, h, c), h)),
        pl.BlockSpec((chunk, dk), lambda b, h, c: (row(b, h, c), qb + h)),
        pl.BlockSpec((chunk, dv), lambda b, h, c: (row(b, h, c), vb + h)),
        pl.BlockSpec((chunk, dv), lambda b, h, c: (row(b, h, c), gb + h)),
        pl.BlockSpec((chunk, half), lambda b, h, c: (c, 0)),
        pl.BlockSpec((chunk, half), lambda b, h, c: (c, 0)),
        pl.BlockSpec((1, chunk, chunk), lambda b, h, c: (h, 0, 0)),
        pl.BlockSpec((1, chunk, 1), lambda b, h, c: (h, 0, 0)),
        pl.BlockSpec((1, chunk, 1), lambda b, h, c: (h, 0, 0)),
        pl.BlockSpec((1, 1, 1), lambda b, h, c: (h, 0, 0)),
        pl.BlockSpec((1, dv), lambda b, h, c: (0, h)),
        pl.BlockSpec((1, 1, dk, dv), lambda b, h, c: (b, h, 0, 0)),
    ]
    out_specs = [
        pl.BlockSpec((chunk, dv), lambda b, h, c: (row(b, h, c), h)),
        pl.BlockSpec((1, 1, dk, dv), lambda b, h, c: (b, h, 0, 0)),
    ]
    out_shape = [jax.ShapeDtypeStruct((batch * seq, H * dv), BF16),
                 jax.ShapeDtypeStruct((batch, H, dk, dv), F32)]
    return pl.pallas_call(
        functools.partial(_retention_kernel, dk=dk),
        grid=(batch, H, nc),
        in_specs=in_specs,
        out_specs=out_specs,
        out_shape=out_shape,
        compiler_params=_cparams("parallel", "parallel", "arbitrary"),
        name=name,
    )(proj, proj, proj, proj, cos, sin, dmat, qdec, kdec, cdec, gn_g.reshape(1, H * dv), s0)


def _dilated_prompt_kernel(*refs, heads, hd, merge):
    if merge:
        (q_ref, k_ref, v_ref, o0_ref, o1_ref, l0_ref, l1_ref, o_ref, kp_ref, vp_ref) = refs
    else:
        (q_ref, k_ref, v_ref, o_ref, l_ref, kp_ref, vp_ref) = refs
    n = pl.program_id(2)
    blk = q_ref.shape[1]

    @pl.when(n == 0)
    def _():
        kp_ref[...] = jnp.zeros_like(kp_ref)
        vp_ref[...] = jnp.zeros_like(vp_ref)

    row = lax.broadcasted_iota(jnp.int32, (blk, blk), 0)
    col = lax.broadcasted_iota(jnp.int32, (blk, blk), 1)
    mask_c = col <= row
    mask_p = (col - row) >= jnp.where(n > 0, 0, blk)
    lane = lax.broadcasted_iota(jnp.int32, (blk, LANES), 1)
    scale = hd ** -0.5
    nt = (((1,), (1,)), ((), ()))
    lse_tile = jnp.zeros((blk, LANES), F32)

    for h in range(heads):
        cs = slice(h * hd, (h + 1) * hd)
        qh = q_ref[0, :, cs]
        kc = k_ref[0, :, cs].astype(BF16)
        vc = v_ref[0, :, cs].astype(BF16)
        kp = kp_ref[:, cs]
        vp = vp_ref[:, cs]
        s_c = jnp.where(mask_c, lax.dot_general(qh, kc, nt, preferred_element_type=F32) * scale, NEG)
        s_p = jnp.where(mask_p, lax.dot_general(qh, kp, nt, preferred_element_type=F32) * scale, NEG)
        m = jnp.maximum(jnp.max(s_c, axis=-1, keepdims=True), jnp.max(s_p, axis=-1, keepdims=True))
        p_c = jnp.exp(s_c - m)
        p_p = jnp.exp(s_p - m)
        den = jnp.sum(p_c, axis=-1, keepdims=True) + jnp.sum(p_p, axis=-1, keepdims=True)
        inv = 1.0 / den
        o = (jnp.dot((p_c * inv).astype(BF16), vc, preferred_element_type=F32)
             + jnp.dot((p_p * inv).astype(BF16), vp, preferred_element_type=F32))
        lse = m + jnp.log(den)
        if merge:
            l0 = l0_ref[0, :, h:h + 1]
            l1 = l1_ref[0, :, h:h + 1]
            mx = jnp.maximum(jnp.maximum(l0, l1), lse)
            e0 = jnp.exp(l0 - mx)
            e1 = jnp.exp(l1 - mx)
            e2 = jnp.exp(lse - mx)
            tot = 1.0 / (e0 + e1 + e2)
            o = ((e0 * tot) * o0_ref[0, :, cs].astype(F32) + (e1 * tot) * o1_ref[0, :, cs].astype(F32)
                 + (e2 * tot) * o)
        else:
            lse_tile = jnp.where(lane == h, lse, lse_tile)
        o_ref[0, :, cs] = o.astype(o_ref.dtype)

    if not merge:
        l_ref[0] = lse_tile
    kp_ref[...] = k_ref[0].astype(BF16)
    vp_ref[...] = v_ref[0].astype(BF16)


def _dilated_prompt(q, kv, gi, *, batch, seq, prev=None, name):
    win, dil = DIL_PAIRS[gi]
    blk = win // dil
    H = DIL_HEADS
    width = q.shape[1] // N_GROUPS
    hd = width // H
    assert seq % (blk * dil) == 0
    sd = seq // dil
    nb = sd // blk
    qv = q.reshape(batch, sd, dil * q.shape[1])
    kvv = kv.reshape(batch, sd, dil * kv.shape[1])
    qpr, kpr = N_GROUPS, 2 * N_GROUPS
    in_specs = [
        pl.BlockSpec((1, blk, width), lambda b, r, n: (b, n, r * qpr + gi)),
        pl.BlockSpec((1, blk, width), lambda b, r, n: (b, n, r * kpr + 2 * gi)),
        pl.BlockSpec((1, blk, width), lambda b, r, n: (b, n, r * kpr + 2 * gi + 1)),
    ]
    args = [qv, kvv, kvv]
    o_spec = pl.BlockSpec((1, blk, width), lambda b, r, n: (b, n, r))
    l_spec = pl.BlockSpec((1, blk, LANES), lambda b, r, n: (b, n, r))
    o_shape = jax.ShapeDtypeStruct((batch, sd, dil * width), BF16)
    l_shape = jax.ShapeDtypeStruct((batch, sd, dil * LANES), F32)
    merge = prev is not None
    if merge:
        (o0, l0), (o1, l1) = prev
        in_specs += [o_spec, o_spec, l_spec, l_spec]
        args += [o0.reshape(o_shape.shape), o1.reshape(o_shape.shape),
                 l0.reshape(l_shape.shape), l1.reshape(l_shape.shape)]
        out_specs, out_shape = o_spec, o_shape
    else:
        out_specs, out_shape = [o_spec, l_spec], [o_shape, l_shape]
    res = pl.pallas_call(
        functools.partial(_dilated_prompt_kernel, heads=H, hd=hd, merge=merge),
        grid=(batch, dil, nb),
        in_specs=in_specs,
        out_specs=out_specs,
        out_shape=out_shape,
        scratch_shapes=[pltpu.VMEM((blk, width), BF16), pltpu.VMEM((blk, width), BF16)],
        compiler_params=_cparams("parallel", "parallel", "arbitrary"),
        name=name,
    )(*args)
    if merge:
        return res.reshape(batch * seq, width)
    return res[0].reshape(batch * seq, width), res[1].reshape(batch * seq, LANES)


def _dilated_sample_kernel(q_ref, kn_ref, vn_ref, k0_ref, v0_ref, k1_ref, v1_ref, k2_ref, v2_ref,
                           o_ref, *, ts, hd):
    caches = ((k0_ref, v0_ref), (k1_ref, v1_ref), (k2_ref, v2_ref))
    scale = hd ** -0.5
    for t in range(ts):
        outs, lses = [], []
        for gi, (win, dil) in enumerate(DIL_PAIRS):
            kc_ref, vc_ref = caches[gi]
            rows = kc_ref.shape[1]
            q = q_ref[0, t, gi]
            r = t % dil if dil > 1 else 0
            kc = kc_ref[0, :, r]
            vc = vc_ref[0, :, r]
            s_c = jnp.sum(kc * q[None], axis=-1, keepdims=True) * scale
            if dil == 1:
                cidx = lax.broadcasted_iota(jnp.int32, s_c.shape, 0)
                s_c = jnp.where(cidx >= t, s_c, NEG)
                new_rows = list(range(t + 1))
            else:
                new_rows = [t]
            s_n = [jnp.sum(kn_ref[0, u, gi] * q, axis=-1, keepdims=True) * scale for u in new_rows]
            m = jnp.max(s_c, axis=0)
            for sn in s_n:
                m = jnp.maximum(m, sn)
            p_c = jnp.exp(s_c - m[None])
            p_n = [jnp.exp(sn - m) for sn in s_n]
            den = jnp.sum(p_c, axis=0)
            for pn in p_n:
                den = den + pn
            inv = 1.0 / den
            o = jnp.sum((p_c * inv[None]) * vc, axis=0)
            for u, pn in zip(new_rows, p_n):
                o = o + (pn * inv) * vn_ref[0, u, gi]
            outs.append(o)
            lses.append(m + jnp.log(den))
        mx = jnp.maximum(jnp.maximum(lses[0], lses[1]), lses[2])
        es = [jnp.exp(l - mx) for l in lses]
        tot = 1.0 / (es[0] + es[1] + es[2])
        o_ref[0, t] = (es[0] * tot) * outs[0] + (es[1] * tot) * outs[1] + (es[2] * tot) * outs[2]


def _dilated_sample(q, knew, vnew, caches_k, caches_v, *, name):
    B, ts, G, H, hd = q.shape
    small = pl.BlockSpec((1, ts, G, H, hd), lambda b: (b, 0, 0, 0, 0))
    in_specs = [small, small, small]
    args = [q, knew, vnew]
    for gi, (win, dil) in enumerate(DIL_PAIRS):
        L = caches_k[gi].shape[1]
        assert L == win and L % dil == 0 and (dil == 1 or ts <= dil)
        nres = min(dil, ts)
        spec = pl.BlockSpec((1, L // dil, nres, H, hd), lambda b: (b, 0, 0, 0, 0))
        for c in (caches_k[gi], caches_v[gi]):
            in_specs.append(spec)
            args.append(c.reshape(B, L // dil, dil, H, hd))
    return pl.pallas_call(
        functools.partial(_dilated_sample_kernel, ts=ts, hd=hd),
        grid=(B,),
        in_specs=in_specs,
        out_specs=pl.BlockSpec((1, ts, H, hd), lambda b: (b, 0, 0, 0)),
        out_shape=jax.ShapeDtypeStruct((B, ts, H, hd), F32),
        compiler_params=_cparams("parallel"),
        name=name,
    )(*args)


SHIFT_ROWS = 512


def _shift_kernel(*refs, n, ts, chunks):
    caches, new_ref, outs = refs[:n], refs[n], refs[n + 1:2 * n + 1]
    buf, in_sem, out_sem, new_sem = refs[2 * n + 1:]
    B = caches[0].shape[0]
    nch = len(chunks)
    assert nch % 2 == 0 and nch >= 2

    def in_copy(b, k):
        i, r0, nr = chunks[k]
        return pltpu.make_async_copy(caches[i].at[b, pl.ds(ts + r0, nr)],
                                     buf.at[k % 2, pl.ds(0, nr)], in_sem.at[k % 2])

    def out_copy(b, k):
        i, r0, nr = chunks[k]
        return pltpu.make_async_copy(buf.at[k % 2, pl.ds(0, nr)],
                                     outs[i].at[b, pl.ds(r0, nr)], out_sem.at[k % 2])

    def new_copy(b, i):
        L = caches[i].shape[1]
        return pltpu.make_async_copy(new_ref.at[i, b], outs[i].at[b, pl.ds(L - ts, ts)], new_sem)

    def body(b, carry):
        for i in range(n):
            new_copy(b, i).start()
        for k in range(nch):
            if k >= 2:
                out_copy(b, k - 2).wait()
            else:
                @pl.when(b > 0)
                def _():
                    out_copy(b - 1, nch - 2 + k).wait()
            cp = in_copy(b, k)
            cp.start()
            cp.wait()
            out_copy(b, k).start()
        return carry

    lax.fori_loop(0, B, body, 0)
    out_copy(B - 1, nch - 2).wait()
    out_copy(B - 1, nch - 1).wait()

    def drain(b, carry):
        for i in range(n):
            new_copy(b, i).wait()
        return carry

    lax.fori_loop(0, B, drain, 0)


def _shift_caches(caches, news, *, name):
    n = len(caches)
    ts = news[0].shape[1]
    chunks = []
    for i, c in enumerate(caches):
        keep = c.shape[1] - ts
        chunks += [(i, r0, min(SHIFT_ROWS, keep - r0)) for r0 in range(0, keep, SHIFT_ROWS)]
    any_spec = pl.BlockSpec(memory_space=pl.ANY)
    return pl.pallas_call(
        functools.partial(_shift_kernel, n=n, ts=ts, chunks=tuple(chunks)),
        in_specs=[any_spec] * n + [pl.BlockSpec(memory_space=pltpu.VMEM)],
        out_specs=[any_spec] * n,
        out_shape=[jax.ShapeDtypeStruct(c.shape, c.dtype) for c in caches],
        scratch_shapes=[pltpu.VMEM((2, SHIFT_ROWS) + caches[0].shape[2:], caches[0].dtype),
                        pltpu.SemaphoreType.DMA((2,)), pltpu.SemaphoreType.DMA((2,)),
                        pltpu.SemaphoreType.DMA(())],
        compiler_params=pltpu.CompilerParams(vmem_limit_bytes=VMEM_LIMIT),
        name=name,
    )(*caches, jnp.stack(news))


def _swiglu_ffn(h, gain, w1, w3, w2, *, tm, tag):
    act = _linear(h, [w1, w3], gain=gain, swiglu=True, out_dtype=BF16, tm=tm, tn=256,
                  name=f"ffn_up_{tag}")
    return _linear(act, [w2], residual=h, out_dtype=F32, tm=tm, tn=256, name=f"ffn_down_{tag}")


def kernel(x_prompt, x_sample, state_ret, cache_k_w128, cache_v_w128, cache_k_w512, cache_v_w512,
           cache_k_w2048, cache_v_w2048, norm_mix, norm_ffn, ret_w_in, ret_gn, ret_w_out,
           kv_norm, w_kv, dil_w_q, dil_w_o, ffn_w1, ffn_w3, ffn_w2, norm_final):
    Bp, Tp, D = x_prompt.shape
    Bs, Ts, _ = x_sample.shape
    H, hd = DIL_HEADS, D // DIL_HEADS
    caches_k = [cache_k_w128, cache_k_w512, cache_k_w2048]
    caches_v = [cache_v_w128, cache_v_w512, cache_v_w2048]
    TM = 1024

    h_p = x_prompt.reshape(Bp * Tp, D)
    h_s = jnp.pad(x_sample, ((0, 0), (0, SAMPLE_PAD - Ts), (0, 0))).reshape(Bs * SAMPLE_PAD, D)

    chunk_p = RET_CHUNK if Tp % RET_CHUNK == 0 else Tp
    proj_p = _linear(h_p, [ret_w_in[0]], gain=norm_mix[0], out_dtype=BF16, tm=TM, tn=512, name="ret_in_p")
    proj_s = _linear(h_s, [ret_w_in[0]], gain=norm_mix[0], out_dtype=BF16, tm=TM, tn=512, name="ret_in_s")
    s0_p = jnp.zeros((Bp,) + state_ret.shape[2:], F32)
    g_p, sp = _retention(proj_p, s0_p, ret_gn[0], batch=Bp, seq=Tp, chunk=chunk_p, chunk_true=chunk_p,
                         pos=jnp.arange(Tp), name="retention_p")
    g_s, ss = _retention(proj_s, state_ret[0], ret_gn[0], batch=Bs, seq=SAMPLE_PAD, chunk=SAMPLE_PAD,
                         chunk_true=Ts, pos=PAST_LEN + jnp.arange(SAMPLE_PAD), name="retention_s")
    h_p = _linear(g_p, [ret_w_out[0]], residual=h_p, out_dtype=F32, tm=TM, tn=512, name="ret_out_p")
    h_s = _linear(g_s, [ret_w_out[0]], residual=h_s, out_dtype=F32, tm=TM, tn=512, name="ret_out_s")
    h_p = _swiglu_ffn(h_p, norm_ffn[0], ffn_w1[0], ffn_w3[0], ffn_w2[0], tm=TM, tag="p0")
    h_s = _swiglu_ffn(h_s, norm_ffn[0], ffn_w1[0], ffn_w3[0], ffn_w2[0], tm=TM, tag="s0")

    kv_p = _linear(h_p, [w_kv], gain=kv_norm, out_dtype=F32, tm=TM, tn=512, name="kv_p")
    kv_s = _linear(h_s, [w_kv], gain=kv_norm, out_dtype=F32, tm=TM, tn=512, name="kv_s")
    kv_p6 = kv_p.reshape(Bp, Tp, N_GROUPS, 2, H, hd)
    kv_s6 = kv_s.reshape(Bs, SAMPLE_PAD, N_GROUPS, 2, H, hd)[:, :Ts]
    new_kp = [kv_p6[:, -min(DIL_PAIRS[i][0], Tp):, i, 0] for i in range(N_GROUPS)]
    new_vp = [kv_p6[:, -min(DIL_PAIRS[i][0], Tp):, i, 1] for i in range(N_GROUPS)]
    knew, vnew = kv_s6[:, :, :, 0], kv_s6[:, :, :, 1]
    shifted = _shift_caches(
        [c for pair in zip(caches_k, caches_v) for c in pair],
        [a[:, :, i] for i in range(N_GROUPS) for a in (knew, vnew)], name="window_shift")

    q_p = _linear(h_p, [dil_w_q[0]], gain=norm_mix[1], out_dtype=BF16, tm=TM, tn=512, name="dil_q_p")
    q_s = _linear(h_s, [dil_w_q[0]], gain=norm_mix[1], out_dtype=F32, tm=TM, tn=512, name="dil_q_s")
    br0 = _dilated_prompt(q_p, kv_p, 0, batch=Bp, seq=Tp, name="dilated_p0")
    br1 = _dilated_prompt(q_p, kv_p, 1, batch=Bp, seq=Tp, name="dilated_p1")
    a_p = _dilated_prompt(q_p, kv_p, 2, batch=Bp, seq=Tp, prev=(br0, br1), name="dilated_p2")
    q_s5 = q_s.reshape(Bs, SAMPLE_PAD, N_GROUPS, H, hd)[:, :Ts]
    a_s = _dilated_sample(q_s5, knew, vnew, caches_k, caches_v, name="dilated_s")
    a_s = jnp.pad(a_s.reshape(Bs, Ts, D), ((0, 0), (0, SAMPLE_PAD - Ts), (0, 0))).reshape(Bs * SAMPLE_PAD, D)
    h_p = _linear(a_p, [dil_w_o[0]], residual=h_p, out_dtype=F32, tm=TM, tn=512, name="dil_o_p")
    h_s = _linear(a_s, [dil_w_o[0]], residual=h_s, out_dtype=F32, tm=TM, tn=512, name="dil_o_s")
    h_p = _swiglu_ffn(h_p, norm_ffn[1], ffn_w1[1], ffn_w3[1], ffn_w2[1], tm=TM, tag="p1")
    h_s = _swiglu_ffn(h_s, norm_ffn[1], ffn_w1[1], ffn_w3[1], ffn_w2[1], tm=TM, tag="s1")

    y_p = _rmsnorm(h_p, norm_final, tm=512, name="final_norm_p").reshape(Bp, Tp, D)
    y_s = _rmsnorm(h_s, norm_final, tm=512, name="final_norm_s").reshape(Bs, SAMPLE_PAD, D)[:, :Ts]
    return (y_p, y_s, sp[None], ss[None],
            new_kp[0], new_vp[0], new_kp[1], new_vp[1], new_kp[2], new_vp[2],
            shifted[0], shifted[1], shifted[2], shifted[3], shifted[4], shifted[5])
```

```python
import functools

import jax
import jax.numpy as jnp
from jax import lax
from jax.experimental import pallas as pl
from jax.experimental.pallas import tpu as pltpu

F32 = jnp.float32
BF16 = jnp.bfloat16

RET_HEADS = 8
RET_CHUNK = 128
ROPE_BASE = 10000.0
DIL_PAIRS = ((128, 1), (512, 4), (2048, 16))
N_GROUPS = len(DIL_PAIRS)
DIL_HEADS = 16
PAST_LEN = 8192
EPS = 1e-6
NEG = -1e30

LANES = 128
SAMPLE_PAD = 16
VMEM_LIMIT = 61 * 1024 * 1024
ROW_TILES = 4


def _cparams(*sem):
    return pltpu.CompilerParams(dimension_semantics=sem, vmem_limit_bytes=VMEM_LIMIT)


def _norm_kernel(x_ref, g_ref, *o_refs):
    xf = x_ref[...]
    ms = jnp.mean(xf * xf, axis=-1, keepdims=True)
    xs = xf * lax.rsqrt(ms + EPS)
    for k, o_ref in enumerate(o_refs):
        o_ref[...] = (xs * g_ref[k:k + 1, :]).astype(o_ref.dtype)


def _norm(x, gains, *, out_dtype, tm, row0=0, rows=None, name):
    M, K = x.shape
    rows = M if rows is None else rows
    assert rows % tm == 0 and row0 % tm == 0
    g = jnp.stack(gains).astype(F32)
    off = row0 // tm
    outs = pl.pallas_call(
        _norm_kernel,
        grid=(rows // tm,),
        in_specs=[pl.BlockSpec((tm, K), lambda i: (i + off, 0)),
                  pl.BlockSpec((len(gains), K), lambda i: (0, 0))],
        out_specs=[pl.BlockSpec((tm, K), lambda i: (i, 0))] * len(gains),
        out_shape=[jax.ShapeDtypeStruct((rows, K), out_dtype)] * len(gains),
        compiler_params=_cparams("parallel"),
        name=name,
    )(x, g)
    return outs


def _linear_kernel(*refs, n_w, has_res, swiglu):
    it = iter(refs)
    x_ref = next(it)
    w_refs = [next(it) for _ in range(n_w)]
    r_ref = next(it) if has_res else None
    o_ref = next(it)
    xn = x_ref[...]
    a = jnp.dot(xn, w_refs[0][...].astype(BF16), preferred_element_type=F32)
    if swiglu:
        b = jnp.dot(xn, w_refs[1][...].astype(BF16), preferred_element_type=F32)
        a = (a * jax.nn.sigmoid(a)) * b
    if has_res:
        a = a + r_ref[...]
    o_ref[...] = a.astype(o_ref.dtype)


def _linear(x, ws, *, residual=None, swiglu=False, out_dtype, tn, name):
    M, K = x.shape
    N = ws[0].shape[1]
    assert M % (ROW_TILES * 16) == 0 and N % tn == 0 and x.dtype == BF16
    tm = M // ROW_TILES
    has_res = residual is not None
    in_specs = [pl.BlockSpec((tm, K), lambda i, j: (i, 0), pipeline_mode=pl.Buffered(1))]
    args = [x]
    for w in ws:
        in_specs.append(pl.BlockSpec((K, tn), lambda i, j: (0, j)))
        args.append(w)
    if has_res:
        in_specs.append(pl.BlockSpec((tm, tn), lambda i, j: (i, j)))
        args.append(residual)
    kern = functools.partial(_linear_kernel, n_w=len(ws), has_res=has_res, swiglu=swiglu)
    return pl.pallas_call(
        kern,
        grid=(ROW_TILES, N // tn),
        in_specs=in_specs,
        out_specs=pl.BlockSpec((tm, tn), lambda i, j: (i, j)),
        out_shape=jax.ShapeDtypeStruct((M, N), out_dtype),
        compiler_params=_cparams("parallel", "arbitrary"),
        name=name,
    )(*args)


def _retention_kernel(*refs, dk, aliased):
    (q_ref, k_ref, v_ref, g_ref, cos_ref, sin_ref, dmat_ref, qdec_ref, kdec_ref,
     cdec_ref, gn_ref, s0_ref) = refs[:12]
    o_ref, s_ref = refs[-2:]
    del aliased

    @pl.when(pl.program_id(2) == 0)
    def _():
        s_ref[...] = s0_ref[...]

    cos = cos_ref[...]
    sin = sin_ref[...]
    half = dk // 2

    def rope(x):
        x1 = x[:, :half]
        x2 = x[:, half:]
        return jnp.concatenate([x1 * cos - x2 * sin, x1 * sin + x2 * cos], axis=-1)

    q = rope(q_ref[...].astype(F32))
    k = rope(k_ref[...].astype(F32)) * (dk ** -0.5)
    v = v_ref[...]
    s = s_ref[0, 0]

    sc = lax.dot_general(q.astype(BF16), k.astype(BF16), (((1,), (1,)), ((), ())),
                         preferred_element_type=F32) * dmat_ref[0]
    o = (jnp.dot(sc.astype(BF16), v, preferred_element_type=F32)
         + jnp.dot((q * qdec_ref[0]).astype(BF16), s.astype(BF16), preferred_element_type=F32))
    kd = (k * kdec_ref[0]).astype(BF16)
    s_ref[0, 0] = s * cdec_ref[0] + lax.dot_general(kd, v, (((0,), (0,)), ((), ())),
                                                   preferred_element_type=F32)

    mu = jnp.mean(o, axis=-1, keepdims=True)
    var = jnp.mean(jnp.square(o - mu), axis=-1, keepdims=True)
    on = (o - mu) * lax.rsqrt(var + EPS) * gn_ref[...]
    gate = g_ref[...].astype(F32)
    o_ref[...] = ((gate * jax.nn.sigmoid(gate)) * on).astype(o_ref.dtype)


def _retention(proj, s0, gn_g, *, row0, batch, seq, chunk, chunk_true, pos, out_init=None, name):
    H = RET_HEADS
    dk, dv = s0.shape[-2], s0.shape[-1]
    nc = seq // chunk
    assert row0 % chunk == 0
    lg = jnp.log1p(-jnp.exp2(-5.0 - jnp.arange(H, dtype=F32)))
    i = jnp.arange(chunk, dtype=F32)
    dist = i[:, None] - i[None, :]
    dmat = jnp.where(dist >= 0, jnp.exp(jnp.maximum(dist, 0.0)[None] * lg[:, None, None]), 0.0)
    qdec = jnp.exp((i + 1.0)[None, :, None] * lg[:, None, None])
    kdec = jnp.exp((chunk_true - 1.0 - i)[None, :, None] * lg[:, None, None])
    cdec = jnp.exp(chunk_true * lg).reshape(H, 1, 1)
    half = dk // 2
    inv = ROPE_BASE ** (-jnp.arange(half, dtype=F32) / half)
    ang = pos.astype(F32)[:, None] * inv[None, :]
    cos, sin = jnp.cos(ang), jnp.sin(ang)

    qb, vb = (H * dk) // dk, (2 * H * dk) // dv
    gb = vb + H
    rb = row0 // chunk
    row = lambda b, h, c: rb + b * nc + c
    in_specs = [
        pl.BlockSpec((chunk, dk), lambda b, h, c: (row(b, h, c), h)),
        pl.BlockSpec((chunk, dk), lambda b, h, c: (row(b, h, c), qb + h)),
        pl.BlockSpec((chunk, dv), lambda b, h, c: (row(b, h, c), vb + h)),
        pl.BlockSpec((chunk, dv), lambda b, h, c: (row(b, h, c), gb + h)),
        pl.BlockSpec((chunk, half), lambda b, h, c: (c, 0)),
        pl.BlockSpec((chunk, half), lambda b, h, c: (c, 0)),
        pl.BlockSpec((1, chunk, chunk), lambda b, h, c: (h, 0, 0)),
        pl.BlockSpec((1, chunk, 1), lambda b, h, c: (h, 0, 0)),
        pl.BlockSpec((1, chunk, 1), lambda b, h, c: (h, 0, 0)),
        pl.BlockSpec((1, 1, 1), lambda b, h, c: (h, 0, 0)),
        pl.BlockSpec((1, dv), lambda b, h, c: (0, h)),
        pl.BlockSpec((1, 1, dk, dv), lambda b, h, c: (b, h, 0, 0)),
    ]
    args = [proj, proj, proj, proj, cos, sin, dmat, qdec, kdec, cdec, gn_g.reshape(1, H * dv), s0]
    aliases = {}
    if out_init is not None:
        in_specs.append(pl.BlockSpec(memory_space=pl.ANY))
        args.append(out_init)
        aliases = {len(args) - 1: 0}
    out_specs = [
        pl.BlockSpec((chunk, dv), lambda b, h, c: (row(b, h, c), h)),
        pl.BlockSpec((1, 1, dk, dv), lambda b, h, c: (b, h, 0, 0)),
    ]
    out_shape = [jax.ShapeDtypeStruct((proj.shape[0], H * dv), BF16),
                 jax.ShapeDtypeStruct((batch, H, dk, dv), F32)]
    return pl.pallas_call(
        functools.partial(_retention_kernel, dk=dk, aliased=out_init is not None),
        grid=(batch, H, nc),
        in_specs=in_specs,
        out_specs=out_specs,
        out_shape=out_shape,
        input_output_aliases=aliases,
        compiler_params=_cparams("parallel", "parallel", "arbitrary"),
        name=name,
    )(*args)


def _dilated_prompt_kernel(*refs, heads, hd, merge, jobs, ts, steps):
    n_in = 7 if merge else 3
    ins, rest = refs[:n_in], refs[n_in:]
    q_ref, k_ref, v_ref = ins[:3]
    nj = len(jobs)
    caches, news = rest[:nj], rest[nj:2 * nj]
    rest = rest[2 * nj:]
    if merge:
        o0_ref, o1_ref, l0_ref, l1_ref = ins[3:]
        o_ref, rest = rest[0], rest[1:]
        l_ref = None
    else:
        o_ref, l_ref, rest = rest[0], rest[1], rest[2:]
    shifted, rest = rest[:nj], rest[nj:]
    kp_ref, vp_ref, buf, in_sem, out_sem = rest

    n = pl.program_id(2)
    blk = q_ref.shape[1]
    step = (pl.program_id(0) * pl.num_programs(1) + pl.program_id(1)) * pl.num_programs(2) + n
    last = steps - 1

    def plan(h):
        out, off = [], 0
        for j, (halves) in enumerate(jobs):
            r0, nr, tail = halves[h]
            if nr:
                out.append((j, r0, nr, off, tail))
                off += nr + (ts if tail else 0)
        return out

    def in_copies(h, bc):
        return [pltpu.make_async_copy(caches[j].at[bc, pl.ds(ts + r0, nr)],
                                      buf.at[h, pl.ds(off, nr)], in_sem.at[h])
                for (j, r0, nr, off, tail) in plan(h)]

    def out_copies(h, bc):
        return [pltpu.make_async_copy(buf.at[h, pl.ds(off, nr + (ts if tail else 0))],
                                      shifted[j].at[bc, pl.ds(r0, nr + (ts if tail else 0))],
                                      out_sem.at[h])
                for (j, r0, nr, off, tail) in plan(h)]

    @pl.when(step == 0)
    def _():
        for cp in in_copies(0, 0):
            cp.start()

    for h in range(2):
        @pl.when(step % 2 == h)
        def _(h=h):
            bc = step // 2
            for cp in in_copies(h, bc):
                cp.wait()
            for (j, r0, nr, off, tail) in plan(h):
                if tail:
                    buf[h, pl.ds(off + nr, ts)] = news[j][bc]
            for cp in out_copies(h, bc):
                cp.start()

            @pl.when(step > 0)
            def _():
                for cp in out_copies(1 - h, (step - 1) // 2):
                    cp.wait()

            @pl.when(step < last)
            def _():
                for cp in in_copies(1 - h, (step + 1) // 2):
                    cp.start()

    @pl.when(n == 0)
    def _():
        kp_ref[...] = jnp.zeros_like(kp_ref)
        vp_ref[...] = jnp.zeros_like(vp_ref)

    row = lax.broadcasted_iota(jnp.int32, (blk, blk), 0)
    col = lax.broadcasted_iota(jnp.int32, (blk, blk), 1)
    mask_c = col <= row
    mask_p = (col - row) >= jnp.where(n > 0, 0, blk)
    lane = lax.broadcasted_iota(jnp.int32, (blk, LANES), 1)
    scale = hd ** -0.5
    nt = (((1,), (1,)), ((), ()))
    lse_tile = jnp.zeros((blk, LANES), F32)

    for h in range(heads):
        cs = slice(h * hd, (h + 1) * hd)
        qh = q_ref[0, :, cs]
        kc = k_ref[0, :, cs].astype(BF16)
        vc = v_ref[0, :, cs].astype(BF16)
        kp = kp_ref[:, cs]
        vp = vp_ref[:, cs]
        s_c = jnp.where(mask_c, lax.dot_general(qh, kc, nt, preferred_element_type=F32) * scale, NEG)
        s_p = jnp.where(mask_p, lax.dot_general(qh, kp, nt, preferred_element_type=F32) * scale, NEG)
        m = jnp.maximum(jnp.max(s_c, axis=-1, keepdims=True), jnp.max(s_p, axis=-1, keepdims=True))
        p_c = jnp.exp(s_c - m)
        p_p = jnp.exp(s_p - m)
        den = jnp.sum(p_c, axis=-1, keepdims=True) + jnp.sum(p_p, axis=-1, keepdims=True)
        inv = 1.0 / den
        o = (jnp.dot((p_c * inv).astype(BF16), vc, preferred_element_type=F32)
             + jnp.dot((p_p * inv).astype(BF16), vp, preferred_element_type=F32))
        lse = m + jnp.log(den)
        if merge:
            l0 = l0_ref[0, :, h:h + 1]
            l1 = l1_ref[0, :, h:h + 1]
            mx = jnp.maximum(jnp.maximum(l0, l1), lse)
            e0 = jnp.exp(l0 - mx)
            e1 = jnp.exp(l1 - mx)
            e2 = jnp.exp(lse - mx)
            tot = 1.0 / (e0 + e1 + e2)
            o = ((e0 * tot) * o0_ref[0, :, cs].astype(F32) + (e1 * tot) * o1_ref[0, :, cs].astype(F32)
                 + (e2 * tot) * o)
        else:
            lse_tile = jnp.where(lane == h, lse, lse_tile)
        o_ref[0, :, cs] = o.astype(o_ref.dtype)

    if not merge:
        l_ref[0] = lse_tile
    kp_ref[...] = k_ref[0].astype(BF16)
    vp_ref[...] = v_ref[0].astype(BF16)

    @pl.when(step == last)
    def _():
        for cp in out_copies(last % 2, last // 2):
            cp.wait()


def _dilated_prompt(q, kv, gi, *, batch, seq, shift_jobs, prev=None, name):
    win, dil = DIL_PAIRS[gi]
    blk = win // dil
    H = DIL_HEADS
    width = q.shape[1] // N_GROUPS
    hd = width // H
    assert seq % (blk * dil) == 0
    sd = seq // dil
    nb = sd // blk
    qv = q[:batch * seq].reshape(batch, sd, dil * q.shape[1])
    kvv = kv[:batch * seq].reshape(batch, sd, dil * kv.shape[1])
    qpr, kpr = N_GROUPS, 2 * N_GROUPS
    in_specs = [
        pl.BlockSpec((1, blk, width), lambda b, r, n: (b, n, r * qpr + gi)),
        pl.BlockSpec((1, blk, width), lambda b, r, n: (b, n, r * kpr + 2 * gi)),
        pl.BlockSpec((1, blk, width), lambda b, r, n: (b, n, r * kpr + 2 * gi + 1)),
    ]
    args = [qv, kvv, kvv]
    o_spec = pl.BlockSpec((1, blk, width), lambda b, r, n: (b, n, r))
    l_spec = pl.BlockSpec((1, blk, LANES), lambda b, r, n: (b, n, r))
    o_shape = jax.ShapeDtypeStruct((batch, sd, dil * width), BF16)
    l_shape = jax.ShapeDtypeStruct((batch, sd, dil * LANES), F32)
    merge = prev is not None
    if merge:
        (o0, l0), (o1, l1) = prev
        in_specs += [o_spec, o_spec, l_spec, l_spec]
        args += [o0.reshape(o_shape.shape), o1.reshape(o_shape.shape),
                 l0.reshape(l_shape.shape), l1.reshape(l_shape.shape)]
        out_specs, out_shape = [o_spec], [o_shape]
    else:
        out_specs, out_shape = [o_spec, l_spec], [o_shape, l_shape]

    steps = batch * dil * nb
    caches = [c for c, _ in shift_jobs]
    news = [a for _, a in shift_jobs]
    ts = news[0].shape[1]
    Bc = caches[0].shape[0]
    assert steps == 2 * Bc
    jobs, slot_rows = [], [0, 0]
    for c in caches:
        keep = c.shape[1] - ts
        first = keep // 2 if keep * c.shape[2] * c.shape[3] * 4 > (2 << 20) else 0
        halves = ((0, first, False), (first, keep - first, True))
        jobs.append(halves)
        for h in range(2):
            slot_rows[h] += halves[h][1] + (ts if halves[h][2] else 0)
    any_spec = pl.BlockSpec(memory_space=pl.ANY)
    in_specs += [any_spec] * len(caches) + [pl.BlockSpec(memory_space=pltpu.VMEM)] * len(news)
    args += caches + news
    out_specs = out_specs + [any_spec] * len(caches)
    out_shape = out_shape + [jax.ShapeDtypeStruct(c.shape, c.dtype) for c in caches]

    res = pl.pallas_call(
        functools.partial(_dilated_prompt_kernel, heads=H, hd=hd, merge=merge, jobs=tuple(jobs), ts=ts,
                          steps=steps),
        grid=(batch, dil, nb),
        in_specs=in_specs,
        out_specs=out_specs,
        out_shape=out_shape,
        scratch_shapes=[pltpu.VMEM((blk, width), BF16), pltpu.VMEM((blk, width), BF16),
                        pltpu.VMEM((2, max(slot_rows)) + caches[0].shape[2:], caches[0].dtype),
                        pltpu.SemaphoreType.DMA((2,)), pltpu.SemaphoreType.DMA((2,))],
        compiler_params=_cparams("arbitrary", "arbitrary", "arbitrary"),
        name=name,
    )(*args)
    n_main = 1 if merge else 2
    main, shifted = res[:n_main], list(res[n_main:])
    if merge:
        return main[0].reshape(batch * seq, width), shifted
    return (main[0].reshape(batch * seq, width), main[1].reshape(batch * seq, LANES)), shifted


def _dilated_sample_kernel(q_ref, kn_ref, vn_ref, k0_ref, v0_ref, k1_ref, v1_ref, k2_ref, v2_ref,
                           o_ref, *, ts, hd):
    caches = ((k0_ref, v0_ref), (k1_ref, v1_ref), (k2_ref, v2_ref))
    scale = hd ** -0.5
    for t in range(ts):
        outs, lses = [], []
        for gi, (win, dil) in enumerate(DIL_PAIRS):
            kc_ref, vc_ref = caches[gi]
            q = q_ref[0, t, gi]
            r = t % dil if dil > 1 else 0
            kc = kc_ref[0, :, r]
            vc = vc_ref[0, :, r]
            s_c = jnp.sum(kc * q[None], axis=-1, keepdims=True) * scale
            if dil == 1:
                cidx = lax.broadcasted_iota(jnp.int32, s_c.shape, 0)
                s_c = jnp.where(cidx >= t, s_c, NEG)
                new_rows = list(range(t + 1))
            else:
                new_rows = [t]
            s_n = [jnp.sum(kn_ref[0, u, gi] * q, axis=-1, keepdims=True) * scale for u in new_rows]
            m = jnp.max(s_c, axis=0)
            for sn in s_n:
                m = jnp.maximum(m, sn)
            p_c = jnp.exp(s_c - m[None])
            p_n = [jnp.exp(sn - m) for sn in s_n]
            den = jnp.sum(p_c, axis=0)
            for pn in p_n:
                den = den + pn
            inv = 1.0 / den
            o = jnp.sum((p_c * inv[None]) * vc, axis=0)
            for u, pn in zip(new_rows, p_n):
                o = o + (pn * inv) * vn_ref[0, u, gi]
            outs.append(o)
            lses.append(m + jnp.log(den))
        mx = jnp.maximum(jnp.maximum(lses[0], lses[1]), lses[2])
        es = [jnp.exp(l - mx) for l in lses]
        tot = 1.0 / (es[0] + es[1] + es[2])
        o_ref[0, t] = (es[0] * tot) * outs[0] + (es[1] * tot) * outs[1] + (es[2] * tot) * outs[2]


def _dilated_sample(q, knew, vnew, caches_k, caches_v, *, name):
    B, ts, G, H, hd = q.shape
    small = pl.BlockSpec((1, ts, G, H, hd), lambda b: (b, 0, 0, 0, 0))
    in_specs = [small, small, small]
    args = [q, knew, vnew]
    for gi, (win, dil) in enumerate(DIL_PAIRS):
        L = caches_k[gi].shape[1]
        assert L == win and L % dil == 0 and (dil == 1 or ts <= dil)
        nres = min(dil, ts)
        spec = pl.BlockSpec((1, L // dil, nres, H, hd), lambda b: (b, 0, 0, 0, 0))
        for c in (caches_k[gi], caches_v[gi]):
            in_specs.append(spec)
            args.append(c.reshape(B, L // dil, dil, H, hd))
    return pl.pallas_call(
        functools.partial(_dilated_sample_kernel, ts=ts, hd=hd),
        grid=(B,),
        in_specs=in_specs,
        out_specs=pl.BlockSpec((1, ts, H, hd), lambda b: (b, 0, 0, 0)),
        out_shape=jax.ShapeDtypeStruct((B, ts, H, hd), F32),
        compiler_params=_cparams("parallel"),
        name=name,
    )(*args)


def kernel(x_prompt, x_sample, state_ret, cache_k_w128, cache_v_w128, cache_k_w512, cache_v_w512,
           cache_k_w2048, cache_v_w2048, norm_mix, norm_ffn, ret_w_in, ret_gn, ret_w_out,
           kv_norm, w_kv, dil_w_q, dil_w_o, ffn_w1, ffn_w3, ffn_w2, norm_final):
    Bp, Tp, D = x_prompt.shape
    Bs, Ts, _ = x_sample.shape
    H, hd = DIL_HEADS, D // DIL_HEADS
    caches_k = [cache_k_w128, cache_k_w512, cache_k_w2048]
    caches_v = [cache_v_w128, cache_v_w512, cache_v_w2048]
    MP, MS = Bp * Tp, Bs * SAMPLE_PAD
    MT = MP + MS
    NT = MT // 16

    h = jnp.concatenate([x_prompt.reshape(MP, D),
                         jnp.pad(x_sample, ((0, 0), (0, SAMPLE_PAD - Ts), (0, 0))).reshape(MS, D)])

    def ffn(h, layer):
        xn, = _norm(h, [norm_ffn[layer]], out_dtype=BF16, tm=NT, name=f"ffn_norm_{layer}")
        act = _linear(xn, [ffn_w1[layer], ffn_w3[layer]], swiglu=True, out_dtype=BF16, tn=512,
                      name=f"ffn_up_{layer}")
        return _linear(act, [ffn_w2[layer]], residual=h, out_dtype=F32, tn=256, name=f"ffn_down_{layer}")

    chunk_p = RET_CHUNK if Tp % RET_CHUNK == 0 else Tp
    xn, = _norm(h, [norm_mix[0]], out_dtype=BF16, tm=NT, name="mix_norm_0")
    proj = _linear(xn, [ret_w_in[0]], out_dtype=BF16, tn=512, name="ret_in")
    s0_p = jnp.zeros((Bp,) + state_ret.shape[2:], F32)
    gated = jnp.zeros((MT, RET_HEADS * state_ret.shape[-1]), BF16)
    gated, sp = _retention(proj, s0_p, ret_gn[0], row0=0, batch=Bp, seq=Tp, chunk=chunk_p,
                           chunk_true=chunk_p, pos=jnp.arange(Tp), out_init=gated, name="retention_p")
    gated, ss = _retention(proj, state_ret[0], ret_gn[0], row0=MP, batch=Bs, seq=SAMPLE_PAD,
                           chunk=SAMPLE_PAD, chunk_true=Ts, pos=PAST_LEN + jnp.arange(SAMPLE_PAD),
                           out_init=gated, name="retention_s")
    h = _linear(gated, [ret_w_out[0]], residual=h, out_dtype=F32, tn=256, name="ret_out")
    h = ffn(h, 0)

    xkv, xq = _norm(h, [kv_norm, norm_mix[1]], out_dtype=BF16, tm=NT, name="kv_q_norm")
    kv = _linear(xkv, [w_kv], out_dtype=F32, tn=512, name="kv_proj")
    q = _linear(xq, [dil_w_q[0]], out_dtype=BF16, tn=512, name="dil_q")
    kv_p6 = kv[:MP].reshape(Bp, Tp, N_GROUPS, 2, H, hd)
    kv_s6 = kv[MP:].reshape(Bs, SAMPLE_PAD, N_GROUPS, 2, H, hd)[:, :Ts]
    new_kp = [kv_p6[:, -min(DIL_PAIRS[i][0], Tp):, i, 0] for i in range(N_GROUPS)]
    new_vp = [kv_p6[:, -min(DIL_PAIRS[i][0], Tp):, i, 1] for i in range(N_GROUPS)]
    knew, vnew = kv_s6[:, :, :, 0], kv_s6[:, :, :, 1]
    q_s5 = q[MP:].astype(F32).reshape(Bs, SAMPLE_PAD, N_GROUPS, H, hd)[:, :Ts]

    job = lambda c, a, i: (c[i], a[:, :, i])
    br0, (nk2,) = _dilated_prompt(q, kv, 0, batch=Bp, seq=Tp, shift_jobs=[job(caches_k, knew, 2)],
                                  name="dilated_p0")
    br1, (nk0, nv0, nk1, nv1) = _dilated_prompt(
        q, kv, 1, batch=Bp, seq=Tp, name="dilated_p1",
        shift_jobs=[job(caches_k, knew, 0), job(caches_v, vnew, 0),
                    job(caches_k, knew, 1), job(caches_v, vnew, 1)])
    a_p, (nv2,) = _dilated_prompt(q, kv, 2, batch=Bp, seq=Tp, prev=(br0, br1),
                                  shift_jobs=[job(caches_v, vnew, 2)], name="dilated_p2")
    a_s = _dilated_sample(q_s5, knew, vnew, caches_k, caches_v, name="dilated_s")
    a_s = jnp.pad(a_s.reshape(Bs, Ts, D), ((0, 0), (0, SAMPLE_PAD - Ts), (0, 0))).reshape(MS, D)
    a = jnp.concatenate([a_p, a_s.astype(BF16)])
    h = _linear(a, [dil_w_o[0]], residual=h, out_dtype=F32, tn=512, name="dil_o")
    h = ffn(h, 1)

    y_p, = _norm(h, [norm_final], out_dtype=F32, tm=MS, rows=MP, name="final_norm_p")
    y_s, = _norm(h, [norm_final], out_dtype=F32, tm=MS, row0=MP, rows=MS, name="final_norm_s")
    return (y_p.reshape(Bp, Tp, D), y_s.reshape(Bs, SAMPLE_PAD, D)[:, :Ts], sp[None], ss[None],
            new_kp[0], new_vp[0], new_kp[1], new_vp[1], new_kp[2], new_vp[2],
            nk0, nv0, nk1, nv1, nk2, nv2)
```

```python
import functools

import jax
import jax.numpy as jnp
from jax import lax
from jax.experimental import pallas as pl
from jax.experimental.pallas import tpu as pltpu

F32 = jnp.float32
BF16 = jnp.bfloat16

RET_HEADS = 8
RET_CHUNK = 128
ROPE_BASE = 10000.0
DIL_PAIRS = ((128, 1), (512, 4), (2048, 16))
N_GROUPS = len(DIL_PAIRS)
DIL_HEADS = 16
PAST_LEN = 8192
EPS = 1e-6
NEG = -1e30

LANES = 128
SAMPLE_PAD = 16
VMEM_LIMIT = 61 * 1024 * 1024
ROW_TILES = 4


def _cparams(*sem):
    return pltpu.CompilerParams(dimension_semantics=sem, vmem_limit_bytes=VMEM_LIMIT)


def _norm_kernel(x_ref, g_ref, *o_refs):
    xf = x_ref[...]
    ms = jnp.mean(xf * xf, axis=-1, keepdims=True)
    xs = xf * lax.rsqrt(ms + EPS)
    for k, o_ref in enumerate(o_refs):
        o_ref[...] = (xs * g_ref[k:k + 1, :]).astype(o_ref.dtype)


def _norm(x, gains, *, out_dtype, tm, row0=0, rows=None, name):
    M, K = x.shape
    rows = M if rows is None else rows
    assert rows % tm == 0 and row0 % tm == 0
    g = jnp.stack(gains).astype(F32)
    off = row0 // tm
    outs = pl.pallas_call(
        _norm_kernel,
        grid=(rows // tm,),
        in_specs=[pl.BlockSpec((tm, K), lambda i: (i + off, 0)),
                  pl.BlockSpec((len(gains), K), lambda i: (0, 0))],
        out_specs=[pl.BlockSpec((tm, K), lambda i: (i, 0))] * len(gains),
        out_shape=[jax.ShapeDtypeStruct((rows, K), out_dtype)] * len(gains),
        compiler_params=_cparams("parallel"),
        name=name,
    )(x, g)
    return outs


def _linear_kernel(*refs, n_w, has_res, swiglu):
    it = iter(refs)
    x_ref = next(it)
    w_refs = [next(it) for _ in range(n_w)]
    r_ref = next(it) if has_res else None
    o_ref = next(it)
    xn = x_ref[...]
    a = jnp.dot(xn, w_refs[0][...].astype(BF16), preferred_element_type=F32)
    if swiglu:
        b = jnp.dot(xn, w_refs[1][...].astype(BF16), preferred_element_type=F32)
        a = (a * jax.nn.sigmoid(a)) * b
    if has_res:
        a = a + r_ref[...]
    o_ref[...] = a.astype(o_ref.dtype)


def _linear(x, ws, *, residual=None, swiglu=False, out_dtype, tn, row0=0, rows=None,
            row_tiles=ROW_TILES, name):
    K = x.shape[1]
    M = x.shape[0] if rows is None else rows
    N = ws[0].shape[1]
    assert M % (row_tiles * 16) == 0 and N % tn == 0 and x.dtype == BF16
    tm = M // row_tiles
    assert row0 % tm == 0 and (residual is None or row0 == 0)
    off = row0 // tm
    has_res = residual is not None
    in_specs = [pl.BlockSpec((tm, K), lambda i, j: (i + off, 0), pipeline_mode=pl.Buffered(1))]
    args = [x]
    for w in ws:
        in_specs.append(pl.BlockSpec((K, tn), lambda i, j: (0, j)))
        args.append(w)
    if has_res:
        in_specs.append(pl.BlockSpec((tm, tn), lambda i, j: (i, j)))
        args.append(residual)
    kern = functools.partial(_linear_kernel, n_w=len(ws), has_res=has_res, swiglu=swiglu)
    return pl.pallas_call(
        kern,
        grid=(row_tiles, N // tn),
        in_specs=in_specs,
        out_specs=pl.BlockSpec((tm, tn), lambda i, j: (i, j)),
        out_shape=jax.ShapeDtypeStruct((M, N), out_dtype),
        compiler_params=_cparams("parallel", "arbitrary"),
        name=name,
    )(*args)


MXU_COLS = 256


def _dilated_linear_kernel(x_ref, w_ref, *refs, dil, win_rows):
    it = iter(refs)
    o_ref = next(it)
    win_ref = next(it) if win_rows else None
    acc_ref = next(it) if dil > 1 else None
    tm, tn = x_ref.shape[0], w_ref.shape[1]
    x = x_ref[...]
    for c in range(tn // MXU_COLS):
        cs = slice(c * MXU_COLS, (c + 1) * MXU_COLS)
        acc = jnp.dot(x, w_ref[:, cs].astype(BF16), preferred_element_type=F32)
        if win_rows:
            win_ref[0, :, cs] = acc[tm - win_rows:, :]
        if dil == 1:
            o_ref[0, 0, :, cs] = acc.astype(o_ref.dtype)
        else:
            for s in range(c * MXU_COLS // LANES, (c + 1) * MXU_COLS // LANES):
                ls = slice(s * LANES, (s + 1) * LANES)
                acc_ref[s] = acc[:, ls.start - c * MXU_COLS:ls.stop - c * MXU_COLS]
                for r in range(dil):
                    o_ref[0, r, :, ls] = acc_ref[s, pl.ds(r, tm // dil, stride=dil), :].astype(o_ref.dtype)


def _dilated_linear(x, w, col0, *, batch, seq, dil, window=None, tm=2048, tn=512, name):
    K = x.shape[1]
    width = DIL_HEADS * (K // DIL_HEADS)
    assert seq % tm == 0 and tm % (dil * 16) == 0 and col0 % tn == 0 and width % tn == 0
    tpb = seq // tm
    cb0 = col0 // tn
    in_specs = [pl.BlockSpec((tm, K), lambda i, j: (i, 0), pipeline_mode=pl.Buffered(1)),
                pl.BlockSpec((K, tn), lambda i, j: (0, cb0 + j))]
    out_specs = [pl.BlockSpec((1, dil, tm // dil, tn), lambda i, j: (i // tpb, 0, i % tpb, j))]
    out_shape = [jax.ShapeDtypeStruct((batch, dil, seq // dil, width), BF16)]
    win_rows = 0
    if window is not None:
        win_rows = min(window, tm)
        assert window % win_rows == 0
        t0 = tpb - window // win_rows

        def win_map(i, j):
            t = i % tpb
            inside = t >= t0
            return (i // tpb, jnp.where(inside, t - t0, 0), jnp.where(inside, j, 0))

        out_specs.append(pl.BlockSpec((1, win_rows, tn), win_map))
        out_shape.append(jax.ShapeDtypeStruct((batch, window, width), F32))
    scratch = [pltpu.VMEM((tn // LANES, tm, LANES), F32)] if dil > 1 else []
    res = pl.pallas_call(
        functools.partial(_dilated_linear_kernel, dil=dil, win_rows=win_rows),
        grid=(batch * tpb, width // tn),
        in_specs=in_specs,
        out_specs=out_specs,
        out_shape=out_shape,
        scratch_shapes=scratch,
        compiler_params=_cparams("arbitrary", "arbitrary"),
        name=name,
    )(x, w)
    return res if window is not None else res[0]


def _retention_kernel(*refs, dk, aliased):
    (q_ref, k_ref, v_ref, g_ref, cos_ref, sin_ref, dmat_ref, qdec_ref, kdec_ref,
     cdec_ref, gn_ref, s0_ref) = refs[:12]
    o_ref, s_ref = refs[-2:]
    del aliased

    @pl.when(pl.program_id(2) == 0)
    def _():
        s_ref[...] = s0_ref[...]

    cos = cos_ref[...]
    sin = sin_ref[...]
    half = dk // 2

    def rope(x):
        x1 = x[:, :half]
        x2 = x[:, half:]
        return jnp.concatenate([x1 * cos - x2 * sin, x1 * sin + x2 * cos], axis=-1)

    q = rope(q_ref[...].astype(F32))
    k = rope(k_ref[...].astype(F32)) * (dk ** -0.5)
    v = v_ref[...]
    s = s_ref[0, 0]

    sc = lax.dot_general(q.astype(BF16), k.astype(BF16), (((1,), (1,)), ((), ())),
                         preferred_element_type=F32) * dmat_ref[0]
    o = (jnp.dot(sc.astype(BF16), v, preferred_element_type=F32)
         + jnp.dot((q * qdec_ref[0]).astype(BF16), s.astype(BF16), preferred_element_type=F32))
    kd = (k * kdec_ref[0]).astype(BF16)
    s_ref[0, 0] = s * cdec_ref[0] + lax.dot_general(kd, v, (((0,), (0,)), ((), ())),
                                                   preferred_element_type=F32)

    mu = jnp.mean(o, axis=-1, keepdims=True)
    var = jnp.mean(jnp.square(o - mu), axis=-1, keepdims=True)
    on = (o - mu) * lax.rsqrt(var + EPS) * gn_ref[...]
    gate = g_ref[...].astype(F32)
    o_ref[...] = ((gate * jax.nn.sigmoid(gate)) * on).astype(o_ref.dtype)


def _retention(proj, s0, gn_g, *, row0, batch, seq, chunk, chunk_true, pos, out_init=None, name):
    H = RET_HEADS
    dk, dv = s0.shape[-2], s0.shape[-1]
    nc = seq // chunk
    assert row0 % chunk == 0
    lg = jnp.log1p(-jnp.exp2(-5.0 - jnp.arange(H, dtype=F32)))
    i = jnp.arange(chunk, dtype=F32)
    dist = i[:, None] - i[None, :]
    dmat = jnp.where(dist >= 0, jnp.exp(jnp.maximum(dist, 0.0)[None] * lg[:, None, None]), 0.0)
    qdec = jnp.exp((i + 1.0)[None, :, None] * lg[:, None, None])
    kdec = jnp.exp((chunk_true - 1.0 - i)[None, :, None] * lg[:, None, None])
    cdec = jnp.exp(chunk_true * lg).reshape(H, 1, 1)
    half = dk // 2
    inv = ROPE_BASE ** (-jnp.arange(half, dtype=F32) / half)
    ang = pos.astype(F32)[:, None] * inv[None, :]
    cos, sin = jnp.cos(ang), jnp.sin(ang)

    qb, vb = (H * dk) // dk, (2 * H * dk) // dv
    gb = vb + H
    rb = row0 // chunk
    row = lambda b, h, c: rb + b * nc + c
    in_specs = [
        pl.BlockSpec((chunk, dk), lambda b, h, c: (row(b, h, c), h)),
        pl.BlockSpec((chunk, dk), lambda b, h, c: (row(b, h, c), qb + h)),
        pl.BlockSpec((chunk, dv), lambda b, h, c: (row(b, h, c), vb + h)),
        pl.BlockSpec((chunk, dv), lambda b, h, c: (row(b, h, c), gb + h)),
        pl.BlockSpec((chunk, half), lambda b, h, c: (c, 0)),
        pl.BlockSpec((chunk, half), lambda b, h, c: (c, 0)),
        pl.BlockSpec((1, chunk, chunk), lambda b, h, c: (h, 0, 0)),
        pl.BlockSpec((1, chunk, 1), lambda b, h, c: (h, 0, 0)),
        pl.BlockSpec((1, chunk, 1), lambda b, h, c: (h, 0, 0)),
        pl.BlockSpec((1, 1, 1), lambda b, h, c: (h, 0, 0)),
        pl.BlockSpec((1, dv), lambda b, h, c: (0, h)),
        pl.BlockSpec((1, 1, dk, dv), lambda b, h, c: (b, h, 0, 0)),
    ]
    args = [proj, proj, proj, proj, cos, sin, dmat, qdec, kdec, cdec, gn_g.reshape(1, H * dv), s0]
    aliases = {}
    if out_init is not None:
        in_specs.append(pl.BlockSpec(memory_space=pl.ANY))
        args.append(out_init)
        aliases = {len(args) - 1: 0}
    out_specs = [
        pl.BlockSpec((chunk, dv), lambda b, h, c: (row(b, h, c), h)),
        pl.BlockSpec((1, 1, dk, dv), lambda b, h, c: (b, h, 0, 0)),
    ]
    out_shape = [jax.ShapeDtypeStruct((proj.shape[0], H * dv), BF16),
                 jax.ShapeDtypeStruct((batch, H, dk, dv), F32)]
    return pl.pallas_call(
        functools.partial(_retention_kernel, dk=dk, aliased=out_init is not None),
        grid=(batch, H, nc),
        in_specs=in_specs,
        out_specs=out_specs,
        out_shape=out_shape,
        input_output_aliases=aliases,
        compiler_params=_cparams("parallel", "parallel", "arbitrary"),
        name=name,
    )(*args)


def _dilated_prompt_kernel(*refs, heads, hd, jobs, ts, steps):
    q_ref, k_ref, v_ref = refs[:3]
    nj = len(jobs)
    caches, news = refs[3:3 + nj], refs[3 + nj:3 + 2 * nj]
    o_ref, l_ref = refs[3 + 2 * nj:5 + 2 * nj]
    shifted = refs[5 + 2 * nj:5 + 3 * nj]
    kp_ref, vp_ref, buf, in_sem, out_sem = refs[5 + 3 * nj:]

    n = pl.program_id(2)
    blk = q_ref.shape[2]
    step = (pl.program_id(0) * pl.num_programs(1) + pl.program_id(1)) * pl.num_programs(2) + n
    last = steps - 1

    def plan(h):
        out, off = [], 0
        for j, (halves) in enumerate(jobs):
            r0, nr, tail = halves[h]
            if nr:
                out.append((j, r0, nr, off, tail))
                off += nr + (ts if tail else 0)
        return out

    def in_copies(h, bc):
        return [pltpu.make_async_copy(caches[j].at[bc, pl.ds(ts + r0, nr)],
                                      buf.at[h, pl.ds(off, nr)], in_sem.at[h])
                for (j, r0, nr, off, tail) in plan(h)]

    def out_copies(h, bc):
        return [pltpu.make_async_copy(buf.at[h, pl.ds(off, nr + (ts if tail else 0))],
                                      shifted[j].at[bc, pl.ds(r0, nr + (ts if tail else 0))],
                                      out_sem.at[h])
                for (j, r0, nr, off, tail) in plan(h)]

    @pl.when(step == 0)
    def _():
        for cp in in_copies(0, 0):
            cp.start()

    for h in range(2):
        @pl.when(step % 2 == h)
        def _(h=h):
            bc = step // 2
            for cp in in_copies(h, bc):
                cp.wait()
            for (j, r0, nr, off, tail) in plan(h):
                if tail:
                    buf[h, pl.ds(off + nr, ts)] = news[j][bc]
            for cp in out_copies(h, bc):
                cp.start()

            @pl.when(step > 0)
            def _():
                for cp in out_copies(1 - h, (step - 1) // 2):
                    cp.wait()

            @pl.when(step < last)
            def _():
                for cp in in_copies(1 - h, (step + 1) // 2):
                    cp.start()

    @pl.when(n == 0)
    def _():
        kp_ref[...] = jnp.zeros_like(kp_ref)
        vp_ref[...] = jnp.zeros_like(vp_ref)

    row = lax.broadcasted_iota(jnp.int32, (blk, blk), 0)
    col = lax.broadcasted_iota(jnp.int32, (blk, blk), 1)
    mask_c = col <= row
    mask_p = (col - row) >= jnp.where(n > 0, 0, blk)
    lane = lax.broadcasted_iota(jnp.int32, (blk, LANES), 1)
    scale = hd ** -0.5
    nt = (((1,), (1,)), ((), ()))
    hpt = heads // l_ref.shape[0]
    lse_tile = None

    for h in range(heads):
        cs = slice(h * hd, (h + 1) * hd)
        qh = q_ref[0, 0, :, cs]
        kc = k_ref[0, 0, :, cs]
        vc = v_ref[0, 0, :, cs]
        kp = kp_ref[:, cs]
        vp = vp_ref[:, cs]
        s_c = jnp.where(mask_c, lax.dot_general(qh, kc, nt, preferred_element_type=F32) * scale, NEG)
        s_p = jnp.where(mask_p, lax.dot_general(qh, kp, nt, preferred_element_type=F32) * scale, NEG)
        m = jnp.maximum(jnp.max(s_c, axis=-1, keepdims=True), jnp.max(s_p, axis=-1, keepdims=True))
        p_c = jnp.exp(s_c - m)
        p_p = jnp.exp(s_p - m)
        den = jnp.sum(p_c, axis=-1, keepdims=True) + jnp.sum(p_p, axis=-1, keepdims=True)
        inv = 1.0 / den
        o = (jnp.dot((p_c * inv).astype(BF16), vc, preferred_element_type=F32)
             + jnp.dot((p_p * inv).astype(BF16), vp, preferred_element_type=F32))
        lse = m + jnp.log(den)
        o_ref[0, 0, :, cs] = o
        if h % hpt == 0:
            lse_tile = jnp.zeros((blk, LANES), F32)
        lse_tile = jnp.where(lane == h % hpt, lse, lse_tile)
        if h % hpt == hpt - 1:
            l_ref[h // hpt, 0, 0] = lse_tile

    kp_ref[...] = k_ref[0, 0]
    vp_ref[...] = v_ref[0, 0]

    @pl.when(step == last)
    def _():
        for cp in out_copies(last % 2, last // 2):
            cp.wait()


LSE_TILES = 4


def _dilated_prompt(q, k, v, gi, *, shift_jobs, name):
    win, dil = DIL_PAIRS[gi]
    blk = win // dil
    H = DIL_HEADS
    batch, _, sd, width = q.shape
    hd = width // H
    assert q.shape[1] == dil and sd % blk == 0
    nb = sd // blk
    slab = pl.BlockSpec((1, 1, blk, width), lambda b, r, n: (b, r, n, 0))
    in_specs = [slab, slab, slab]
    args = [q, k, v]
    out_specs = [slab, pl.BlockSpec((LSE_TILES, 1, 1, blk, LANES), lambda b, r, n: (0, b, r, n, 0))]
    out_shape = [jax.ShapeDtypeStruct(q.shape, F32),
                 jax.ShapeDtypeStruct((LSE_TILES, batch, dil, sd, LANES), F32)]

    steps = batch * dil * nb
    caches = [c for c, _ in shift_jobs]
    news = [a for _, a in shift_jobs]
    ts = news[0].shape[1]
    Bc = caches[0].shape[0]
    assert steps == 2 * Bc
    jobs, slot_rows = [], [0, 0]
    for c in caches:
        keep = c.shape[1] - ts
        first = keep // 2 if keep * c.shape[2] * c.shape[3] * 4 > (2 << 20) else 0
        halves = ((0, first, False), (first, keep - first, True))
        jobs.append(halves)
        for h in range(2):
            slot_rows[h] += halves[h][1] + (ts if halves[h][2] else 0)
    any_spec = pl.BlockSpec(memory_space=pl.ANY)
    in_specs += [any_spec] * len(caches) + [pl.BlockSpec(memory_space=pltpu.VMEM)] * len(news)
    args += caches + news
    out_specs = out_specs + [any_spec] * len(caches)
    out_shape = out_shape + [jax.ShapeDtypeStruct(c.shape, c.dtype) for c in caches]

    res = pl.pallas_call(
        functools.partial(_dilated_prompt_kernel, heads=H, hd=hd, jobs=tuple(jobs), ts=ts, steps=steps),
        grid=(batch, dil, nb),
        in_specs=in_specs,
        out_specs=out_specs,
        out_shape=out_shape,
        scratch_shapes=[pltpu.VMEM((blk, width), BF16), pltpu.VMEM((blk, width), BF16),
                        pltpu.VMEM((2, max(slot_rows)) + caches[0].shape[2:], caches[0].dtype),
                        pltpu.SemaphoreType.DMA((2,)), pltpu.SemaphoreType.DMA((2,))],
        compiler_params=_cparams("arbitrary", "arbitrary", "arbitrary"),
        name=name,
    )(*args)
    return res[0], res[1], list(res[2:])


def _merge_kernel(o0_ref, o1_ref, o2_ref, l0_ref, l1_ref, l2_ref, init_ref, out_ref,
                  s0_ref, s1_ref, nat_ref, *, hd):
    del init_ref
    d1, d2 = o1_ref.shape[1], o2_ref.shape[1]
    rows = o2_ref.shape[2]
    rep = d2 // d1
    nh = o2_ref.shape[3] // hd
    for hh in range(nh):
        cs = slice(hh * hd, (hh + 1) * hd)
        s0_ref[hh] = o0_ref[:, cs]
        s1_ref[hh] = o1_ref[0, :, :, cs]
    for r in range(d2):
        l2 = l2_ref[0, 0, r]
        l1 = l1_ref[0, 0, r % d1, pl.ds(r // d1, rows, stride=rep), :]
        l0 = l0_ref[0, pl.ds(r, rows, stride=d2), :]
        mx = jnp.maximum(jnp.maximum(l0, l1), l2)
        e0 = jnp.exp(l0 - mx)
        e1 = jnp.exp(l1 - mx)
        e2 = jnp.exp(l2 - mx)
        tot = 1.0 / (e0 + e1 + e2)
        w0, w1, w2 = e0 * tot, e1 * tot, e2 * tot
        for hh in range(nh):
            ls = slice(hh, hh + 1)
            o2 = o2_ref[0, r, :, hh * hd:(hh + 1) * hd]
            o1 = s1_ref[hh, r % d1, pl.ds(r // d1, rows, stride=rep), :]
            o0 = s0_ref[hh, pl.ds(r, rows, stride=d2), :]
            nat_ref[hh, pl.ds(r, rows, stride=d2), :] = w0[:, ls] * o0 + w1[:, ls] * o1 + w2[:, ls] * o2
    for hh in range(nh):
        out_ref[:, hh * hd:(hh + 1) * hd] = nat_ref[hh].astype(out_ref.dtype)


def _merge(outs, lses, init, *, name):
    (_, d0), (_, d1), (_, d2) = DIL_PAIRS
    batch, _, sd2, width = outs[2].shape
    seq = sd2 * d2
    rows = DIL_PAIRS[2][0] // d2
    span = rows * d2
    assert d0 == 1 and d2 % d1 == 0 and seq % span == 0
    nsp = seq // span
    cw = width // LSE_TILES
    hd = width // DIL_HEADS
    o0 = outs[0].reshape(batch * seq, width)
    l0 = lses[0].reshape(LSE_TILES, batch * seq, LANES)
    in_specs = [
        pl.BlockSpec((span, cw), lambda t, c: (t, c)),
        pl.BlockSpec((1, d1, span // d1, cw), lambda t, c: (t // nsp, 0, t % nsp, c)),
        pl.BlockSpec((1, d2, rows, cw), lambda t, c: (t // nsp, 0, t % nsp, c)),
        pl.BlockSpec((1, span, LANES), lambda t, c: (c, t, 0)),
        pl.BlockSpec((1, 1, d1, span // d1, LANES), lambda t, c: (c, t // nsp, 0, t % nsp, 0)),
        pl.BlockSpec((1, 1, d2, rows, LANES), lambda t, c: (c, t // nsp, 0, t % nsp, 0)),
        pl.BlockSpec(memory_space=pl.ANY),
    ]
    return pl.pallas_call(
        functools.partial(_merge_kernel, hd=hd),
        grid=(batch * nsp, LSE_TILES),
        in_specs=in_specs,
        out_specs=pl.BlockSpec((span, cw), lambda t, c: (t, c)),
        out_shape=jax.ShapeDtypeStruct(init.shape, init.dtype),
        scratch_shapes=[pltpu.VMEM((cw // hd, span, hd), F32),
                        pltpu.VMEM((cw // hd, d1, span // d1, hd), F32),
                        pltpu.VMEM((cw // hd, span, hd), F32)],
        input_output_aliases={6: 0},
        compiler_params=_cparams("parallel", "parallel"),
        name=name,
    )(o0, outs[1], outs[2], l0, lses[1], lses[2], init)


def _dilated_sample_kernel(q_ref, kn_ref, vn_ref, k0_ref, v0_ref, k1_ref, v1_ref, k2_ref, v2_ref,
                           o_ref, *, ts, hd):
    caches = ((k0_ref, v0_ref), (k1_ref, v1_ref), (k2_ref, v2_ref))
    scale = hd ** -0.5
    for t in range(ts):
        outs, lses = [], []
        for gi, (win, dil) in enumerate(DIL_PAIRS):
            kc_ref, vc_ref = caches[gi]
            q = q_ref[0, t, gi]
            r = t % dil if dil > 1 else 0
            kc = kc_ref[0, :, r]
            vc = vc_ref[0, :, r]
            s_c = jnp.sum(kc * q[None], axis=-1, keepdims=True) * scale
            if dil == 1:
                cidx = lax.broadcasted_iota(jnp.int32, s_c.shape, 0)
                s_c = jnp.where(cidx >= t, s_c, NEG)
                new_rows = list(range(t + 1))
            else:
                new_rows = [t]
            s_n = [jnp.sum(kn_ref[0, u, gi] * q, axis=-1, keepdims=True) * scale for u in new_rows]
            m = jnp.max(s_c, axis=0)
            for sn in s_n:
                m = jnp.maximum(m, sn)
            p_c = jnp.exp(s_c - m[None])
            p_n = [jnp.exp(sn - m) for sn in s_n]
            den = jnp.sum(p_c, axis=0)
            for pn in p_n:
                den = den + pn
            inv = 1.0 / den
            o = jnp.sum((p_c * inv[None]) * vc, axis=0)
            for u, pn in zip(new_rows, p_n):
                o = o + (pn * inv) * vn_ref[0, u, gi]
            outs.append(o)
            lses.append(m + jnp.log(den))
        mx = jnp.maximum(jnp.maximum(lses[0], lses[1]), lses[2])
        es = [jnp.exp(l - mx) for l in lses]
        tot = 1.0 / (es[0] + es[1] + es[2])
        o_ref[0, t] = (es[0] * tot) * outs[0] + (es[1] * tot) * outs[1] + (es[2] * tot) * outs[2]


def _dilated_sample(q, knew, vnew, caches_k, caches_v, *, name):
    B, ts, G, H, hd = q.shape
    small = pl.BlockSpec((1, ts, G, H, hd), lambda b: (b, 0, 0, 0, 0))
    in_specs = [small, small, small]
    args = [q, knew, vnew]
    for gi, (win, dil) in enumerate(DIL_PAIRS):
        L = caches_k[gi].shape[1]
        assert L == win and L % dil == 0 and (dil == 1 or ts <= dil)
        nres = min(dil, ts)
        spec = pl.BlockSpec((1, L // dil, nres, H, hd), lambda b: (b, 0, 0, 0, 0))
        for c in (caches_k[gi], caches_v[gi]):
            in_specs.append(spec)
            args.append(c.reshape(B, L // dil, dil, H, hd))
    return pl.pallas_call(
        functools.partial(_dilated_sample_kernel, ts=ts, hd=hd),
        grid=(B,),
        in_specs=in_specs,
        out_specs=pl.BlockSpec((1, ts, H, hd), lambda b: (b, 0, 0, 0)),
        out_shape=jax.ShapeDtypeStruct((B, ts, H, hd), F32),
        compiler_params=_cparams("parallel"),
        name=name,
    )(*args)


def kernel(x_prompt, x_sample, state_ret, cache_k_w128, cache_v_w128, cache_k_w512, cache_v_w512,
           cache_k_w2048, cache_v_w2048, norm_mix, norm_ffn, ret_w_in, ret_gn, ret_w_out,
           kv_norm, w_kv, dil_w_q, dil_w_o, ffn_w1, ffn_w3, ffn_w2, norm_final):
    Bp, Tp, D = x_prompt.shape
    Bs, Ts, _ = x_sample.shape
    H, hd = DIL_HEADS, D // DIL_HEADS
    caches_k = [cache_k_w128, cache_k_w512, cache_k_w2048]
    caches_v = [cache_v_w128, cache_v_w512, cache_v_w2048]
    MP, MS = Bp * Tp, Bs * SAMPLE_PAD
    MT = MP + MS
    NT = MT // 16

    h = jnp.concatenate([x_prompt.reshape(MP, D),
                         jnp.pad(x_sample, ((0, 0), (0, SAMPLE_PAD - Ts), (0, 0))).reshape(MS, D)])

    def ffn(h, layer):
        xn, = _norm(h, [norm_ffn[layer]], out_dtype=BF16, tm=NT, name=f"ffn_norm_{layer}")
        act = _linear(xn, [ffn_w1[layer], ffn_w3[layer]], swiglu=True, out_dtype=BF16, tn=512,
                      name=f"ffn_up_{layer}")
        return _linear(act, [ffn_w2[layer]], residual=h, out_dtype=F32, tn=256, name=f"ffn_down_{layer}")

    chunk_p = RET_CHUNK if Tp % RET_CHUNK == 0 else Tp
    xn, = _norm(h, [norm_mix[0]], out_dtype=BF16, tm=NT, name="mix_norm_0")
    proj = _linear(xn, [ret_w_in[0]], out_dtype=BF16, tn=512, name="ret_in")
    s0_p = jnp.zeros((Bp,) + state_ret.shape[2:], F32)
    gated = jnp.zeros((MT, RET_HEADS * state_ret.shape[-1]), BF16)
    gated, sp = _retention(proj, s0_p, ret_gn[0], row0=0, batch=Bp, seq=Tp, chunk=chunk_p,
                           chunk_true=chunk_p, pos=jnp.arange(Tp), out_init=gated, name="retention_p")
    gated, ss = _retention(proj, state_ret[0], ret_gn[0], row0=MP, batch=Bs, seq=SAMPLE_PAD,
                           chunk=SAMPLE_PAD, chunk_true=Ts, pos=PAST_LEN + jnp.arange(SAMPLE_PAD),
                           out_init=gated, name="retention_s")
    h = _linear(gated, [ret_w_out[0]], residual=h, out_dtype=F32, tn=256, name="ret_out")
    h = ffn(h, 0)

    xkv, xq = _norm(h, [kv_norm, norm_mix[1]], out_dtype=BF16, tm=NT, name="kv_q_norm")
    qd, kd, vd, new_kp, new_vp = [], [], [], [], []
    for gi, (win, dil) in enumerate(DIL_PAIRS):
        assert Tp % win == 0
        qd.append(_dilated_linear(xq, dil_w_q[0], gi * D, batch=Bp, seq=Tp, dil=dil, name=f"dil_q_{gi}"))
        kg, kw = _dilated_linear(xkv, w_kv, 2 * gi * D, batch=Bp, seq=Tp, dil=dil, window=win,
                                 name=f"dil_k_{gi}")
        vg, vw = _dilated_linear(xkv, w_kv, (2 * gi + 1) * D, batch=Bp, seq=Tp, dil=dil, window=win,
                                 name=f"dil_v_{gi}")
        kd.append(kg)
        vd.append(vg)
        new_kp.append(kw.reshape(Bp, win, H, hd))
        new_vp.append(vw.reshape(Bp, win, H, hd))
    kv_s = _linear(xkv, [w_kv], out_dtype=F32, tn=1024, row0=MP, rows=MS, row_tiles=1, name="kv_s")
    q_s = _linear(xq, [dil_w_q[0]], out_dtype=F32, tn=1024, row0=MP, rows=MS, row_tiles=1, name="q_s")
    kv_s6 = kv_s.reshape(Bs, SAMPLE_PAD, N_GROUPS, 2, H, hd)[:, :Ts]
    knew, vnew = kv_s6[:, :, :, 0], kv_s6[:, :, :, 1]
    q_s5 = q_s.reshape(Bs, SAMPLE_PAD, N_GROUPS, H, hd)[:, :Ts]

    job = lambda c, a, i: (c[i], a[:, :, i])
    o0, l0, (nk2,) = _dilated_prompt(qd[0], kd[0], vd[0], 0, shift_jobs=[job(caches_k, knew, 2)],
                                     name="dilated_p0")
    o1, l1, (nk0, nv0, nk1, nv1) = _dilated_prompt(
        qd[1], kd[1], vd[1], 1, name="dilated_p1",
        shift_jobs=[job(caches_k, knew, 0), job(caches_v, vnew, 0),
                    job(caches_k, knew, 1), job(caches_v, vnew, 1)])
    o2, l2, (nv2,) = _dilated_prompt(qd[2], kd[2], vd[2], 2, shift_jobs=[job(caches_v, vnew, 2)],
                                     name="dilated_p2")
    a_s = _dilated_sample(q_s5, knew, vnew, caches_k, caches_v, name="dilated_s")
    a_s = jnp.pad(a_s.reshape(Bs, Ts, D), ((0, 0), (0, SAMPLE_PAD - Ts), (0, 0))).reshape(MS, D)
    a = jnp.concatenate([jnp.zeros((MP, D), BF16), a_s.astype(BF16)])
    a = _merge([o0, o1, o2], [l0, l1, l2], a, name="dilated_merge")
    h = _linear(a, [dil_w_o[0]], residual=h, out_dtype=F32, tn=512, name="dil_o")
    h = ffn(h, 1)

    y_p, = _norm(h, [norm_final], out_dtype=F32, tm=MS, rows=MP, name="final_norm_p")
    y_s, = _norm(h, [norm_final], out_dtype=F32, tm=MS, row0=MP, rows=MS, name="final_norm_s")
    return (y_p.reshape(Bp, Tp, D), y_s.reshape(Bs, SAMPLE_PAD, D)[:, :Ts], sp[None], ss[None],
            new_kp[0], new_vp[0], new_kp[1], new_vp[1], new_kp[2], new_vp[2],
            nk0, nv0, nk1, nv1, nk2, nv2)
```

```python
import functools

import jax
import jax.numpy as jnp
from jax import lax
from jax.experimental import pallas as pl
from jax.experimental.pallas import tpu as pltpu

F32 = jnp.float32
BF16 = jnp.bfloat16

RET_HEADS = 8
RET_CHUNK = 128
ROPE_BASE = 10000.0
DIL_PAIRS = ((128, 1), (512, 4), (2048, 16))
N_GROUPS = len(DIL_PAIRS)
DIL_HEADS = 16
PAST_LEN = 8192
EPS = 1e-6
NEG = -1e30

LANES = 128
SAMPLE_PAD = 16
VMEM_LIMIT = 61 * 1024 * 1024
ROW_TILES = 4


def _cparams(*sem):
    return pltpu.CompilerParams(dimension_semantics=sem, vmem_limit_bytes=VMEM_LIMIT)


def _norm_kernel(x_ref, g_ref, *o_refs):
    xf = x_ref[...]
    ms = jnp.mean(xf * xf, axis=-1, keepdims=True)
    xs = xf * lax.rsqrt(ms + EPS)
    for k, o_ref in enumerate(o_refs):
        o_ref[...] = (xs * g_ref[k:k + 1, :]).astype(o_ref.dtype)


def _norm(x, gains, *, out_dtype, tm, row0=0, rows=None, name):
    M, K = x.shape
    rows = M if rows is None else rows
    assert rows % tm == 0 and row0 % tm == 0
    g = jnp.stack(gains).astype(F32)
    off = row0 // tm
    outs = pl.pallas_call(
        _norm_kernel,
        grid=(rows // tm,),
        in_specs=[pl.BlockSpec((tm, K), lambda i: (i + off, 0)),
                  pl.BlockSpec((len(gains), K), lambda i: (0, 0))],
        out_specs=[pl.BlockSpec((tm, K), lambda i: (i, 0))] * len(gains),
        out_shape=[jax.ShapeDtypeStruct((rows, K), out_dtype)] * len(gains),
        compiler_params=_cparams("parallel"),
        name=name,
    )(x, g)
    return outs


def _linear_kernel(*refs, n_w, has_res, swiglu):
    it = iter(refs)
    x_ref = next(it)
    w_refs = [next(it) for _ in range(n_w)]
    r_ref = next(it) if has_res else None
    o_ref = next(it)
    xn = x_ref[...]
    a = jnp.dot(xn, w_refs[0][...].astype(BF16), preferred_element_type=F32)
    if swiglu:
        b = jnp.dot(xn, w_refs[1][...].astype(BF16), preferred_element_type=F32)
        a = (a * jax.nn.sigmoid(a)) * b
    if has_res:
        a = a + r_ref[...]
    o_ref[...] = a.astype(o_ref.dtype)


def _linear(x, ws, *, residual=None, swiglu=False, out_dtype, tn, row0=0, rows=None,
            row_tiles=ROW_TILES, name):
    K = x.shape[1]
    M = x.shape[0] if rows is None else rows
    N = ws[0].shape[1]
    assert M % (row_tiles * 16) == 0 and N % tn == 0 and x.dtype == BF16
    tm = M // row_tiles
    assert row0 % tm == 0 and (residual is None or row0 == 0)
    off = row0 // tm
    has_res = residual is not None
    in_specs = [pl.BlockSpec((tm, K), lambda i, j: (i + off, 0), pipeline_mode=pl.Buffered(1))]
    args = [x]
    for w in ws:
        in_specs.append(pl.BlockSpec((K, tn), lambda i, j: (0, j)))
        args.append(w)
    if has_res:
        in_specs.append(pl.BlockSpec((tm, tn), lambda i, j: (i, j)))
        args.append(residual)
    kern = functools.partial(_linear_kernel, n_w=len(ws), has_res=has_res, swiglu=swiglu)
    return pl.pallas_call(
        kern,
        grid=(row_tiles, N // tn),
        in_specs=in_specs,
        out_specs=pl.BlockSpec((tm, tn), lambda i, j: (i, j)),
        out_shape=jax.ShapeDtypeStruct((M, N), out_dtype),
        compiler_params=_cparams("parallel", "arbitrary"),
        name=name,
    )(*args)


MXU_COLS = 256


def _dilated_linear_kernel(x_ref, w_ref, *refs, dil, win_rows):
    it = iter(refs)
    o_ref = next(it)
    win_ref = next(it) if win_rows else None
    acc_ref = next(it) if dil > 1 else None
    tm, tn = x_ref.shape[0], w_ref.shape[1]
    x = x_ref[...]
    for c in range(tn // MXU_COLS):
        cs = slice(c * MXU_COLS, (c + 1) * MXU_COLS)
        acc = jnp.dot(x, w_ref[:, cs].astype(BF16), preferred_element_type=F32)
        if win_rows:
            win_ref[0, :, cs] = acc[tm - win_rows:, :]
        if dil == 1:
            o_ref[0, 0, :, cs] = acc.astype(o_ref.dtype)
        else:
            for s in range(c * MXU_COLS // LANES, (c + 1) * MXU_COLS // LANES):
                ls = slice(s * LANES, (s + 1) * LANES)
                acc_ref[s] = acc[:, ls.start - c * MXU_COLS:ls.stop - c * MXU_COLS]
                for r in range(dil):
                    o_ref[0, r, :, ls] = acc_ref[s, pl.ds(r, tm // dil, stride=dil), :].astype(o_ref.dtype)


def _dilated_linear(x, w, col0, *, batch, seq, dil, window=None, tm=2048, tn=512, name):
    K = x.shape[1]
    width = DIL_HEADS * (K // DIL_HEADS)
    assert seq % tm == 0 and tm % (dil * 16) == 0 and col0 % tn == 0 and width % tn == 0
    tpb = seq // tm
    cb0 = col0 // tn
    in_specs = [pl.BlockSpec((tm, K), lambda i, j: (i, 0), pipeline_mode=pl.Buffered(1)),
                pl.BlockSpec((K, tn), lambda i, j: (0, cb0 + j))]
    out_specs = [pl.BlockSpec((1, dil, tm // dil, tn), lambda i, j: (i // tpb, 0, i % tpb, j))]
    out_shape = [jax.ShapeDtypeStruct((batch, dil, seq // dil, width), BF16)]
    win_rows = 0
    if window is not None:
        win_rows = min(window, tm)
        assert window % win_rows == 0
        t0 = tpb - window // win_rows

        def win_map(i, j):
            t = i % tpb
            inside = t >= t0
            return (i // tpb, jnp.where(inside, t - t0, 0), jnp.where(inside, j, 0))

        out_specs.append(pl.BlockSpec((1, win_rows, tn), win_map))
        out_shape.append(jax.ShapeDtypeStruct((batch, window, width), F32))
    scratch = [pltpu.VMEM((tn // LANES, tm, LANES), F32)] if dil > 1 else []
    res = pl.pallas_call(
        functools.partial(_dilated_linear_kernel, dil=dil, win_rows=win_rows),
        grid=(batch * tpb, width // tn),
        in_specs=in_specs,
        out_specs=out_specs,
        out_shape=out_shape,
        scratch_shapes=scratch,
        compiler_params=_cparams("arbitrary", "arbitrary"),
        name=name,
    )(x, w)
    return res if window is not None else res[0]


def _retention_kernel(*refs, dk, dv, hps, chunk, nc):
    (q_ref, k_ref, v_ref, g_ref, cos_ref, sin_ref, dmat_ref, qdec_ref, kdec_ref,
     cdec_ref, gn_ref, s0_ref) = refs[:12]
    o_ref, s_ref = refs[-2:]

    s_ref[...] = s0_ref[...]
    half = dk // 2

    def one_chunk(c):
        rows = slice(None) if nc == 1 else pl.ds(pl.multiple_of(c * chunk, chunk), chunk)
        cos = cos_ref[rows, :]
        sin = sin_ref[rows, :]

        def rope(x):
            x1 = x[:, :half]
            x2 = x[:, half:]
            return jnp.concatenate([x1 * cos - x2 * sin, x1 * sin + x2 * cos], axis=-1)

        for j in range(hps):
            q = rope(q_ref[rows, j * dk:(j + 1) * dk].astype(F32))
            k = rope(k_ref[rows, j * dk:(j + 1) * dk].astype(F32)) * (dk ** -0.5)
            v = v_ref[rows, j * dv:(j + 1) * dv]
            s = s_ref[0, j]
            sc = lax.dot_general(q.astype(BF16), k.astype(BF16), (((1,), (1,)), ((), ())),
                                 preferred_element_type=F32) * dmat_ref[j]
            o = (jnp.dot(sc.astype(BF16), v, preferred_element_type=F32)
                 + jnp.dot((q * qdec_ref[j]).astype(BF16), s.astype(BF16), preferred_element_type=F32))
            kd = (k * kdec_ref[j]).astype(BF16)
            s_ref[0, j] = s * cdec_ref[j] + lax.dot_general(kd, v, (((0,), (0,)), ((), ())),
                                                           preferred_element_type=F32)
            mu = jnp.mean(o, axis=-1, keepdims=True)
            var = jnp.mean(jnp.square(o - mu), axis=-1, keepdims=True)
            on = (o - mu) * lax.rsqrt(var + EPS) * gn_ref[:, j * dv:(j + 1) * dv]
            gate = g_ref[rows, j * dv:(j + 1) * dv].astype(F32)
            o_ref[rows, j * dv:(j + 1) * dv] = ((gate * jax.nn.sigmoid(gate)) * on).astype(o_ref.dtype)

    if nc == 1:
        one_chunk(0)
    else:
        def body(c, carry):
            one_chunk(c)
            return carry

        lax.fori_loop(0, nc, body, 0)


def _retention(proj, s0, gn_g, *, row0, batch, seq, chunk, chunk_true, pos, out_init=None, name):
    H = RET_HEADS
    dk, dv = s0.shape[-2], s0.shape[-1]
    nc = seq // chunk
    assert row0 % chunk == 0
    lg = jnp.log1p(-jnp.exp2(-5.0 - jnp.arange(H, dtype=F32)))
    i = jnp.arange(chunk, dtype=F32)
    dist = i[:, None] - i[None, :]
    dmat = jnp.where(dist >= 0, jnp.exp(jnp.maximum(dist, 0.0)[None] * lg[:, None, None]), 0.0)
    qdec = jnp.exp((i + 1.0)[None, :, None] * lg[:, None, None])
    kdec = jnp.exp((chunk_true - 1.0 - i)[None, :, None] * lg[:, None, None])
    cdec = jnp.exp(chunk_true * lg).reshape(H, 1, 1)
    half = dk // 2
    inv = ROPE_BASE ** (-jnp.arange(half, dtype=F32) / half)
    ang = pos.astype(F32)[:, None] * inv[None, :]
    cos, sin = jnp.cos(ang), jnp.sin(ang)

    hps = H if seq * H * (2 * dk + 2 * dv) * 2 <= (1 << 20) else 1
    hg = H // hps
    kb, vb = hg, (2 * H * dk) // (hps * dv)
    gb = vb + hg
    assert row0 % seq == 0
    rb = row0 // seq
    in_specs = [
        pl.BlockSpec((seq, hps * dk), lambda b, h: (rb + b, h)),
        pl.BlockSpec((seq, hps * dk), lambda b, h: (rb + b, kb + h)),
        pl.BlockSpec((seq, hps * dv), lambda b, h: (rb + b, vb + h)),
        pl.BlockSpec((seq, hps * dv), lambda b, h: (rb + b, gb + h)),
        pl.BlockSpec((seq, half), lambda b, h: (0, 0)),
        pl.BlockSpec((seq, half), lambda b, h: (0, 0)),
        pl.BlockSpec((hps, chunk, chunk), lambda b, h: (h, 0, 0)),
        pl.BlockSpec((hps, chunk, 1), lambda b, h: (h, 0, 0)),
        pl.BlockSpec((hps, chunk, 1), lambda b, h: (h, 0, 0)),
        pl.BlockSpec((hps, 1, 1), lambda b, h: (h, 0, 0)),
        pl.BlockSpec((1, hps * dv), lambda b, h: (0, h)),
        pl.BlockSpec((1, hps, dk, dv), lambda b, h: (b, h, 0, 0)),
    ]
    args = [proj, proj, proj, proj, cos, sin, dmat, qdec, kdec, cdec, gn_g.reshape(1, H * dv), s0]
    aliases = {}
    if out_init is not None:
        in_specs.append(pl.BlockSpec(memory_space=pl.ANY))
        args.append(out_init)
        aliases = {len(args) - 1: 0}
    out_specs = [
        pl.BlockSpec((seq, hps * dv), lambda b, h: (rb + b, h)),
        pl.BlockSpec((1, hps, dk, dv), lambda b, h: (b, h, 0, 0)),
    ]
    out_shape = [jax.ShapeDtypeStruct((proj.shape[0], H * dv), BF16),
                 jax.ShapeDtypeStruct((batch, H, dk, dv), F32)]
    return pl.pallas_call(
        functools.partial(_retention_kernel, dk=dk, dv=dv, hps=hps, chunk=chunk, nc=nc),
        grid=(batch, hg),
        in_specs=in_specs,
        out_specs=out_specs,
        out_shape=out_shape,
        input_output_aliases=aliases,
        compiler_params=_cparams("parallel", "parallel"),
        name=name,
    )(*args)


def _dilated_prompt_kernel(*refs, heads, hd, jobs, ts, steps):
    q_ref, k_ref, v_ref = refs[:3]
    nj = len(jobs)
    caches, news = refs[3:3 + nj], refs[3 + nj:3 + 2 * nj]
    o_ref, l_ref = refs[3 + 2 * nj:5 + 2 * nj]
    shifted = refs[5 + 2 * nj:5 + 3 * nj]
    kp_ref, vp_ref, buf, in_sem, out_sem = refs[5 + 3 * nj:]

    n = pl.program_id(2)
    blk = q_ref.shape[2]
    step = (pl.program_id(0) * pl.num_programs(1) + pl.program_id(1)) * pl.num_programs(2) + n
    last = steps - 1

    def plan(h):
        out, off = [], 0
        for j, (halves) in enumerate(jobs):
            r0, nr, tail = halves[h]
            if nr:
                out.append((j, r0, nr, off, tail))
                off += nr + (ts if tail else 0)
        return out

    def in_copies(h, bc):
        return [pltpu.make_async_copy(caches[j].at[bc, pl.ds(ts + r0, nr)],
                                      buf.at[h, pl.ds(off, nr)], in_sem.at[h])
                for (j, r0, nr, off, tail) in plan(h)]

    def out_copies(h, bc):
        return [pltpu.make_async_copy(buf.at[h, pl.ds(off, nr + (ts if tail else 0))],
                                      shifted[j].at[bc, pl.ds(r0, nr + (ts if tail else 0))],
                                      out_sem.at[h])
                for (j, r0, nr, off, tail) in plan(h)]

    @pl.when(step == 0)
    def _():
        for cp in in_copies(0, 0):
            cp.start()

    for h in range(2):
        @pl.when(step % 2 == h)
        def _(h=h):
            bc = step // 2
            for cp in in_copies(h, bc):
                cp.wait()
            for (j, r0, nr, off, tail) in plan(h):
                if tail:
                    buf[h, pl.ds(off + nr, ts)] = news[j][bc]
            for cp in out_copies(h, bc):
                cp.start()

            @pl.when(step > 0)
            def _():
                for cp in out_copies(1 - h, (step - 1) // 2):
                    cp.wait()

            @pl.when(step < last)
            def _():
                for cp in in_copies(1 - h, (step + 1) // 2):
                    cp.start()

    @pl.when(n == 0)
    def _():
        kp_ref[...] = jnp.zeros_like(kp_ref)
        vp_ref[...] = jnp.zeros_like(vp_ref)

    row = lax.broadcasted_iota(jnp.int32, (blk, blk), 0)
    col = lax.broadcasted_iota(jnp.int32, (blk, blk), 1)
    mask_c = col <= row
    mask_p = (col - row) >= jnp.where(n > 0, 0, blk)
    lane = lax.broadcasted_iota(jnp.int32, (blk, LANES), 1)
    scale = hd ** -0.5
    nt = (((1,), (1,)), ((), ()))
    hpt = heads // l_ref.shape[0]

    def qk(cs, key_ref, mask):
        s = lax.dot_general(q_ref[0, 0, :, cs], key_ref[:, cs], nt, preferred_element_type=F32)
        return jnp.where(mask, s * scale, NEG)

    kc_ref, vc_ref = k_ref.at[0, 0], v_ref.at[0, 0]
    for g0 in range(0, heads, hpt):
        cols = [slice(h * hd, (h + 1) * hd) for h in range(g0, g0 + hpt)]
        s_c = [qk(cs, kc_ref, mask_c) for cs in cols]
        s_p = [qk(cs, kp_ref, mask_p) for cs in cols]
        m = [jnp.maximum(jnp.max(a, axis=-1, keepdims=True), jnp.max(b, axis=-1, keepdims=True))
             for a, b in zip(s_c, s_p)]
        p_c = [jnp.exp(a - mm) for a, mm in zip(s_c, m)]
        p_p = [jnp.exp(b - mm) for b, mm in zip(s_p, m)]
        den = [jnp.sum(a, axis=-1, keepdims=True) + jnp.sum(b, axis=-1, keepdims=True)
               for a, b in zip(p_c, p_p)]
        inv = [1.0 / d for d in den]
        lse_tile = jnp.zeros((blk, LANES), F32)
        for i, cs in enumerate(cols):
            o_ref[0, 0, :, cs] = (
                jnp.dot((p_c[i] * inv[i]).astype(BF16), vc_ref[:, cs], preferred_element_type=F32)
                + jnp.dot((p_p[i] * inv[i]).astype(BF16), vp_ref[:, cs], preferred_element_type=F32))
            lse_tile = jnp.where(lane == i, m[i] + jnp.log(den[i]), lse_tile)
        l_ref[g0 // hpt, 0, 0] = lse_tile

    kp_ref[...] = k_ref[0, 0]
    vp_ref[...] = v_ref[0, 0]

    @pl.when(step == last)
    def _():
        for cp in out_copies(last % 2, last // 2):
            cp.wait()


LSE_TILES = 4


def _dilated_prompt(q, k, v, gi, *, shift_jobs, name):
    win, dil = DIL_PAIRS[gi]
    blk = win // dil
    H = DIL_HEADS
    batch, _, sd, width = q.shape
    hd = width // H
    assert q.shape[1] == dil and sd % blk == 0
    nb = sd // blk
    slab = pl.BlockSpec((1, 1, blk, width), lambda b, r, n: (b, r, n, 0))
    in_specs = [slab, slab, slab]
    args = [q, k, v]
    out_specs = [slab, pl.BlockSpec((LSE_TILES, 1, 1, blk, LANES), lambda b, r, n: (0, b, r, n, 0))]
    out_shape = [jax.ShapeDtypeStruct(q.shape, F32),
                 jax.ShapeDtypeStruct((LSE_TILES, batch, dil, sd, LANES), F32)]

    steps = batch * dil * nb
    caches = [c for c, _ in shift_jobs]
    news = [a for _, a in shift_jobs]
    ts = news[0].shape[1]
    Bc = caches[0].shape[0]
    assert steps == 2 * Bc
    jobs, slot_rows = [], [0, 0]
    for c in caches:
        keep = c.shape[1] - ts
        first = keep // 2 if keep * c.shape[2] * c.shape[3] * 4 > (2 << 20) else 0
        halves = ((0, first, False), (first, keep - first, True))
        jobs.append(halves)
        for h in range(2):
            slot_rows[h] += halves[h][1] + (ts if halves[h][2] else 0)
    any_spec = pl.BlockSpec(memory_space=pl.ANY)
    in_specs += [any_spec] * len(caches) + [pl.BlockSpec(memory_space=pltpu.VMEM)] * len(news)
    args += caches + news
    out_specs = out_specs + [any_spec] * len(caches)
    out_shape = out_shape + [jax.ShapeDtypeStruct(c.shape, c.dtype) for c in caches]

    res = pl.pallas_call(
        functools.partial(_dilated_prompt_kernel, heads=H, hd=hd, jobs=tuple(jobs), ts=ts, steps=steps),
        grid=(batch, dil, nb),
        in_specs=in_specs,
        out_specs=out_specs,
        out_shape=out_shape,
        scratch_shapes=[pltpu.VMEM((blk, width), BF16), pltpu.VMEM((blk, width), BF16),
                        pltpu.VMEM((2, max(slot_rows)) + caches[0].shape[2:], caches[0].dtype),
                        pltpu.SemaphoreType.DMA((2,)), pltpu.SemaphoreType.DMA((2,))],
        compiler_params=_cparams("arbitrary", "arbitrary", "arbitrary"),
        name=name,
    )(*args)
    return res[0], res[1], list(res[2:])


def _merge_kernel(o0_ref, o1_ref, o2_ref, l0_ref, l1_ref, l2_ref, init_ref, out_ref,
                  s0_ref, s1_ref, nat_ref, *, hd):
    del init_ref
    d1, d2 = o1_ref.shape[1], o2_ref.shape[1]
    rows = o2_ref.shape[2]
    rep = d2 // d1
    nh = o2_ref.shape[3] // hd
    for hh in range(nh):
        cs = slice(hh * hd, (hh + 1) * hd)
        s0_ref[hh] = o0_ref[:, cs]
        s1_ref[hh] = o1_ref[0, :, :, cs]
    for r in range(d2):
        l2 = l2_ref[0, 0, r]
        l1 = l1_ref[0, 0, r % d1, pl.ds(r // d1, rows, stride=rep), :]
        l0 = l0_ref[0, pl.ds(r, rows, stride=d2), :]
        mx = jnp.maximum(jnp.maximum(l0, l1), l2)
        e0 = jnp.exp(l0 - mx)
        e1 = jnp.exp(l1 - mx)
        e2 = jnp.exp(l2 - mx)
        tot = 1.0 / (e0 + e1 + e2)
        w0, w1, w2 = e0 * tot, e1 * tot, e2 * tot
        for hh in range(nh):
            ls = slice(hh, hh + 1)
            o2 = o2_ref[0, r, :, hh * hd:(hh + 1) * hd]
            o1 = s1_ref[hh, r % d1, pl.ds(r // d1, rows, stride=rep), :]
            o0 = s0_ref[hh, pl.ds(r, rows, stride=d2), :]
            nat_ref[hh, pl.ds(r, rows, stride=d2), :] = w0[:, ls] * o0 + w1[:, ls] * o1 + w2[:, ls] * o2
    for hh in range(nh):
        out_ref[:, hh * hd:(hh + 1) * hd] = nat_ref[hh].astype(out_ref.dtype)


def _merge(outs, lses, init, *, name):
    (_, d0), (_, d1), (_, d2) = DIL_PAIRS
    batch, _, sd2, width = outs[2].shape
    seq = sd2 * d2
    rows = DIL_PAIRS[2][0] // d2
    span = rows * d2
    assert d0 == 1 and d2 % d1 == 0 and seq % span == 0
    nsp = seq // span
    cw = width // LSE_TILES
    hd = width // DIL_HEADS
    o0 = outs[0].reshape(batch * seq, width)
    l0 = lses[0].reshape(LSE_TILES, batch * seq, LANES)
    in_specs = [
        pl.BlockSpec((span, cw), lambda t, c: (t, c)),
        pl.BlockSpec((1, d1, span // d1, cw), lambda t, c: (t // nsp, 0, t % nsp, c)),
        pl.BlockSpec((1, d2, rows, cw), lambda t, c: (t // nsp, 0, t % nsp, c)),
        pl.BlockSpec((1, span, LANES), lambda t, c: (c, t, 0)),
        pl.BlockSpec((1, 1, d1, span // d1, LANES), lambda t, c: (c, t // nsp, 0, t % nsp, 0)),
        pl.BlockSpec((1, 1, d2, rows, LANES), lambda t, c: (c, t // nsp, 0, t % nsp, 0)),
        pl.BlockSpec(memory_space=pl.ANY),
    ]
    return pl.pallas_call(
        functools.partial(_merge_kernel, hd=hd),
        grid=(batch * nsp, LSE_TILES),
        in_specs=in_specs,
        out_specs=pl.BlockSpec((span, cw), lambda t, c: (t, c)),
        out_shape=jax.ShapeDtypeStruct(init.shape, init.dtype),
        scratch_shapes=[pltpu.VMEM((cw // hd, span, hd), F32),
                        pltpu.VMEM((cw // hd, d1, span // d1, hd), F32),
                        pltpu.VMEM((cw // hd, span, hd), F32)],
        input_output_aliases={6: 0},
        compiler_params=_cparams("parallel", "parallel"),
        name=name,
    )(o0, outs[1], outs[2], l0, lses[1], lses[2], init)


def _dilated_sample_kernel(q_ref, kn_ref, vn_ref, k0_ref, v0_ref, k1_ref, v1_ref, k2_ref, v2_ref,
                           o_ref, *, ts, hd):
    caches = ((k0_ref, v0_ref), (k1_ref, v1_ref), (k2_ref, v2_ref))
    scale = hd ** -0.5
    for t in range(ts):
        outs, lses = [], []
        for gi, (win, dil) in enumerate(DIL_PAIRS):
            kc_ref, vc_ref = caches[gi]
            q = q_ref[0, t, gi]
            r = t % dil if dil > 1 else 0
            kc = kc_ref[0, :, r]
            vc = vc_ref[0, :, r]
            s_c = jnp.sum(kc * q[None], axis=-1, keepdims=True) * scale
            if dil == 1:
                cidx = lax.broadcasted_iota(jnp.int32, s_c.shape, 0)
                s_c = jnp.where(cidx >= t, s_c, NEG)
                new_rows = list(range(t + 1))
            else:
                new_rows = [t]
            s_n = [jnp.sum(kn_ref[0, u, gi] * q, axis=-1, keepdims=True) * scale for u in new_rows]
            m = jnp.max(s_c, axis=0)
            for sn in s_n:
                m = jnp.maximum(m, sn)
            p_c = jnp.exp(s_c - m[None])
            p_n = [jnp.exp(sn - m) for sn in s_n]
            den = jnp.sum(p_c, axis=0)
            for pn in p_n:
                den = den + pn
            inv = 1.0 / den
            o = jnp.sum((p_c * inv[None]) * vc, axis=0)
            for u, pn in zip(new_rows, p_n):
                o = o + (pn * inv) * vn_ref[0, u, gi]
            outs.append(o)
            lses.append(m + jnp.log(den))
        mx = jnp.maximum(jnp.maximum(lses[0], lses[1]), lses[2])
        es = [jnp.exp(l - mx) for l in lses]
        tot = 1.0 / (es[0] + es[1] + es[2])
        o_ref[0, t] = (es[0] * tot) * outs[0] + (es[1] * tot) * outs[1] + (es[2] * tot) * outs[2]


def _dilated_sample(q, knew, vnew, caches_k, caches_v, *, name):
    B, ts, G, H, hd = q.shape
    small = pl.BlockSpec((1, ts, G, H, hd), lambda b: (b, 0, 0, 0, 0))
    in_specs = [small, small, small]
    args = [q, knew, vnew]
    for gi, (win, dil) in enumerate(DIL_PAIRS):
        L = caches_k[gi].shape[1]
        assert L == win and L % dil == 0 and (dil == 1 or ts <= dil)
        nres = min(dil, ts)
        spec = pl.BlockSpec((1, L // dil, nres, H, hd), lambda b: (b, 0, 0, 0, 0))
        for c in (caches_k[gi], caches_v[gi]):
            in_specs.append(spec)
            args.append(c.reshape(B, L // dil, dil, H, hd))
    return pl.pallas_call(
        functools.partial(_dilated_sample_kernel, ts=ts, hd=hd),
        grid=(B,),
        in_specs=in_specs,
        out_specs=pl.BlockSpec((1, ts, H, hd), lambda b: (b, 0, 0, 0)),
        out_shape=jax.ShapeDtypeStruct((B, ts, H, hd), F32),
        compiler_params=_cparams("parallel"),
        name=name,
    )(*args)


def kernel(x_prompt, x_sample, state_ret, cache_k_w128, cache_v_w128, cache_k_w512, cache_v_w512,
           cache_k_w2048, cache_v_w2048, norm_mix, norm_ffn, ret_w_in, ret_gn, ret_w_out,
           kv_norm, w_kv, dil_w_q, dil_w_o, ffn_w1, ffn_w3, ffn_w2, norm_final):
    Bp, Tp, D = x_prompt.shape
    Bs, Ts, _ = x_sample.shape
    H, hd = DIL_HEADS, D // DIL_HEADS
    caches_k = [cache_k_w128, cache_k_w512, cache_k_w2048]
    caches_v = [cache_v_w128, cache_v_w512, cache_v_w2048]
    MP, MS = Bp * Tp, Bs * SAMPLE_PAD
    MT = MP + MS
    NT = MT // 16

    h = jnp.concatenate([x_prompt.reshape(MP, D),
                         jnp.pad(x_sample, ((0, 0), (0, SAMPLE_PAD - Ts), (0, 0))).reshape(MS, D)])

    def ffn(h, layer):
        xn, = _norm(h, [norm_ffn[layer]], out_dtype=BF16, tm=NT, name=f"ffn_norm_{layer}")
        act = _linear(xn, [ffn_w1[layer], ffn_w3[layer]], swiglu=True, out_dtype=BF16, tn=512,
                      name=f"ffn_up_{layer}")
        return _linear(act, [ffn_w2[layer]], residual=h, out_dtype=F32, tn=256, name=f"ffn_down_{layer}")

    chunk_p = RET_CHUNK if Tp % RET_CHUNK == 0 else Tp
    xn, = _norm(h, [norm_mix[0]], out_dtype=BF16, tm=NT, name="mix_norm_0")
    proj = _linear(xn, [ret_w_in[0]], out_dtype=BF16, tn=512, name="ret_in")
    s0_p = jnp.zeros((Bp,) + state_ret.shape[2:], F32)
    gated = jnp.zeros((MT, RET_HEADS * state_ret.shape[-1]), BF16)
    gated, sp = _retention(proj, s0_p, ret_gn[0], row0=0, batch=Bp, seq=Tp, chunk=chunk_p,
                           chunk_true=chunk_p, pos=jnp.arange(Tp), out_init=gated, name="retention_p")
    gated, ss = _retention(proj, state_ret[0], ret_gn[0], row0=MP, batch=Bs, seq=SAMPLE_PAD,
                           chunk=SAMPLE_PAD, chunk_true=Ts, pos=PAST_LEN + jnp.arange(SAMPLE_PAD),
                           out_init=gated, name="retention_s")
    h = _linear(gated, [ret_w_out[0]], residual=h, out_dtype=F32, tn=256, name="ret_out")
    h = ffn(h, 0)

    xkv, xq = _norm(h, [kv_norm, norm_mix[1]], out_dtype=BF16, tm=NT, name="kv_q_norm")
    qd, kd, vd, new_kp, new_vp = [], [], [], [], []
    for gi, (win, dil) in enumerate(DIL_PAIRS):
        assert Tp % win == 0
        qd.append(_dilated_linear(xq, dil_w_q[0], gi * D, batch=Bp, seq=Tp, dil=dil, name=f"dil_q_{gi}"))
        kg, kw = _dilated_linear(xkv, w_kv, 2 * gi * D, batch=Bp, seq=Tp, dil=dil, window=win,
                                 name=f"dil_k_{gi}")
        vg, vw = _dilated_linear(xkv, w_kv, (2 * gi + 1) * D, batch=Bp, seq=Tp, dil=dil, window=win,
                                 name=f"dil_v_{gi}")
        kd.append(kg)
        vd.append(vg)
        new_kp.append(kw.reshape(Bp, win, H, hd))
        new_vp.append(vw.reshape(Bp, win, H, hd))
    kv_s = _linear(xkv, [w_kv], out_dtype=F32, tn=1024, row0=MP, rows=MS, row_tiles=1, name="kv_s")
    q_s = _linear(xq, [dil_w_q[0]], out_dtype=F32, tn=1024, row0=MP, rows=MS, row_tiles=1, name="q_s")
    kv_s6 = kv_s.reshape(Bs, SAMPLE_PAD, N_GROUPS, 2, H, hd)[:, :Ts]
    knew, vnew = kv_s6[:, :, :, 0], kv_s6[:, :, :, 1]
    q_s5 = q_s.reshape(Bs, SAMPLE_PAD, N_GROUPS, H, hd)[:, :Ts]

    job = lambda c, a, i: (c[i], a[:, :, i])
    o0, l0, (nk2,) = _dilated_prompt(qd[0], kd[0], vd[0], 0, shift_jobs=[job(caches_k, knew, 2)],
                                     name="dilated_p0")
    o1, l1, (nk0, nv0, nk1, nv1) = _dilated_prompt(
        qd[1], kd[1], vd[1], 1, name="dilated_p1",
        shift_jobs=[job(caches_k, knew, 0), job(caches_v, vnew, 0),
                    job(caches_k, knew, 1), job(caches_v, vnew, 1)])
    o2, l2, (nv2,) = _dilated_prompt(qd[2], kd[2], vd[2], 2, shift_jobs=[job(caches_v, vnew, 2)],
                                     name="dilated_p2")
    a_s = _dilated_sample(q_s5, knew, vnew, caches_k, caches_v, name="dilated_s")
    a_s = jnp.pad(a_s.reshape(Bs, Ts, D), ((0, 0), (0, SAMPLE_PAD - Ts), (0, 0))).reshape(MS, D)
    a = jnp.concatenate([jnp.zeros((MP, D), BF16), a_s.astype(BF16)])
    a = _merge([o0, o1, o2], [l0, l1, l2], a, name="dilated_merge")
    h = _linear(a, [dil_w_o[0]], residual=h, out_dtype=F32, tn=512, name="dil_o")
    h = ffn(h, 1)

    y_p, = _norm(h, [norm_final], out_dtype=F32, tm=MS, rows=MP, name="final_norm_p")
    y_s, = _norm(h, [norm_final], out_dtype=F32, tm=MS, row0=MP, rows=MS, name="final_norm_s")
    return (y_p.reshape(Bp, Tp, D), y_s.reshape(Bs, SAMPLE_PAD, D)[:, :Ts], sp[None], ss[None],
            new_kp[0], new_vp[0], new_kp[1], new_vp[1], new_kp[2], new_vp[2],
            nk0, nv0, nk1, nv1, nk2, nv2)
```

```python
import functools

import jax
import jax.numpy as jnp
from jax import lax
from jax.experimental import pallas as pl
from jax.experimental.pallas import tpu as pltpu

F32 = jnp.float32
BF16 = jnp.bfloat16

RET_HEADS = 8
RET_CHUNK = 128
ROPE_BASE = 10000.0
DIL_PAIRS = ((128, 1), (512, 4), (2048, 16))
N_GROUPS = len(DIL_PAIRS)
DIL_HEADS = 16
PAST_LEN = 8192
EPS = 1e-6
NEG = -1e30

LANES = 128
SAMPLE_PAD = 16
VMEM_LIMIT = 61 * 1024 * 1024
ROW_TILES = 4


def _cparams(*sem):
    return pltpu.CompilerParams(dimension_semantics=sem, vmem_limit_bytes=VMEM_LIMIT)


def _norm_kernel(x_ref, g_ref, *o_refs):
    xf = x_ref[...]
    ms = jnp.mean(xf * xf, axis=-1, keepdims=True)
    xs = xf * lax.rsqrt(ms + EPS)
    for k, o_ref in enumerate(o_refs):
        o_ref[...] = (xs * g_ref[k:k + 1, :]).astype(o_ref.dtype)


def _norm(x, gains, *, out_dtype, tm, row0=0, rows=None, name):
    M, K = x.shape
    rows = M if rows is None else rows
    assert rows % tm == 0 and row0 % tm == 0
    g = jnp.stack(gains).astype(F32)
    off = row0 // tm
    outs = pl.pallas_call(
        _norm_kernel,
        grid=(rows // tm,),
        in_specs=[pl.BlockSpec((tm, K), lambda i: (i + off, 0)),
                  pl.BlockSpec((len(gains), K), lambda i: (0, 0))],
        out_specs=[pl.BlockSpec((tm, K), lambda i: (i, 0))] * len(gains),
        out_shape=[jax.ShapeDtypeStruct((rows, K), out_dtype)] * len(gains),
        compiler_params=_cparams("parallel"),
        name=name,
    )(x, g)
    return outs


def _linear_kernel(*refs, n_w, has_res, swiglu):
    it = iter(refs)
    x_ref = next(it)
    w_refs = [next(it) for _ in range(n_w)]
    r_ref = next(it) if has_res else None
    o_ref = next(it)
    xn = x_ref[...]
    a = jnp.dot(xn, w_refs[0][...].astype(BF16), preferred_element_type=F32)
    if swiglu:
        b = jnp.dot(xn, w_refs[1][...].astype(BF16), preferred_element_type=F32)
        a = (a * jax.nn.sigmoid(a)) * b
    if has_res:
        a = a + r_ref[...]
    o_ref[...] = a.astype(o_ref.dtype)


def _linear(x, ws, *, residual=None, swiglu=False, out_dtype, tn, row0=0, rows=None,
            row_tiles=ROW_TILES, name):
    K = x.shape[1]
    M = x.shape[0] if rows is None else rows
    ws = [w if isinstance(w, tuple) else (w[None], 0) for w in ws]
    N = ws[0][0].shape[2]
    assert M % (row_tiles * 16) == 0 and N % tn == 0 and x.dtype == BF16
    tm = M // row_tiles
    assert row0 % tm == 0 and (residual is None or row0 == 0)
    off = row0 // tm
    has_res = residual is not None
    in_specs = [pl.BlockSpec((tm, K), lambda i, j: (i + off, 0), pipeline_mode=pl.Buffered(1))]
    args = [x]
    for w, layer in ws:
        in_specs.append(pl.BlockSpec((None, K, tn), lambda i, j, layer=layer: (layer, 0, j)))
        args.append(w)
    if has_res:
        in_specs.append(pl.BlockSpec((tm, tn), lambda i, j: (i, j)))
        args.append(residual)
    kern = functools.partial(_linear_kernel, n_w=len(ws), has_res=has_res, swiglu=swiglu)
    return pl.pallas_call(
        kern,
        grid=(row_tiles, N // tn),
        in_specs=in_specs,
        out_specs=pl.BlockSpec((tm, tn), lambda i, j: (i, j)),
        out_shape=jax.ShapeDtypeStruct((M, N), out_dtype),
        compiler_params=_cparams("parallel", "arbitrary"),
        name=name,
    )(*args)


MXU_COLS = 256


def _dilated_linear_kernel(x_ref, w_ref, *refs, dil, win_rows):
    it = iter(refs)
    o_ref = next(it)
    win_ref = next(it) if win_rows else None
    acc_ref = next(it) if dil > 1 else None
    tm, tn = x_ref.shape[0], w_ref.shape[1]
    x = x_ref[...]
    for c in range(tn // MXU_COLS):
        cs = slice(c * MXU_COLS, (c + 1) * MXU_COLS)
        acc = jnp.dot(x, w_ref[:, cs].astype(BF16), preferred_element_type=F32)
        if win_rows:
            win_ref[0, :, cs] = acc[tm - win_rows:, :]
        if dil == 1:
            o_ref[0, 0, :, cs] = acc.astype(o_ref.dtype)
        else:
            for s in range(c * MXU_COLS // LANES, (c + 1) * MXU_COLS // LANES):
                ls = slice(s * LANES, (s + 1) * LANES)
                acc_ref[s] = acc[:, ls.start - c * MXU_COLS:ls.stop - c * MXU_COLS]
                for r in range(dil):
                    o_ref[0, r, :, ls] = acc_ref[s, pl.ds(r, tm // dil, stride=dil), :].astype(o_ref.dtype)


def _dilated_linear(x, w, col0, *, batch, seq, dil, window=None, tm=2048, tn=512, name):
    K = x.shape[1]
    width = DIL_HEADS * (K // DIL_HEADS)
    assert seq % tm == 0 and tm % (dil * 16) == 0 and col0 % tn == 0 and width % tn == 0
    tpb = seq // tm
    cb0 = col0 // tn
    in_specs = [pl.BlockSpec((tm, K), lambda i, j: (i, 0)),
                pl.BlockSpec((K, tn), lambda i, j: (0, cb0 + j))]
    out_specs = [pl.BlockSpec((1, dil, tm // dil, tn), lambda i, j: (i // tpb, 0, i % tpb, j))]
    out_shape = [jax.ShapeDtypeStruct((batch, dil, seq // dil, width), BF16)]
    win_rows = 0
    if window is not None:
        win_rows = min(window, tm)
        assert window % win_rows == 0
        t0 = tpb - window // win_rows

        def win_map(i, j):
            t = i % tpb
            inside = t >= t0
            return (i // tpb, jnp.where(inside, t - t0, 0), jnp.where(inside, j, 0))

        out_specs.append(pl.BlockSpec((1, win_rows, tn), win_map))
        out_shape.append(jax.ShapeDtypeStruct((batch, window, width), F32))
    scratch = [pltpu.VMEM((tn // LANES, tm, LANES), F32)] if dil > 1 else []
    res = pl.pallas_call(
        functools.partial(_dilated_linear_kernel, dil=dil, win_rows=win_rows),
        grid=(batch * tpb, width // tn),
        in_specs=in_specs,
        out_specs=out_specs,
        out_shape=out_shape,
        scratch_shapes=scratch,
        compiler_params=_cparams("arbitrary", "arbitrary"),
        name=name,
    )(x, w)
    return res if window is not None else res[0]


def _retention_kernel(*refs, dk, dv, hps, chunk, nc):
    (q_ref, k_ref, v_ref, g_ref, cos_ref, sin_ref, dmat_ref, qdec_ref, kdec_ref,
     cdec_ref, gn_ref, s0_ref) = refs[:12]
    o_ref, s_ref = refs[-2:]

    s_ref[...] = s0_ref[...]
    half = dk // 2

    def one_chunk(c):
        rows = slice(None) if nc == 1 else pl.ds(pl.multiple_of(c * chunk, chunk), chunk)
        cos = cos_ref[rows, :]
        sin = sin_ref[rows, :]

        def rope(x):
            x1 = x[:, :half]
            x2 = x[:, half:]
            return jnp.concatenate([x1 * cos - x2 * sin, x1 * sin + x2 * cos], axis=-1)

        for j in range(hps):
            q = rope(q_ref[rows, j * dk:(j + 1) * dk].astype(F32))
            k = rope(k_ref[rows, j * dk:(j + 1) * dk].astype(F32)) * (dk ** -0.5)
            v = v_ref[rows, j * dv:(j + 1) * dv]
            s = s_ref[0, j]
            sc = lax.dot_general(q.astype(BF16), k.astype(BF16), (((1,), (1,)), ((), ())),
                                 preferred_element_type=F32) * dmat_ref[j]
            o = (jnp.dot(sc.astype(BF16), v, preferred_element_type=F32)
                 + jnp.dot((q * qdec_ref[j]).astype(BF16), s.astype(BF16), preferred_element_type=F32))
            kd = (k * kdec_ref[j]).astype(BF16)
            s_ref[0, j] = s * cdec_ref[j] + lax.dot_general(kd, v, (((0,), (0,)), ((), ())),
                                                           preferred_element_type=F32)
            mu = jnp.mean(o, axis=-1, keepdims=True)
            var = jnp.mean(jnp.square(o - mu), axis=-1, keepdims=True)
            on = (o - mu) * lax.rsqrt(var + EPS) * gn_ref[:, j * dv:(j + 1) * dv]
            gate = g_ref[rows, j * dv:(j + 1) * dv].astype(F32)
            o_ref[rows, j * dv:(j + 1) * dv] = ((gate * jax.nn.sigmoid(gate)) * on).astype(o_ref.dtype)

    if nc == 1:
        one_chunk(0)
    else:
        def body(c, carry):
            one_chunk(c)
            return carry

        lax.fori_loop(0, nc, body, 0)


def _retention(proj, s0, gn_g, *, row0, batch, seq, chunk, chunk_true, pos, out_init=None, name):
    H = RET_HEADS
    dk, dv = s0.shape[-2], s0.shape[-1]
    nc = seq // chunk
    assert row0 % chunk == 0
    lg = jnp.log1p(-jnp.exp2(-5.0 - jnp.arange(H, dtype=F32)))
    i = jnp.arange(chunk, dtype=F32)
    dist = i[:, None] - i[None, :]
    dmat = jnp.where(dist >= 0, jnp.exp(jnp.maximum(dist, 0.0)[None] * lg[:, None, None]), 0.0)
    qdec = jnp.exp((i + 1.0)[None, :, None] * lg[:, None, None])
    kdec = jnp.exp((chunk_true - 1.0 - i)[None, :, None] * lg[:, None, None])
    cdec = jnp.exp(chunk_true * lg).reshape(H, 1, 1)
    half = dk // 2
    inv = ROPE_BASE ** (-jnp.arange(half, dtype=F32) / half)
    ang = pos.astype(F32)[:, None] * inv[None, :]
    cos, sin = jnp.cos(ang), jnp.sin(ang)

    hps = H if seq * H * (2 * dk + 2 * dv) * 2 <= (1 << 20) else 1
    hg = H // hps
    kb, vb = hg, (2 * H * dk) // (hps * dv)
    gb = vb + hg
    assert row0 % seq == 0
    rb = row0 // seq
    in_specs = [
        pl.BlockSpec((seq, hps * dk), lambda b, h: (rb + b, h)),
        pl.BlockSpec((seq, hps * dk), lambda b, h: (rb + b, kb + h)),
        pl.BlockSpec((seq, hps * dv), lambda b, h: (rb + b, vb + h)),
        pl.BlockSpec((seq, hps * dv), lambda b, h: (rb + b, gb + h)),
        pl.BlockSpec((seq, half), lambda b, h: (0, 0)),
        pl.BlockSpec((seq, half), lambda b, h: (0, 0)),
        pl.BlockSpec((hps, chunk, chunk), lambda b, h: (h, 0, 0)),
        pl.BlockSpec((hps, chunk, 1), lambda b, h: (h, 0, 0)),
        pl.BlockSpec((hps, chunk, 1), lambda b, h: (h, 0, 0)),
        pl.BlockSpec((hps, 1, 1), lambda b, h: (h, 0, 0)),
        pl.BlockSpec((1, hps * dv), lambda b, h: (0, h)),
        pl.BlockSpec((1, hps, dk, dv), lambda b, h: (b, h, 0, 0)),
    ]
    args = [proj, proj, proj, proj, cos, sin, dmat, qdec, kdec, cdec, gn_g.reshape(1, H * dv), s0]
    aliases = {}
    if out_init is not None:
        in_specs.append(pl.BlockSpec(memory_space=pl.ANY))
        args.append(out_init)
        aliases = {len(args) - 1: 0}
    out_specs = [
        pl.BlockSpec((seq, hps * dv), lambda b, h: (rb + b, h)),
        pl.BlockSpec((1, hps, dk, dv), lambda b, h: (b, h, 0, 0)),
    ]
    out_shape = [jax.ShapeDtypeStruct((proj.shape[0], H * dv), BF16),
                 jax.ShapeDtypeStruct((batch, H, dk, dv), F32)]
    return pl.pallas_call(
        functools.partial(_retention_kernel, dk=dk, dv=dv, hps=hps, chunk=chunk, nc=nc),
        grid=(batch, hg),
        in_specs=in_specs,
        out_specs=out_specs,
        out_shape=out_shape,
        input_output_aliases=aliases,
        compiler_params=_cparams("parallel", "parallel"),
        name=name,
    )(*args)


def _dilated_prompt_kernel(*refs, heads, hd, jobs, ts, steps):
    q_ref, k_ref, v_ref = refs[:3]
    nj = len(jobs)
    caches, news = refs[3:3 + nj], refs[3 + nj:3 + 2 * nj]
    o_ref, l_ref = refs[3 + 2 * nj:5 + 2 * nj]
    shifted = refs[5 + 2 * nj:5 + 3 * nj]
    kp_ref, vp_ref, buf, in_sem, out_sem = refs[5 + 3 * nj:]

    n = pl.program_id(2)
    blk = q_ref.shape[2]
    step = (pl.program_id(0) * pl.num_programs(1) + pl.program_id(1)) * pl.num_programs(2) + n
    last = steps - 1

    def plan(h):
        out, off = [], 0
        for j, (halves) in enumerate(jobs):
            r0, nr, tail = halves[h]
            parts = COPY_STREAMS if nr >= COPY_STREAMS * 64 else 1
            for p in range(parts):
                lo, hi = nr * p // parts, nr * (p + 1) // parts
                if hi > lo:
                    last = tail and p == parts - 1
                    out.append((j, r0 + lo, hi - lo, off, last))
                    off += hi - lo + (ts if last else 0)
        return out

    def in_copies(h, bc):
        return [pltpu.make_async_copy(caches[j].at[bc, pl.ds(ts + r0, nr)],
                                      buf.at[h, pl.ds(off, nr)], in_sem.at[h])
                for (j, r0, nr, off, tail) in plan(h)]

    def out_copies(h, bc):
        return [pltpu.make_async_copy(buf.at[h, pl.ds(off, nr + (ts if tail else 0))],
                                      shifted[j].at[bc, pl.ds(r0, nr + (ts if tail else 0))],
                                      out_sem.at[h])
                for (j, r0, nr, off, tail) in plan(h)]

    @pl.when(step == 0)
    def _():
        for cp in in_copies(0, 0):
            cp.start()

    for h in range(2):
        @pl.when(step % 2 == h)
        def _(h=h):
            bc = step // 2
            for cp in in_copies(h, bc):
                cp.wait()
            for (j, r0, nr, off, tail) in plan(h):
                if tail:
                    buf[h, pl.ds(off + nr, ts)] = news[j][bc]
            for cp in out_copies(h, bc):
                cp.start()

            @pl.when(step > 0)
            def _():
                for cp in out_copies(1 - h, (step - 1) // 2):
                    cp.wait()

            @pl.when(step < last)
            def _():
                for cp in in_copies(1 - h, (step + 1) // 2):
                    cp.start()

    @pl.when(n == 0)
    def _():
        kp_ref[...] = jnp.zeros_like(kp_ref)
        vp_ref[...] = jnp.zeros_like(vp_ref)

    row = lax.broadcasted_iota(jnp.int32, (blk, blk), 0)
    col = lax.broadcasted_iota(jnp.int32, (blk, blk), 1)
    mask_c = col <= row
    mask_p = (col - row) >= jnp.where(n > 0, 0, blk)
    lane = lax.broadcasted_iota(jnp.int32, (blk, LANES), 1)
    scale = hd ** -0.5
    nt = (((1,), (1,)), ((), ()))
    hpt = heads // l_ref.shape[0]

    def qk(cs, key_ref, mask):
        s = lax.dot_general(q_ref[0, 0, :, cs], key_ref[:, cs], nt, preferred_element_type=F32)
        return jnp.where(mask, s * scale, NEG)

    kc_ref, vc_ref = k_ref.at[0, 0], v_ref.at[0, 0]
    for g0 in range(0, heads, hpt):
        cols = [slice(h * hd, (h + 1) * hd) for h in range(g0, g0 + hpt)]
        s_c = [qk(cs, kc_ref, mask_c) for cs in cols]
        s_p = [qk(cs, kp_ref, mask_p) for cs in cols]
        m = [jnp.maximum(jnp.max(a, axis=-1, keepdims=True), jnp.max(b, axis=-1, keepdims=True))
             for a, b in zip(s_c, s_p)]
        p_c = [jnp.exp(a - mm) for a, mm in zip(s_c, m)]
        p_p = [jnp.exp(b - mm) for b, mm in zip(s_p, m)]
        den = [jnp.sum(a, axis=-1, keepdims=True) + jnp.sum(b, axis=-1, keepdims=True)
               for a, b in zip(p_c, p_p)]
        inv = [1.0 / d for d in den]
        lse_tile = jnp.zeros((blk, LANES), F32)
        for i, cs in enumerate(cols):
            o_ref[0, 0, :, cs] = (
                jnp.dot((p_c[i] * inv[i]).astype(BF16), vc_ref[:, cs], preferred_element_type=F32)
                + jnp.dot((p_p[i] * inv[i]).astype(BF16), vp_ref[:, cs], preferred_element_type=F32))
            lse_tile = jnp.where(lane == i, m[i] + jnp.log(den[i]), lse_tile)
        l_ref[g0 // hpt, 0, 0] = lse_tile

    kp_ref[...] = k_ref[0, 0]
    vp_ref[...] = v_ref[0, 0]

    @pl.when(step == last)
    def _():
        for cp in out_copies(last % 2, last // 2):
            cp.wait()


LSE_TILES = 4
COPY_STREAMS = 4


def _dilated_prompt(q, k, v, gi, *, shift_jobs, name):
    win, dil = DIL_PAIRS[gi]
    blk = win // dil
    H = DIL_HEADS
    batch, _, sd, width = q.shape
    hd = width // H
    assert q.shape[1] == dil and sd % blk == 0
    nb = sd // blk
    slab = pl.BlockSpec((1, 1, blk, width), lambda b, r, n: (b, r, n, 0))
    in_specs = [slab, slab, slab]
    args = [q, k, v]
    out_specs = [slab, pl.BlockSpec((LSE_TILES, 1, 1, blk, LANES), lambda b, r, n: (0, b, r, n, 0))]
    out_shape = [jax.ShapeDtypeStruct(q.shape, F32),
                 jax.ShapeDtypeStruct((LSE_TILES, batch, dil, sd, LANES), F32)]

    steps = batch * dil * nb
    caches = [c for c, _ in shift_jobs]
    news = [a for _, a in shift_jobs]
    ts = news[0].shape[1]
    Bc = caches[0].shape[0]
    assert steps == 2 * Bc
    jobs, slot_rows = [], [0, 0]
    for c in caches:
        keep = c.shape[1] - ts
        first = keep // 2 if keep * c.shape[2] * c.shape[3] * 4 > (2 << 20) else 0
        halves = ((0, first, False), (first, keep - first, True))
        jobs.append(halves)
        for h in range(2):
            slot_rows[h] += halves[h][1] + (ts if halves[h][2] else 0)
    any_spec = pl.BlockSpec(memory_space=pl.ANY)
    in_specs += [any_spec] * len(caches) + [pl.BlockSpec(memory_space=pltpu.VMEM)] * len(news)
    args += caches + news
    out_specs = out_specs + [any_spec] * len(caches)
    out_shape = out_shape + [jax.ShapeDtypeStruct(c.shape, c.dtype) for c in caches]

    res = pl.pallas_call(
        functools.partial(_dilated_prompt_kernel, heads=H, hd=hd, jobs=tuple(jobs), ts=ts, steps=steps),
        grid=(batch, dil, nb),
        in_specs=in_specs,
        out_specs=out_specs,
        out_shape=out_shape,
        scratch_shapes=[pltpu.VMEM((blk, width), BF16), pltpu.VMEM((blk, width), BF16),
                        pltpu.VMEM((2, max(slot_rows)) + caches[0].shape[2:], caches[0].dtype),
                        pltpu.SemaphoreType.DMA((2,)), pltpu.SemaphoreType.DMA((2,))],
        compiler_params=_cparams("arbitrary", "arbitrary", "arbitrary"),
        name=name,
    )(*args)
    return res[0], res[1], list(res[2:])


def _merge_kernel(o0_ref, o1_ref, o2_ref, l0_ref, l1_ref, l2_ref, init_ref, out_ref,
                  s0_ref, s1_ref, nat_ref, *, hd):
    del init_ref
    d1, d2 = o1_ref.shape[1], o2_ref.shape[1]
    rows = o2_ref.shape[2]
    rep = d2 // d1
    nh = o2_ref.shape[3] // hd
    for hh in range(nh):
        cs = slice(hh * hd, (hh + 1) * hd)
        s0_ref[hh] = o0_ref[:, cs]
        s1_ref[hh] = o1_ref[0, :, :, cs]
    for r in range(d2):
        l2 = l2_ref[0, 0, r]
        l1 = l1_ref[0, 0, r % d1, pl.ds(r // d1, rows, stride=rep), :]
        l0 = l0_ref[0, pl.ds(r, rows, stride=d2), :]
        mx = jnp.maximum(jnp.maximum(l0, l1), l2)
        e0 = jnp.exp(l0 - mx)
        e1 = jnp.exp(l1 - mx)
        e2 = jnp.exp(l2 - mx)
        tot = 1.0 / (e0 + e1 + e2)
        w0, w1, w2 = e0 * tot, e1 * tot, e2 * tot
        for hh in range(nh):
            ls = slice(hh, hh + 1)
            o2 = o2_ref[0, r, :, hh * hd:(hh + 1) * hd]
            o1 = s1_ref[hh, r % d1, pl.ds(r // d1, rows, stride=rep), :]
            o0 = s0_ref[hh, pl.ds(r, rows, stride=d2), :]
            nat_ref[hh, pl.ds(r, rows, stride=d2), :] = w0[:, ls] * o0 + w1[:, ls] * o1 + w2[:, ls] * o2
    for hh in range(nh):
        out_ref[:, hh * hd:(hh + 1) * hd] = nat_ref[hh].astype(out_ref.dtype)


def _merge(outs, lses, init, *, name):
    (_, d0), (_, d1), (_, d2) = DIL_PAIRS
    batch, _, sd2, width = outs[2].shape
    seq = sd2 * d2
    rows = DIL_PAIRS[2][0] // d2
    span = rows * d2
    assert d0 == 1 and d2 % d1 == 0 and seq % span == 0
    nsp = seq // span
    cw = width // LSE_TILES
    hd = width // DIL_HEADS
    o0 = outs[0].reshape(batch * seq, width)
    l0 = lses[0].reshape(LSE_TILES, batch * seq, LANES)
    in_specs = [
        pl.BlockSpec((span, cw), lambda t, c: (t, c)),
        pl.BlockSpec((1, d1, span // d1, cw), lambda t, c: (t // nsp, 0, t % nsp, c)),
        pl.BlockSpec((1, d2, rows, cw), lambda t, c: (t // nsp, 0, t % nsp, c)),
        pl.BlockSpec((1, span, LANES), lambda t, c: (c, t, 0)),
        pl.BlockSpec((1, 1, d1, span // d1, LANES), lambda t, c: (c, t // nsp, 0, t % nsp, 0)),
        pl.BlockSpec((1, 1, d2, rows, LANES), lambda t, c: (c, t // nsp, 0, t % nsp, 0)),
        pl.BlockSpec(memory_space=pl.ANY),
    ]
    return pl.pallas_call(
        functools.partial(_merge_kernel, hd=hd),
        grid=(batch * nsp, LSE_TILES),
        in_specs=in_specs,
        out_specs=pl.BlockSpec((span, cw), lambda t, c: (t, c)),
        out_shape=jax.ShapeDtypeStruct(init.shape, init.dtype),
        scratch_shapes=[pltpu.VMEM((cw // hd, span, hd), F32),
                        pltpu.VMEM((cw // hd, d1, span // d1, hd), F32),
                        pltpu.VMEM((cw // hd, span, hd), F32)],
        input_output_aliases={6: 0},
        compiler_params=_cparams("parallel", "parallel"),
        name=name,
    )(o0, outs[1], outs[2], l0, lses[1], lses[2], init)


def _dilated_sample_kernel(q_ref, kn_ref, vn_ref, k0_ref, v0_ref, k1_ref, v1_ref, k2_ref, v2_ref,
                           o_ref, *, ts, hd):
    caches = ((k0_ref, v0_ref), (k1_ref, v1_ref), (k2_ref, v2_ref))
    scale = hd ** -0.5
    for t in range(ts):
        outs, lses = [], []
        for gi, (win, dil) in enumerate(DIL_PAIRS):
            kc_ref, vc_ref = caches[gi]
            q = q_ref[0, t, gi]
            r = t % dil if dil > 1 else 0
            kc = kc_ref[0, :, r]
            vc = vc_ref[0, :, r]
            s_c = jnp.sum(kc * q[None], axis=-1, keepdims=True) * scale
            if dil == 1:
                cidx = lax.broadcasted_iota(jnp.int32, s_c.shape, 0)
                s_c = jnp.where(cidx >= t, s_c, NEG)
                new_rows = list(range(t + 1))
            else:
                new_rows = [t]
            s_n = [jnp.sum(kn_ref[0, u, gi] * q, axis=-1, keepdims=True) * scale for u in new_rows]
            m = jnp.max(s_c, axis=0)
            for sn in s_n:
                m = jnp.maximum(m, sn)
            p_c = jnp.exp(s_c - m[None])
            p_n = [jnp.exp(sn - m) for sn in s_n]
            den = jnp.sum(p_c, axis=0)
            for pn in p_n:
                den = den + pn
            inv = 1.0 / den
            o = jnp.sum((p_c * inv[None]) * vc, axis=0)
            for u, pn in zip(new_rows, p_n):
                o = o + (pn * inv) * vn_ref[0, u, gi]
            outs.append(o)
            lses.append(m + jnp.log(den))
        mx = jnp.maximum(jnp.maximum(lses[0], lses[1]), lses[2])
        es = [jnp.exp(l - mx) for l in lses]
        tot = 1.0 / (es[0] + es[1] + es[2])
        o_ref[0, t] = (es[0] * tot) * outs[0] + (es[1] * tot) * outs[1] + (es[2] * tot) * outs[2]


def _dilated_sample(q, knew, vnew, caches_k, caches_v, *, name):
    B, ts, G, H, hd = q.shape
    small = pl.BlockSpec((1, ts, G, H, hd), lambda b: (b, 0, 0, 0, 0))
    in_specs = [small, small, small]
    args = [q, knew, vnew]
    for gi, (win, dil) in enumerate(DIL_PAIRS):
        L = caches_k[gi].shape[1]
        assert L == win and L % dil == 0 and (dil == 1 or ts <= dil)
        nres = min(dil, ts)
        spec = pl.BlockSpec((1, L // dil, nres, H, hd), lambda b: (b, 0, 0, 0, 0))
        for c in (caches_k[gi], caches_v[gi]):
            in_specs.append(spec)
            args.append(c.reshape(B, L // dil, dil, H, hd))
    return pl.pallas_call(
        functools.partial(_dilated_sample_kernel, ts=ts, hd=hd),
        grid=(B,),
        in_specs=in_specs,
        out_specs=pl.BlockSpec((1, ts, H, hd), lambda b: (b, 0, 0, 0)),
        out_shape=jax.ShapeDtypeStruct((B, ts, H, hd), F32),
        compiler_params=_cparams("parallel"),
        name=name,
    )(*args)


def kernel(x_prompt, x_sample, state_ret, cache_k_w128, cache_v_w128, cache_k_w512, cache_v_w512,
           cache_k_w2048, cache_v_w2048, norm_mix, norm_ffn, ret_w_in, ret_gn, ret_w_out,
           kv_norm, w_kv, dil_w_q, dil_w_o, ffn_w1, ffn_w3, ffn_w2, norm_final):
    Bp, Tp, D = x_prompt.shape
    Bs, Ts, _ = x_sample.shape
    H, hd = DIL_HEADS, D // DIL_HEADS
    caches_k = [cache_k_w128, cache_k_w512, cache_k_w2048]
    caches_v = [cache_v_w128, cache_v_w512, cache_v_w2048]
    MP, MS = Bp * Tp, Bs * SAMPLE_PAD
    MT = MP + MS
    NT = MT // 16

    h = jnp.concatenate([x_prompt.reshape(MP, D),
                         jnp.pad(x_sample, ((0, 0), (0, SAMPLE_PAD - Ts), (0, 0))).reshape(MS, D)])

    def ffn(h, layer):
        xn, = _norm(h, [norm_ffn[layer]], out_dtype=BF16, tm=NT, name=f"ffn_norm_{layer}")
        act = _linear(xn, [(ffn_w1, layer), (ffn_w3, layer)], swiglu=True, out_dtype=BF16, tn=512,
                      name=f"ffn_up_{layer}")
        return _linear(act, [(ffn_w2, layer)], residual=h, out_dtype=F32, tn=256, name=f"ffn_down_{layer}")

    chunk_p = RET_CHUNK if Tp % RET_CHUNK == 0 else Tp
    xn, = _norm(h, [norm_mix[0]], out_dtype=BF16, tm=NT, name="mix_norm_0")
    proj = _linear(xn, [ret_w_in[0]], out_dtype=BF16, tn=512, name="ret_in")
    s0_p = jnp.zeros((Bp,) + state_ret.shape[2:], F32)
    gated = jnp.zeros((MT, RET_HEADS * state_ret.shape[-1]), BF16)
    gated, sp = _retention(proj, s0_p, ret_gn[0], row0=0, batch=Bp, seq=Tp, chunk=chunk_p,
                           chunk_true=chunk_p, pos=jnp.arange(Tp), out_init=gated, name="retention_p")
    gated, ss = _retention(proj, state_ret[0], ret_gn[0], row0=MP, batch=Bs, seq=SAMPLE_PAD,
                           chunk=SAMPLE_PAD, chunk_true=Ts, pos=PAST_LEN + jnp.arange(SAMPLE_PAD),
                           out_init=gated, name="retention_s")
    h = _linear(gated, [ret_w_out[0]], residual=h, out_dtype=F32, tn=256, name="ret_out")
    h = ffn(h, 0)

    xkv, xq = _norm(h, [kv_norm, norm_mix[1]], out_dtype=BF16, tm=NT, name="kv_q_norm")
    qd, kd, vd, new_kp, new_vp = [], [], [], [], []
    for gi, (win, dil) in enumerate(DIL_PAIRS):
        assert Tp % win == 0
        qd.append(_dilated_linear(xq, dil_w_q[0], gi * D, batch=Bp, seq=Tp, dil=dil, tn=1024,
                                  name=f"dil_q_{gi}"))
        kg, kw = _dilated_linear(xkv, w_kv, 2 * gi * D, batch=Bp, seq=Tp, dil=dil, window=win,
                                 name=f"dil_k_{gi}")
        vg, vw = _dilated_linear(xkv, w_kv, (2 * gi + 1) * D, batch=Bp, seq=Tp, dil=dil, window=win,
                                 name=f"dil_v_{gi}")
        kd.append(kg)
        vd.append(vg)
        new_kp.append(kw.reshape(Bp, win, H, hd))
        new_vp.append(vw.reshape(Bp, win, H, hd))
    kv_s = _linear(xkv, [w_kv], out_dtype=F32, tn=1024, row0=MP, rows=MS, row_tiles=1, name="kv_s")
    q_s = _linear(xq, [dil_w_q[0]], out_dtype=F32, tn=1024, row0=MP, rows=MS, row_tiles=1, name="q_s")
    kv_s6 = kv_s.reshape(Bs, SAMPLE_PAD, N_GROUPS, 2, H, hd)[:, :Ts]
    knew, vnew = kv_s6[:, :, :, 0], kv_s6[:, :, :, 1]
    q_s5 = q_s.reshape(Bs, SAMPLE_PAD, N_GROUPS, H, hd)[:, :Ts]

    job = lambda c, a, i: (c[i], a[:, :, i])
    o0, l0, (nk2,) = _dilated_prompt(qd[0], kd[0], vd[0], 0, shift_jobs=[job(caches_k, knew, 2)],
                                     name="dilated_p0")
    o1, l1, (nk0, nv0, nk1, nv1) = _dilated_prompt(
        qd[1], kd[1], vd[1], 1, name="dilated_p1",
        shift_jobs=[job(caches_k, knew, 0), job(caches_v, vnew, 0),
                    job(caches_k, knew, 1), job(caches_v, vnew, 1)])
    o2, l2, (nv2,) = _dilated_prompt(qd[2], kd[2], vd[2], 2, shift_jobs=[job(caches_v, vnew, 2)],
                                     name="dilated_p2")
    a_s = _dilated_sample(q_s5, knew, vnew, caches_k, caches_v, name="dilated_s")
    a_s = jnp.pad(a_s.reshape(Bs, Ts, D), ((0, 0), (0, SAMPLE_PAD - Ts), (0, 0))).reshape(MS, D)
    a = jnp.concatenate([jnp.zeros((MP, D), BF16), a_s.astype(BF16)])
    a = _merge([o0, o1, o2], [l0, l1, l2], a, name="dilated_merge")
    h = _linear(a, [dil_w_o[0]], residual=h, out_dtype=F32, tn=512, name="dil_o")
    h = ffn(h, 1)

    y_p, = _norm(h, [norm_final], out_dtype=F32, tm=MS, rows=MP, name="final_norm_p")
    y_s, = _norm(h, [norm_final], out_dtype=F32, tm=MS, row0=MP, rows=MS, name="final_norm_s")
    return (y_p.reshape(Bp, Tp, D), y_s.reshape(Bs, SAMPLE_PAD, D)[:, :Ts], sp[None], ss[None],
            new_kp[0], new_vp[0], new_kp[1], new_vp[1], new_kp[2], new_vp[2],
            nk0, nv0, nk1, nv1, nk2, nv2)
```

```python
import functools

import jax
import jax.numpy as jnp
from jax import lax
from jax.experimental import pallas as pl
from jax.experimental.pallas import tpu as pltpu

F32 = jnp.float32
BF16 = jnp.bfloat16

RET_HEADS = 8
RET_CHUNK = 128
ROPE_BASE = 10000.0
DIL_PAIRS = ((128, 1), (512, 4), (2048, 16))
N_GROUPS = len(DIL_PAIRS)
DIL_HEADS = 16
PAST_LEN = 8192
EPS = 1e-6
NEG = -1e30

LANES = 128
SAMPLE_PAD = 16
VMEM_LIMIT = 61 * 1024 * 1024
ROW_TILES = 4


def _cparams(*sem):
    return pltpu.CompilerParams(dimension_semantics=sem, vmem_limit_bytes=VMEM_LIMIT)


def _norm_kernel(x_ref, g_ref, *o_refs):
    xf = x_ref[...]
    ms = jnp.mean(xf * xf, axis=-1, keepdims=True)
    xs = xf * lax.rsqrt(ms + EPS)
    for k, o_ref in enumerate(o_refs):
        o_ref[...] = (xs * g_ref[k:k + 1, :]).astype(o_ref.dtype)


def _norm(x, gains, *, out_dtype, tm, row0=0, rows=None, name):
    M, K = x.shape
    rows = M if rows is None else rows
    assert rows % tm == 0 and row0 % tm == 0
    g = jnp.stack(gains).astype(F32)
    off = row0 // tm
    outs = pl.pallas_call(
        _norm_kernel,
        grid=(rows // tm,),
        in_specs=[pl.BlockSpec((tm, K), lambda i: (i + off, 0)),
                  pl.BlockSpec((len(gains), K), lambda i: (0, 0))],
        out_specs=[pl.BlockSpec((tm, K), lambda i: (i, 0))] * len(gains),
        out_shape=[jax.ShapeDtypeStruct((rows, K), out_dtype)] * len(gains),
        compiler_params=_cparams("parallel"),
        name=name,
    )(x, g)
    return outs


def _linear_kernel(*refs, n_w, has_res, swiglu):
    it = iter(refs)
    x_ref = next(it)
    w_refs = [next(it) for _ in range(n_w)]
    r_ref = next(it) if has_res else None
    o_ref = next(it)
    xn = x_ref[...]
    a = jnp.dot(xn, w_refs[0][...].astype(BF16), preferred_element_type=F32)
    if swiglu:
        b = jnp.dot(xn, w_refs[1][...].astype(BF16), preferred_element_type=F32)
        a = (a * jax.nn.sigmoid(a)) * b
    if has_res:
        a = a + r_ref[...]
    o_ref[...] = a.astype(o_ref.dtype)


def _linear(x, ws, *, residual=None, swiglu=False, out_dtype, tn, row0=0, rows=None,
            row_tiles=ROW_TILES, name):
    K = x.shape[1]
    M = x.shape[0] if rows is None else rows
    ws = [w if isinstance(w, tuple) else (w[None], 0) for w in ws]
    N = ws[0][0].shape[2]
    assert M % (row_tiles * 16) == 0 and N % tn == 0 and x.dtype == BF16
    tm = M // row_tiles
    assert row0 % tm == 0 and (residual is None or row0 == 0)
    off = row0 // tm
    has_res = residual is not None
    in_specs = [pl.BlockSpec((tm, K), lambda i, j: (i + off, 0), pipeline_mode=pl.Buffered(1))]
    args = [x]
    for w, layer in ws:
        in_specs.append(pl.BlockSpec((None, K, tn), lambda i, j, layer=layer: (layer, 0, j)))
        args.append(w)
    if has_res:
        in_specs.append(pl.BlockSpec((tm, tn), lambda i, j: (i, j)))
        args.append(residual)
    kern = functools.partial(_linear_kernel, n_w=len(ws), has_res=has_res, swiglu=swiglu)
    return pl.pallas_call(
        kern,
        grid=(row_tiles, N // tn),
        in_specs=in_specs,
        out_specs=pl.BlockSpec((tm, tn), lambda i, j: (i, j)),
        out_shape=jax.ShapeDtypeStruct((M, N), out_dtype),
        compiler_params=_cparams("parallel", "arbitrary"),
        name=name,
    )(*args)


MXU_COLS = 256


def _dilated_linear_kernel(x_ref, w_ref, *refs, dil, win_rows):
    it = iter(refs)
    o_ref = next(it)
    win_ref = next(it) if win_rows else None
    acc_ref = next(it) if dil > 1 else None
    tm, tn = x_ref.shape[0], w_ref.shape[1]
    x = x_ref[...]
    for c in range(tn // MXU_COLS):
        cs = slice(c * MXU_COLS, (c + 1) * MXU_COLS)
        acc = jnp.dot(x, w_ref[:, cs].astype(BF16), preferred_element_type=F32)
        if win_rows:
            win_ref[0, :, cs] = acc[tm - win_rows:, :]
        if dil == 1:
            o_ref[0, 0, :, cs] = acc.astype(o_ref.dtype)
        else:
            for s in range(c * MXU_COLS // LANES, (c + 1) * MXU_COLS // LANES):
                ls = slice(s * LANES, (s + 1) * LANES)
                acc_ref[s] = acc[:, ls.start - c * MXU_COLS:ls.stop - c * MXU_COLS]
                for r in range(dil):
                    o_ref[0, r, :, ls] = acc_ref[s, pl.ds(r, tm // dil, stride=dil), :].astype(o_ref.dtype)


def _dilated_linear(x, w, col0, *, batch, seq, dil, window=None, tm=2048, tn=512, name):
    K = x.shape[1]
    width = DIL_HEADS * (K // DIL_HEADS)
    assert seq % tm == 0 and tm % (dil * 16) == 0 and col0 % tn == 0 and width % tn == 0
    tpb = seq // tm
    cb0 = col0 // tn
    in_specs = [pl.BlockSpec((tm, K), lambda i, j: (i, 0)),
                pl.BlockSpec((K, tn), lambda i, j: (0, cb0 + j))]
    out_specs = [pl.BlockSpec((1, dil, tm // dil, tn), lambda i, j: (i // tpb, 0, i % tpb, j))]
    out_shape = [jax.ShapeDtypeStruct((batch, dil, seq // dil, width), BF16)]
    win_rows = 0
    if window is not None:
        win_rows = min(window, tm)
        assert window % win_rows == 0
        t0 = tpb - window // win_rows

        def win_map(i, j):
            t = i % tpb
            inside = t >= t0
            return (i // tpb, jnp.where(inside, t - t0, 0), jnp.where(inside, j, 0))

        out_specs.append(pl.BlockSpec((1, win_rows, tn), win_map))
        out_shape.append(jax.ShapeDtypeStruct((batch, window, width), F32))
    scratch = [pltpu.VMEM((tn // LANES, tm, LANES), F32)] if dil > 1 else []
    res = pl.pallas_call(
        functools.partial(_dilated_linear_kernel, dil=dil, win_rows=win_rows),
        grid=(batch * tpb, width // tn),
        in_specs=in_specs,
        out_specs=out_specs,
        out_shape=out_shape,
        scratch_shapes=scratch,
        compiler_params=_cparams("arbitrary", "arbitrary"),
        name=name,
    )(x, w)
    return res if window is not None else res[0]


def _retention_kernel(*refs, dk, dv, hps, chunk, nc):
    (q_ref, k_ref, v_ref, g_ref, cos_ref, sin_ref, dmat_ref, qdec_ref, kdec_ref,
     cdec_ref, gn_ref, s0_ref) = refs[:12]
    o_ref, s_ref = refs[-2:]

    s_ref[...] = s0_ref[...]
    half = dk // 2

    def one_chunk(c):
        rows = slice(None) if nc == 1 else pl.ds(pl.multiple_of(c * chunk, chunk), chunk)
        cos = cos_ref[rows, :]
        sin = sin_ref[rows, :]

        def rope(x):
            x1 = x[:, :half]
            x2 = x[:, half:]
            return jnp.concatenate([x1 * cos - x2 * sin, x1 * sin + x2 * cos], axis=-1)

        for j in range(hps):
            q = rope(q_ref[rows, j * dk:(j + 1) * dk].astype(F32))
            k = rope(k_ref[rows, j * dk:(j + 1) * dk].astype(F32)) * (dk ** -0.5)
            v = v_ref[rows, j * dv:(j + 1) * dv]
            s = s_ref[0, j]
            sc = lax.dot_general(q.astype(BF16), k.astype(BF16), (((1,), (1,)), ((), ())),
                                 preferred_element_type=F32) * dmat_ref[j]
            o = (jnp.dot(sc.astype(BF16), v, preferred_element_type=F32)
                 + jnp.dot((q * qdec_ref[j]).astype(BF16), s.astype(BF16), preferred_element_type=F32))
            kd = (k * kdec_ref[j]).astype(BF16)
            s_ref[0, j] = s * cdec_ref[j] + lax.dot_general(kd, v, (((0,), (0,)), ((), ())),
                                                           preferred_element_type=F32)
            mu = jnp.mean(o, axis=-1, keepdims=True)
            var = jnp.mean(jnp.square(o - mu), axis=-1, keepdims=True)
            on = (o - mu) * lax.rsqrt(var + EPS) * gn_ref[:, j * dv:(j + 1) * dv]
            gate = g_ref[rows, j * dv:(j + 1) * dv].astype(F32)
            o_ref[rows, j * dv:(j + 1) * dv] = ((gate * jax.nn.sigmoid(gate)) * on).astype(o_ref.dtype)

    if nc == 1:
        one_chunk(0)
    else:
        def body(c, carry):
            one_chunk(c)
            return carry

        lax.fori_loop(0, nc, body, 0, unroll=4 if nc % 4 == 0 else 1)


def _retention(proj, s0, gn_g, *, row0, batch, seq, chunk, chunk_true, pos, out_init=None, name):
    H = RET_HEADS
    dk, dv = s0.shape[-2], s0.shape[-1]
    nc = seq // chunk
    assert row0 % chunk == 0
    lg = jnp.log1p(-jnp.exp2(-5.0 - jnp.arange(H, dtype=F32)))
    i = jnp.arange(chunk, dtype=F32)
    dist = i[:, None] - i[None, :]
    dmat = jnp.where(dist >= 0, jnp.exp(jnp.maximum(dist, 0.0)[None] * lg[:, None, None]), 0.0)
    qdec = jnp.exp((i + 1.0)[None, :, None] * lg[:, None, None])
    kdec = jnp.exp((chunk_true - 1.0 - i)[None, :, None] * lg[:, None, None])
    cdec = jnp.exp(chunk_true * lg).reshape(H, 1, 1)
    half = dk // 2
    inv = ROPE_BASE ** (-jnp.arange(half, dtype=F32) / half)
    ang = pos.astype(F32)[:, None] * inv[None, :]
    cos, sin = jnp.cos(ang), jnp.sin(ang)

    hps = H if seq * H * (2 * dk + 2 * dv) * 2 <= (1 << 20) else 1
    hg = H // hps
    kb, vb = hg, (2 * H * dk) // (hps * dv)
    gb = vb + hg
    assert row0 % seq == 0
    rb = row0 // seq
    in_specs = [
        pl.BlockSpec((seq, hps * dk), lambda b, h: (rb + b, h)),
        pl.BlockSpec((seq, hps * dk), lambda b, h: (rb + b, kb + h)),
        pl.BlockSpec((seq, hps * dv), lambda b, h: (rb + b, vb + h)),
        pl.BlockSpec((seq, hps * dv), lambda b, h: (rb + b, gb + h)),
        pl.BlockSpec((seq, half), lambda b, h: (0, 0)),
        pl.BlockSpec((seq, half), lambda b, h: (0, 0)),
        pl.BlockSpec((hps, chunk, chunk), lambda b, h: (h, 0, 0)),
        pl.BlockSpec((hps, chunk, 1), lambda b, h: (h, 0, 0)),
        pl.BlockSpec((hps, chunk, 1), lambda b, h: (h, 0, 0)),
        pl.BlockSpec((hps, 1, 1), lambda b, h: (h, 0, 0)),
        pl.BlockSpec((1, hps * dv), lambda b, h: (0, h)),
        pl.BlockSpec((1, hps, dk, dv), lambda b, h: (b, h, 0, 0)),
    ]
    args = [proj, proj, proj, proj, cos, sin, dmat, qdec, kdec, cdec, gn_g.reshape(1, H * dv), s0]
    aliases = {}
    if out_init is not None:
        in_specs.append(pl.BlockSpec(memory_space=pl.ANY))
        args.append(out_init)
        aliases = {len(args) - 1: 0}
    out_specs = [
        pl.BlockSpec((seq, hps * dv), lambda b, h: (rb + b, h)),
        pl.BlockSpec((1, hps, dk, dv), lambda b, h: (b, h, 0, 0)),
    ]
    out_shape = [jax.ShapeDtypeStruct((proj.shape[0], H * dv), BF16),
                 jax.ShapeDtypeStruct((batch, H, dk, dv), F32)]
    return pl.pallas_call(
        functools.partial(_retention_kernel, dk=dk, dv=dv, hps=hps, chunk=chunk, nc=nc),
        grid=(batch, hg),
        in_specs=in_specs,
        out_specs=out_specs,
        out_shape=out_shape,
        input_output_aliases=aliases,
        compiler_params=_cparams("parallel", "parallel"),
        name=name,
    )(*args)


def _dilated_prompt_kernel(*refs, heads, hd, jobs, ts, steps):
    q_ref, k_ref, v_ref = refs[:3]
    nj = len(jobs)
    caches, news = refs[3:3 + nj], refs[3 + nj:3 + 2 * nj]
    o_ref, l_ref = refs[3 + 2 * nj:5 + 2 * nj]
    shifted = refs[5 + 2 * nj:5 + 3 * nj]
    kp_ref, vp_ref, buf, in_sem, out_sem = refs[5 + 3 * nj:]

    n = pl.program_id(2)
    blk = q_ref.shape[2]
    step = (pl.program_id(0) * pl.num_programs(1) + pl.program_id(1)) * pl.num_programs(2) + n
    last = steps - 1

    def plan(h):
        out, off = [], 0
        for j, (halves) in enumerate(jobs):
            r0, nr, tail = halves[h]
            parts = COPY_STREAMS if nr >= COPY_STREAMS * 64 else 1
            for p in range(parts):
                lo, hi = nr * p // parts, nr * (p + 1) // parts
                if hi > lo:
                    last = tail and p == parts - 1
                    out.append((j, r0 + lo, hi - lo, off, last))
                    off += hi - lo + (ts if last else 0)
        return out

    def in_copies(h, bc):
        return [pltpu.make_async_copy(caches[j].at[bc, pl.ds(ts + r0, nr)],
                                      buf.at[h, pl.ds(off, nr)], in_sem.at[h])
                for (j, r0, nr, off, tail) in plan(h)]

    def out_copies(h, bc):
        return [pltpu.make_async_copy(buf.at[h, pl.ds(off, nr + (ts if tail else 0))],
                                      shifted[j].at[bc, pl.ds(r0, nr + (ts if tail else 0))],
                                      out_sem.at[h])
                for (j, r0, nr, off, tail) in plan(h)]

    @pl.when(step == 0)
    def _():
        for cp in in_copies(0, 0):
            cp.start()

    for h in range(2):
        @pl.when(step % 2 == h)
        def _(h=h):
            bc = step // 2
            for cp in in_copies(h, bc):
                cp.wait()
            for (j, r0, nr, off, tail) in plan(h):
                if tail:
                    buf[h, pl.ds(off + nr, ts)] = news[j][bc]
            for cp in out_copies(h, bc):
                cp.start()

            @pl.when(step > 0)
            def _():
                for cp in out_copies(1 - h, (step - 1) // 2):
                    cp.wait()

            @pl.when(step < last)
            def _():
                for cp in in_copies(1 - h, (step + 1) // 2):
                    cp.start()

    @pl.when(n == 0)
    def _():
        kp_ref[...] = jnp.zeros_like(kp_ref)
        vp_ref[...] = jnp.zeros_like(vp_ref)

    row = lax.broadcasted_iota(jnp.int32, (blk, blk), 0)
    col = lax.broadcasted_iota(jnp.int32, (blk, blk), 1)
    mask_c = col <= row
    mask_p = (col - row) >= jnp.where(n > 0, 0, blk)
    lane = lax.broadcasted_iota(jnp.int32, (blk, LANES), 1)
    scale = hd ** -0.5
    nt = (((1,), (1,)), ((), ()))
    hpt = heads // l_ref.shape[0]

    def qk(cs, key_ref, mask):
        s = lax.dot_general(q_ref[0, 0, :, cs], key_ref[:, cs], nt, preferred_element_type=F32)
        return jnp.where(mask, s * scale, NEG)

    kc_ref, vc_ref = k_ref.at[0, 0], v_ref.at[0, 0]
    for g0 in range(0, heads, hpt):
        cols = [slice(h * hd, (h + 1) * hd) for h in range(g0, g0 + hpt)]
        s_c = [qk(cs, kc_ref, mask_c) for cs in cols]
        s_p = [qk(cs, kp_ref, mask_p) for cs in cols]
        m = [jnp.maximum(jnp.max(a, axis=-1, keepdims=True), jnp.max(b, axis=-1, keepdims=True))
             for a, b in zip(s_c, s_p)]
        p_c = [jnp.exp(a - mm) for a, mm in zip(s_c, m)]
        p_p = [jnp.exp(b - mm) for b, mm in zip(s_p, m)]
        den = [jnp.sum(a, axis=-1, keepdims=True) + jnp.sum(b, axis=-1, keepdims=True)
               for a, b in zip(p_c, p_p)]
        inv = [1.0 / d for d in den]
        lse_tile = jnp.zeros((blk, LANES), F32)
        for i, cs in enumerate(cols):
            o_ref[0, 0, :, cs] = (
                jnp.dot((p_c[i] * inv[i]).astype(BF16), vc_ref[:, cs], preferred_element_type=F32)
                + jnp.dot((p_p[i] * inv[i]).astype(BF16), vp_ref[:, cs], preferred_element_type=F32)
            ).astype(o_ref.dtype)
            lse_tile = jnp.where(lane == i, m[i] + jnp.log(den[i]), lse_tile)
        l_ref[g0 // hpt, 0, 0] = lse_tile

    kp_ref[...] = k_ref[0, 0]
    vp_ref[...] = v_ref[0, 0]

    @pl.when(step == last)
    def _():
        for cp in out_copies(last % 2, last // 2):
            cp.wait()


LSE_TILES = 4
COPY_STREAMS = 4


def _dilated_prompt(q, k, v, gi, *, shift_jobs, name):
    win, dil = DIL_PAIRS[gi]
    blk = win // dil
    H = DIL_HEADS
    batch, _, sd, width = q.shape
    hd = width // H
    assert q.shape[1] == dil and sd % blk == 0
    nb = sd // blk
    slab = pl.BlockSpec((1, 1, blk, width), lambda b, r, n: (b, r, n, 0))
    in_specs = [slab, slab, slab]
    args = [q, k, v]
    out_specs = [slab, pl.BlockSpec((LSE_TILES, 1, 1, blk, LANES), lambda b, r, n: (0, b, r, n, 0))]
    out_shape = [jax.ShapeDtypeStruct(q.shape, BF16),
                 jax.ShapeDtypeStruct((LSE_TILES, batch, dil, sd, LANES), F32)]

    steps = batch * dil * nb
    caches = [c for c, _ in shift_jobs]
    news = [a for _, a in shift_jobs]
    ts = news[0].shape[1]
    Bc = caches[0].shape[0]
    assert steps == 2 * Bc
    jobs, slot_rows = [], [0, 0]
    for c in caches:
        keep = c.shape[1] - ts
        first = keep // 2 if keep * c.shape[2] * c.shape[3] * 4 > (2 << 20) else 0
        halves = ((0, first, False), (first, keep - first, True))
        jobs.append(halves)
        for h in range(2):
            slot_rows[h] += halves[h][1] + (ts if halves[h][2] else 0)
    any_spec = pl.BlockSpec(memory_space=pl.ANY)
    in_specs += [any_spec] * len(caches) + [pl.BlockSpec(memory_space=pltpu.VMEM)] * len(news)
    args += caches + news
    out_specs = out_specs + [any_spec] * len(caches)
    out_shape = out_shape + [jax.ShapeDtypeStruct(c.shape, c.dtype) for c in caches]

    res = pl.pallas_call(
        functools.partial(_dilated_prompt_kernel, heads=H, hd=hd, jobs=tuple(jobs), ts=ts, steps=steps),
        grid=(batch, dil, nb),
        in_specs=in_specs,
        out_specs=out_specs,
        out_shape=out_shape,
        scratch_shapes=[pltpu.VMEM((blk, width), BF16), pltpu.VMEM((blk, width), BF16),
                        pltpu.VMEM((2, max(slot_rows)) + caches[0].shape[2:], caches[0].dtype),
                        pltpu.SemaphoreType.DMA((2,)), pltpu.SemaphoreType.DMA((2,))],
        compiler_params=_cparams("arbitrary", "arbitrary", "arbitrary"),
        name=name,
    )(*args)
    return res[0], res[1], list(res[2:])


def _merge_kernel(o0_ref, o1_ref, o2_ref, l0_ref, l1_ref, l2_ref, init_ref, out_ref,
                  s0_ref, s1_ref, nat_ref, *, hd):
    del init_ref
    d1, d2 = o1_ref.shape[1], o2_ref.shape[1]
    rows = o2_ref.shape[2]
    rep = d2 // d1
    nh = o2_ref.shape[3] // hd
    for hh in range(nh):
        cs = slice(hh * hd, (hh + 1) * hd)
        s0_ref[hh] = o0_ref[:, cs].astype(F32)
        s1_ref[hh] = o1_ref[0, :, :, cs].astype(F32)
    for r in range(d2):
        l2 = l2_ref[0, 0, r]
        l1 = l1_ref[0, 0, r % d1, pl.ds(r // d1, rows, stride=rep), :]
        l0 = l0_ref[0, pl.ds(r, rows, stride=d2), :]
        mx = jnp.maximum(jnp.maximum(l0, l1), l2)
        e0 = jnp.exp(l0 - mx)
        e1 = jnp.exp(l1 - mx)
        e2 = jnp.exp(l2 - mx)
        tot = 1.0 / (e0 + e1 + e2)
        w0, w1, w2 = e0 * tot, e1 * tot, e2 * tot
        for hh in range(nh):
            ls = slice(hh, hh + 1)
            o2 = o2_ref[0, r, :, hh * hd:(hh + 1) * hd].astype(F32)
            o1 = s1_ref[hh, r % d1, pl.ds(r // d1, rows, stride=rep), :]
            o0 = s0_ref[hh, pl.ds(r, rows, stride=d2), :]
            nat_ref[hh, pl.ds(r, rows, stride=d2), :] = w0[:, ls] * o0 + w1[:, ls] * o1 + w2[:, ls] * o2
    for hh in range(nh):
        out_ref[:, hh * hd:(hh + 1) * hd] = nat_ref[hh].astype(out_ref.dtype)


def _merge(outs, lses, init, *, name):
    (_, d0), (_, d1), (_, d2) = DIL_PAIRS
    batch, _, sd2, width = outs[2].shape
    seq = sd2 * d2
    rows = DIL_PAIRS[2][0] // d2
    span = rows * d2
    assert d0 == 1 and d2 % d1 == 0 and seq % span == 0
    nsp = seq // span
    cw = width // LSE_TILES
    hd = width // DIL_HEADS
    o0 = outs[0].reshape(batch * seq, width)
    l0 = lses[0].reshape(LSE_TILES, batch * seq, LANES)
    in_specs = [
        pl.BlockSpec((span, cw), lambda t, c: (t, c)),
        pl.BlockSpec((1, d1, span // d1, cw), lambda t, c: (t // nsp, 0, t % nsp, c)),
        pl.BlockSpec((1, d2, rows, cw), lambda t, c: (t // nsp, 0, t % nsp, c)),
        pl.BlockSpec((1, span, LANES), lambda t, c: (c, t, 0)),
        pl.BlockSpec((1, 1, d1, span // d1, LANES), lambda t, c: (c, t // nsp, 0, t % nsp, 0)),
        pl.BlockSpec((1, 1, d2, rows, LANES), lambda t, c: (c, t // nsp, 0, t % nsp, 0)),
        pl.BlockSpec(memory_space=pl.ANY),
    ]
    return pl.pallas_call(
        functools.partial(_merge_kernel, hd=hd),
        grid=(batch * nsp, LSE_TILES),
        in_specs=in_specs,
        out_specs=pl.BlockSpec((span, cw), lambda t, c: (t, c)),
        out_shape=jax.ShapeDtypeStruct(init.shape, init.dtype),
        scratch_shapes=[pltpu.VMEM((cw // hd, span, hd), F32),
                        pltpu.VMEM((cw // hd, d1, span // d1, hd), F32),
                        pltpu.VMEM((cw // hd, span, hd), F32)],
        input_output_aliases={6: 0},
        compiler_params=_cparams("parallel", "parallel"),
        name=name,
    )(o0, outs[1], outs[2], l0, lses[1], lses[2], init)


def _dilated_sample_kernel(q_ref, kn_ref, vn_ref, k0_ref, v0_ref, k1_ref, v1_ref, k2_ref, v2_ref,
                           o_ref, *, ts, hd):
    caches = ((k0_ref, v0_ref), (k1_ref, v1_ref), (k2_ref, v2_ref))
    scale = hd ** -0.5
    for t in range(ts):
        outs, lses = [], []
        for gi, (win, dil) in enumerate(DIL_PAIRS):
            kc_ref, vc_ref = caches[gi]
            q = q_ref[0, t, gi]
            r = t % dil if dil > 1 else 0
            kc = kc_ref[0, :, r]
            vc = vc_ref[0, :, r]
            s_c = jnp.sum(kc * q[None], axis=-1, keepdims=True) * scale
            if dil == 1:
                cidx = lax.broadcasted_iota(jnp.int32, s_c.shape, 0)
                s_c = jnp.where(cidx >= t, s_c, NEG)
                new_rows = list(range(t + 1))
            else:
                new_rows = [t]
            s_n = [jnp.sum(kn_ref[0, u, gi] * q, axis=-1, keepdims=True) * scale for u in new_rows]
            m = jnp.max(s_c, axis=0)
            for sn in s_n:
                m = jnp.maximum(m, sn)
            p_c = jnp.exp(s_c - m[None])
            p_n = [jnp.exp(sn - m) for sn in s_n]
            den = jnp.sum(p_c, axis=0)
            for pn in p_n:
                den = den + pn
            inv = 1.0 / den
            o = jnp.sum((p_c * inv[None]) * vc, axis=0)
            for u, pn in zip(new_rows, p_n):
                o = o + (pn * inv) * vn_ref[0, u, gi]
            outs.append(o)
            lses.append(m + jnp.log(den))
        mx = jnp.maximum(jnp.maximum(lses[0], lses[1]), lses[2])
        es = [jnp.exp(l - mx) for l in lses]
        tot = 1.0 / (es[0] + es[1] + es[2])
        o_ref[0, t] = (es[0] * tot) * outs[0] + (es[1] * tot) * outs[1] + (es[2] * tot) * outs[2]


def _dilated_sample(q, knew, vnew, caches_k, caches_v, *, name):
    B, ts, G, H, hd = q.shape
    small = pl.BlockSpec((1, ts, G, H, hd), lambda b: (b, 0, 0, 0, 0))
    in_specs = [small, small, small]
    args = [q, knew, vnew]
    for gi, (win, dil) in enumerate(DIL_PAIRS):
        L = caches_k[gi].shape[1]
        assert L == win and L % dil == 0 and (dil == 1 or ts <= dil)
        nres = min(dil, ts)
        spec = pl.BlockSpec((1, L // dil, nres, H, hd), lambda b: (b, 0, 0, 0, 0))
        for c in (caches_k[gi], caches_v[gi]):
            in_specs.append(spec)
            args.append(c.reshape(B, L // dil, dil, H, hd))
    return pl.pallas_call(
        functools.partial(_dilated_sample_kernel, ts=ts, hd=hd),
        grid=(B,),
        in_specs=in_specs,
        out_specs=pl.BlockSpec((1, ts, H, hd), lambda b: (b, 0, 0, 0)),
        out_shape=jax.ShapeDtypeStruct((B, ts, H, hd), F32),
        compiler_params=_cparams("parallel"),
        name=name,
    )(*args)


def kernel(x_prompt, x_sample, state_ret, cache_k_w128, cache_v_w128, cache_k_w512, cache_v_w512,
           cache_k_w2048, cache_v_w2048, norm_mix, norm_ffn, ret_w_in, ret_gn, ret_w_out,
           kv_norm, w_kv, dil_w_q, dil_w_o, ffn_w1, ffn_w3, ffn_w2, norm_final):
    Bp, Tp, D = x_prompt.shape
    Bs, Ts, _ = x_sample.shape
    H, hd = DIL_HEADS, D // DIL_HEADS
    caches_k = [cache_k_w128, cache_k_w512, cache_k_w2048]
    caches_v = [cache_v_w128, cache_v_w512, cache_v_w2048]
    MP, MS = Bp * Tp, Bs * SAMPLE_PAD
    MT = MP + MS
    NT = MT // 16

    h = jnp.concatenate([x_prompt.reshape(MP, D),
                         jnp.pad(x_sample, ((0, 0), (0, SAMPLE_PAD - Ts), (0, 0))).reshape(MS, D)])

    def ffn(h, layer):
        xn, = _norm(h, [norm_ffn[layer]], out_dtype=BF16, tm=NT, name=f"ffn_norm_{layer}")
        act = _linear(xn, [(ffn_w1, layer), (ffn_w3, layer)], swiglu=True, out_dtype=BF16, tn=512,
                      name=f"ffn_up_{layer}")
        return _linear(act, [(ffn_w2, layer)], residual=h, out_dtype=F32, tn=256, name=f"ffn_down_{layer}")

    chunk_p = RET_CHUNK if Tp % RET_CHUNK == 0 else Tp
    xn, = _norm(h, [norm_mix[0]], out_dtype=BF16, tm=NT, name="mix_norm_0")
    proj = _linear(xn, [ret_w_in[0]], out_dtype=BF16, tn=512, name="ret_in")
    s0_p = jnp.zeros((Bp,) + state_ret.shape[2:], F32)
    gated = jnp.zeros((MT, RET_HEADS * state_ret.shape[-1]), BF16)
    gated, sp = _retention(proj, s0_p, ret_gn[0], row0=0, batch=Bp, seq=Tp, chunk=chunk_p,
                           chunk_true=chunk_p, pos=jnp.arange(Tp), out_init=gated, name="retention_p")
    gated, ss = _retention(proj, state_ret[0], ret_gn[0], row0=MP, batch=Bs, seq=SAMPLE_PAD,
                           chunk=SAMPLE_PAD, chunk_true=Ts, pos=PAST_LEN + jnp.arange(SAMPLE_PAD),
                           out_init=gated, name="retention_s")
    h = _linear(gated, [ret_w_out[0]], residual=h, out_dtype=F32, tn=256, name="ret_out")
    h = ffn(h, 0)

    xkv, xq = _norm(h, [kv_norm, norm_mix[1]], out_dtype=BF16, tm=NT, name="kv_q_norm")
    qd, kd, vd, new_kp, new_vp = [], [], [], [], []
    for gi, (win, dil) in enumerate(DIL_PAIRS):
        assert Tp % win == 0
        qd.append(_dilated_linear(xq, dil_w_q[0], gi * D, batch=Bp, seq=Tp, dil=dil, tn=1024,
                                  name=f"dil_q_{gi}"))
        kg, kw = _dilated_linear(xkv, w_kv, 2 * gi * D, batch=Bp, seq=Tp, dil=dil, window=win,
                                 name=f"dil_k_{gi}")
        vg, vw = _dilated_linear(xkv, w_kv, (2 * gi + 1) * D, batch=Bp, seq=Tp, dil=dil, window=win,
                                 name=f"dil_v_{gi}")
        kd.append(kg)
        vd.append(vg)
        new_kp.append(kw.reshape(Bp, win, H, hd))
        new_vp.append(vw.reshape(Bp, win, H, hd))
    kv_s = _linear(xkv, [w_kv], out_dtype=F32, tn=1024, row0=MP, rows=MS, row_tiles=1, name="kv_s")
    q_s = _linear(xq, [dil_w_q[0]], out_dtype=F32, tn=1024, row0=MP, rows=MS, row_tiles=1, name="q_s")
    kv_s6 = kv_s.reshape(Bs, SAMPLE_PAD, N_GROUPS, 2, H, hd)[:, :Ts]
    knew, vnew = kv_s6[:, :, :, 0], kv_s6[:, :, :, 1]
    q_s5 = q_s.reshape(Bs, SAMPLE_PAD, N_GROUPS, H, hd)[:, :Ts]

    job = lambda c, a, i: (c[i], a[:, :, i])
    o0, l0, (nk2,) = _dilated_prompt(qd[0], kd[0], vd[0], 0, shift_jobs=[job(caches_k, knew, 2)],
                                     name="dilated_p0")
    o1, l1, (nk0, nv0, nk1, nv1) = _dilated_prompt(
        qd[1], kd[1], vd[1], 1, name="dilated_p1",
        shift_jobs=[job(caches_k, knew, 0), job(caches_v, vnew, 0),
                    job(caches_k, knew, 1), job(caches_v, vnew, 1)])
    o2, l2, (nv2,) = _dilated_prompt(qd[2], kd[2], vd[2], 2, shift_jobs=[job(caches_v, vnew, 2)],
                                     name="dilated_p2")
    a_s = _dilated_sample(q_s5, knew, vnew, caches_k, caches_v, name="dilated_s")
    a_s = jnp.pad(a_s.reshape(Bs, Ts, D), ((0, 0), (0, SAMPLE_PAD - Ts), (0, 0))).reshape(MS, D)
    a = jnp.concatenate([jnp.zeros((MP, D), BF16), a_s.astype(BF16)])
    a = _merge([o0, o1, o2], [l0, l1, l2], a, name="dilated_merge")
    h = _linear(a, [dil_w_o[0]], residual=h, out_dtype=F32, tn=512, name="dil_o")
    h = ffn(h, 1)

    y_p, = _norm(h, [norm_final], out_dtype=F32, tm=MS, rows=MP, name="final_norm_p")
    y_s, = _norm(h, [norm_final], out_dtype=F32, tm=MS, row0=MP, rows=MS, name="final_norm_s")
    return (y_p.reshape(Bp, Tp, D), y_s.reshape(Bs, SAMPLE_PAD, D)[:, :Ts], sp[None], ss[None],
            new_kp[0], new_vp[0], new_kp[1], new_vp[1], new_kp[2], new_vp[2],
            nk0, nv0, nk1, nv1, nk2, nv2)
```

```python
import functools

import jax
import jax.numpy as jnp
from jax import lax
from jax.experimental import pallas as pl
from jax.experimental.pallas import tpu as pltpu

F32 = jnp.float32
BF16 = jnp.bfloat16

RET_HEADS = 8
RET_CHUNK = 128
ROPE_BASE = 10000.0
DIL_PAIRS = ((128, 1), (512, 4), (2048, 16))
N_GROUPS = len(DIL_PAIRS)
DIL_HEADS = 16
PAST_LEN = 8192
EPS = 1e-6
NEG = -1e30

LANES = 128
SAMPLE_PAD = 16
VMEM_LIMIT = 61 * 1024 * 1024
ROW_TILES = 4


def _cparams(*sem):
    return pltpu.CompilerParams(dimension_semantics=sem, vmem_limit_bytes=VMEM_LIMIT)


def _norm_kernel(x_ref, g_ref, *o_refs):
    xf = x_ref[...]
    ms = jnp.mean(xf * xf, axis=-1, keepdims=True)
    xs = xf * lax.rsqrt(ms + EPS)
    for k, o_ref in enumerate(o_refs):
        o_ref[...] = (xs * g_ref[k:k + 1, :]).astype(o_ref.dtype)


def _norm(x, gains, *, out_dtype, tm, row0=0, rows=None, name):
    M, K = x.shape
    rows = M if rows is None else rows
    assert rows % tm == 0 and row0 % tm == 0
    g = jnp.stack(gains).astype(F32)
    off = row0 // tm
    outs = pl.pallas_call(
        _norm_kernel,
        grid=(rows // tm,),
        in_specs=[pl.BlockSpec((tm, K), lambda i: (i + off, 0)),
                  pl.BlockSpec((len(gains), K), lambda i: (0, 0))],
        out_specs=[pl.BlockSpec((tm, K), lambda i: (i, 0))] * len(gains),
        out_shape=[jax.ShapeDtypeStruct((rows, K), out_dtype)] * len(gains),
        compiler_params=_cparams("parallel"),
        name=name,
    )(x, g)
    return outs


def _linear_kernel(*refs, n_w, has_res, swiglu):
    it = iter(refs)
    x_ref = next(it)
    w_refs = [next(it) for _ in range(n_w)]
    r_ref = next(it) if has_res else None
    o_ref = next(it)
    xn = x_ref[...]
    a = jnp.dot(xn, w_refs[0][...].astype(BF16), preferred_element_type=F32)
    if swiglu:
        b = jnp.dot(xn, w_refs[1][...].astype(BF16), preferred_element_type=F32)
        a = (a * jax.nn.sigmoid(a)) * b
    if has_res:
        a = a + r_ref[...]
    o_ref[...] = a.astype(o_ref.dtype)


def _linear(x, ws, *, residual=None, swiglu=False, out_dtype, tn, row0=0, rows=None,
            row_tiles=ROW_TILES, name):
    K = x.shape[1]
    M = x.shape[0] if rows is None else rows
    ws = [w if isinstance(w, tuple) else (w[None], 0) for w in ws]
    N = ws[0][0].shape[2]
    assert M % (row_tiles * 16) == 0 and N % tn == 0 and x.dtype == BF16
    tm = M // row_tiles
    assert row0 % tm == 0 and (residual is None or row0 == 0)
    off = row0 // tm
    has_res = residual is not None
    in_specs = [pl.BlockSpec((tm, K), lambda i, j: (i + off, 0), pipeline_mode=pl.Buffered(1))]
    args = [x]
    for w, layer in ws:
        in_specs.append(pl.BlockSpec((None, K, tn), lambda i, j, layer=layer: (layer, 0, j)))
        args.append(w)
    if has_res:
        in_specs.append(pl.BlockSpec((tm, tn), lambda i, j: (i, j)))
        args.append(residual)
    kern = functools.partial(_linear_kernel, n_w=len(ws), has_res=has_res, swiglu=swiglu)
    return pl.pallas_call(
        kern,
        grid=(row_tiles, N // tn),
        in_specs=in_specs,
        out_specs=pl.BlockSpec((tm, tn), lambda i, j: (i, j)),
        out_shape=jax.ShapeDtypeStruct((M, N), out_dtype),
        compiler_params=_cparams("parallel", "arbitrary"),
        name=name,
    )(*args)


MXU_COLS = 256


def _dilated_linear_kernel(x_ref, xs_ref, w_ref, *refs, dil, win_rows, row_parts):
    it = iter(refs)
    o_ref = next(it)
    os_ref = next(it)
    win_ref = next(it) if win_rows else None
    acc_ref = next(it) if dil > 1 else None
    tm, tn = x_ref.shape[0], w_ref.shape[1]
    rp = tm // row_parts
    w0 = tm - win_rows
    for c in range(tn // MXU_COLS):
        cs = slice(c * MXU_COLS, (c + 1) * MXU_COLS)
        wb = w_ref[:, cs].astype(BF16)
        for p in range(row_parts):
            lo = p * rp
            acc = jnp.dot(x_ref[lo:lo + rp, :], wb, preferred_element_type=F32)
            if win_rows and lo + rp > w0:
                a = max(w0, lo)
                win_ref[0, a - w0:lo + rp - w0, cs] = acc[a - lo:, :]
            if dil == 1:
                o_ref[0, 0, lo:lo + rp, cs] = acc.astype(o_ref.dtype)
            else:
                n = rp // dil
                for s in range(c * MXU_COLS // LANES, (c + 1) * MXU_COLS // LANES):
                    ls = slice(s * LANES, (s + 1) * LANES)
                    acc_ref[s, lo:lo + rp, :] = acc[:, ls.start - c * MXU_COLS:ls.stop - c * MXU_COLS]
                    for r in range(dil):
                        o_ref[0, r, p * n:(p + 1) * n, ls] = (
                            acc_ref[s, pl.ds(lo + r, n, stride=dil), :].astype(o_ref.dtype))

    @pl.when(pl.program_id(0) == 0)
    def _():
        os_ref[...] = jnp.dot(xs_ref[...], w_ref[...].astype(BF16), preferred_element_type=F32)


def _dilated_linear(x, w, col0, *, batch, seq, dil, extra_rows, window=None, tm=2048, tn=512, name):
    K = x.shape[1]
    width = DIL_HEADS * (K // DIL_HEADS)
    row_parts = 2
    assert seq % tm == 0 and tm % (row_parts * dil * 16) == 0 and col0 % tn == 0 and width % tn == 0
    assert (batch * seq) % extra_rows == 0
    tpb = seq // tm
    cb0 = col0 // tn
    ncol = width // tn
    xs_block = (batch * seq) // extra_rows
    in_specs = [pl.BlockSpec((tm, K), lambda i, j: (i, 0)),
                pl.BlockSpec((extra_rows, K), lambda i, j: (xs_block, 0)),
                pl.BlockSpec((K, tn), lambda i, j: (0, cb0 + j))]
    out_specs = [pl.BlockSpec((1, dil, tm // dil, tn), lambda i, j: (i // tpb, 0, i % tpb, j)),
                 pl.BlockSpec((extra_rows, tn), lambda i, j: (0, jnp.where(i == 0, j, ncol - 1)))]
    out_shape = [jax.ShapeDtypeStruct((batch, dil, seq // dil, width), BF16),
                 jax.ShapeDtypeStruct((extra_rows, width), F32)]
    win_rows = 0
    if window is not None:
        win_rows = min(window, tm)
        assert window % win_rows == 0
        t0 = tpb - window // win_rows

        def win_map(i, j):
            t = i % tpb
            inside = t >= t0
            return (i // tpb, jnp.where(inside, t - t0, 0), jnp.where(inside, j, 0))

        out_specs.append(pl.BlockSpec((1, win_rows, tn), win_map))
        out_shape.append(jax.ShapeDtypeStruct((batch, window, width), F32))
    scratch = [pltpu.VMEM((tn // LANES, tm, LANES), F32)] if dil > 1 else []
    return pl.pallas_call(
        functools.partial(_dilated_linear_kernel, dil=dil, win_rows=win_rows, row_parts=row_parts),
        grid=(batch * tpb, ncol),
        in_specs=in_specs,
        out_specs=out_specs,
        out_shape=out_shape,
        scratch_shapes=scratch,
        compiler_params=_cparams("arbitrary", "arbitrary"),
        name=name,
    )(x, x, w)


def _retention_kernel(*refs, dk, dv, hps, chunk, nc):
    (q_ref, k_ref, v_ref, g_ref, cos_ref, sin_ref, dmat_ref, qdec_ref, kdec_ref,
     cdec_ref, gn_ref, s0_ref) = refs[:12]
    o_ref, s_ref = refs[-2:]

    s_ref[...] = s0_ref[...]
    half = dk // 2

    def one_chunk(c):
        rows = slice(None) if nc == 1 else pl.ds(pl.multiple_of(c * chunk, chunk), chunk)
        cos = cos_ref[rows, :]
        sin = sin_ref[rows, :]

        def rope(x):
            x1 = x[:, :half]
            x2 = x[:, half:]
            return jnp.concatenate([x1 * cos - x2 * sin, x1 * sin + x2 * cos], axis=-1)

        for j in range(hps):
            q = rope(q_ref[rows, j * dk:(j + 1) * dk].astype(F32))
            k = rope(k_ref[rows, j * dk:(j + 1) * dk].astype(F32)) * (dk ** -0.5)
            v = v_ref[rows, j * dv:(j + 1) * dv]
            s = s_ref[0, j]
            sc = lax.dot_general(q.astype(BF16), k.astype(BF16), (((1,), (1,)), ((), ())),
                                 preferred_element_type=F32) * dmat_ref[j]
            o = (jnp.dot(sc.astype(BF16), v, preferred_element_type=F32)
                 + jnp.dot((q * qdec_ref[j]).astype(BF16), s.astype(BF16), preferred_element_type=F32))
            kd = (k * kdec_ref[j]).astype(BF16)
            s_ref[0, j] = s * cdec_ref[j] + lax.dot_general(kd, v, (((0,), (0,)), ((), ())),
                                                           preferred_element_type=F32)
            mu = jnp.mean(o, axis=-1, keepdims=True)
            var = jnp.mean(jnp.square(o - mu), axis=-1, keepdims=True)
            on = (o - mu) * lax.rsqrt(var + EPS) * gn_ref[:, j * dv:(j + 1) * dv]
            gate = g_ref[rows, j * dv:(j + 1) * dv].astype(F32)
            o_ref[rows, j * dv:(j + 1) * dv] = ((gate * jax.nn.sigmoid(gate)) * on).astype(o_ref.dtype)

    if nc == 1:
        one_chunk(0)
    else:
        def body(c, carry):
            one_chunk(c)
            return carry

        lax.fori_loop(0, nc, body, 0, unroll=4 if nc % 4 == 0 else 1)


def _retention(proj, s0, gn_g, *, row0, batch, seq, chunk, chunk_true, pos, out_init=None, name):
    H = RET_HEADS
    dk, dv = s0.shape[-2], s0.shape[-1]
    nc = seq // chunk
    assert row0 % chunk == 0
    lg = jnp.log1p(-jnp.exp2(-5.0 - jnp.arange(H, dtype=F32)))
    i = jnp.arange(chunk, dtype=F32)
    dist = i[:, None] - i[None, :]
    dmat = jnp.where(dist >= 0, jnp.exp(jnp.maximum(dist, 0.0)[None] * lg[:, None, None]), 0.0)
    qdec = jnp.exp((i + 1.0)[None, :, None] * lg[:, None, None])
    kdec = jnp.exp((chunk_true - 1.0 - i)[None, :, None] * lg[:, None, None])
    cdec = jnp.exp(chunk_true * lg).reshape(H, 1, 1)
    half = dk // 2
    inv = ROPE_BASE ** (-jnp.arange(half, dtype=F32) / half)
    ang = pos.astype(F32)[:, None] * inv[None, :]
    cos, sin = jnp.cos(ang), jnp.sin(ang)

    hps = H if seq * H * (2 * dk + 2 * dv) * 2 <= (1 << 20) else 1
    hg = H // hps
    kb, vb = hg, (2 * H * dk) // (hps * dv)
    gb = vb + hg
    assert row0 % seq == 0
    rb = row0 // seq
    in_specs = [
        pl.BlockSpec((seq, hps * dk), lambda b, h: (rb + b, h)),
        pl.BlockSpec((seq, hps * dk), lambda b, h: (rb + b, kb + h)),
        pl.BlockSpec((seq, hps * dv), lambda b, h: (rb + b, vb + h)),
        pl.BlockSpec((seq, hps * dv), lambda b, h: (rb + b, gb + h)),
        pl.BlockSpec((seq, half), lambda b, h: (0, 0)),
        pl.BlockSpec((seq, half), lambda b, h: (0, 0)),
        pl.BlockSpec((hps, chunk, chunk), lambda b, h: (h, 0, 0)),
        pl.BlockSpec((hps, chunk, 1), lambda b, h: (h, 0, 0)),
        pl.BlockSpec((hps, chunk, 1), lambda b, h: (h, 0, 0)),
        pl.BlockSpec((hps, 1, 1), lambda b, h: (h, 0, 0)),
        pl.BlockSpec((1, hps * dv), lambda b, h: (0, h)),
        pl.BlockSpec((1, hps, dk, dv), lambda b, h: (b, h, 0, 0)),
    ]
    args = [proj, proj, proj, proj, cos, sin, dmat, qdec, kdec, cdec, gn_g.reshape(1, H * dv), s0]
    aliases = {}
    if out_init is not None:
        in_specs.append(pl.BlockSpec(memory_space=pl.ANY))
        args.append(out_init)
        aliases = {len(args) - 1: 0}
    out_specs = [
        pl.BlockSpec((seq, hps * dv), lambda b, h: (rb + b, h)),
        pl.BlockSpec((1, hps, dk, dv), lambda b, h: (b, h, 0, 0)),
    ]
    out_shape = [jax.ShapeDtypeStruct((proj.shape[0], H * dv), BF16),
                 jax.ShapeDtypeStruct((batch, H, dk, dv), F32)]
    return pl.pallas_call(
        functools.partial(_retention_kernel, dk=dk, dv=dv, hps=hps, chunk=chunk, nc=nc),
        grid=(batch, hg),
        in_specs=in_specs,
        out_specs=out_specs,
        out_shape=out_shape,
        input_output_aliases=aliases,
        compiler_params=_cparams("parallel", "parallel"),
        name=name,
    )(*args)


def _dilated_prompt_kernel(*refs, heads, hd, jobs, ts, steps):
    q_ref, k_ref, v_ref = refs[:3]
    nj = len(jobs)
    caches, news = refs[3:3 + nj], refs[3 + nj:3 + 2 * nj]
    o_ref, l_ref = refs[3 + 2 * nj:5 + 2 * nj]
    shifted = refs[5 + 2 * nj:5 + 3 * nj]
    kp_ref, vp_ref, buf, in_sem, out_sem = refs[5 + 3 * nj:]

    n = pl.program_id(2)
    blk = q_ref.shape[2]
    step = (pl.program_id(0) * pl.num_programs(1) + pl.program_id(1)) * pl.num_programs(2) + n
    last = steps - 1

    def plan(h):
        out, off = [], 0
        for j, (halves) in enumerate(jobs):
            r0, nr, tail = halves[h]
            parts = COPY_STREAMS if nr >= COPY_STREAMS * 64 else 1
            for p in range(parts):
                lo, hi = nr * p // parts, nr * (p + 1) // parts
                if hi > lo:
                    last = tail and p == parts - 1
                    out.append((j, r0 + lo, hi - lo, off, last))
                    off += hi - lo + (ts if last else 0)
        return out

    def in_copies(h, bc):
        return [pltpu.make_async_copy(caches[j].at[bc, pl.ds(ts + r0, nr)],
                                      buf.at[h, pl.ds(off, nr)], in_sem.at[h])
                for (j, r0, nr, off, tail) in plan(h)]

    def out_copies(h, bc):
        return [pltpu.make_async_copy(buf.at[h, pl.ds(off, nr + (ts if tail else 0))],
                                      shifted[j].at[bc, pl.ds(r0, nr + (ts if tail else 0))],
                                      out_sem.at[h])
                for (j, r0, nr, off, tail) in plan(h)]

    @pl.when(step == 0)
    def _():
        for cp in in_copies(0, 0):
            cp.start()

    for h in range(2):
        @pl.when(step % 2 == h)
        def _(h=h):
            bc = step // 2
            for cp in in_copies(h, bc):
                cp.wait()
            for (j, r0, nr, off, tail) in plan(h):
                if tail:
                    buf[h, pl.ds(off + nr, ts)] = news[j][bc]
            for cp in out_copies(h, bc):
                cp.start()

            @pl.when(step > 0)
            def _():
                for cp in out_copies(1 - h, (step - 1) // 2):
                    cp.wait()

            @pl.when(step < last)
            def _():
                for cp in in_copies(1 - h, (step + 1) // 2):
                    cp.start()

    @pl.when(n == 0)
    def _():
        kp_ref[...] = jnp.zeros_like(kp_ref)
        vp_ref[...] = jnp.zeros_like(vp_ref)

    row = lax.broadcasted_iota(jnp.int32, (blk, blk), 0)
    col = lax.broadcasted_iota(jnp.int32, (blk, blk), 1)
    mask_c = col <= row
    mask_p = (col - row) >= jnp.where(n > 0, 0, blk)
    lane = lax.broadcasted_iota(jnp.int32, (blk, LANES), 1)
    scale = hd ** -0.5
    nt = (((1,), (1,)), ((), ()))
    hpt = heads // l_ref.shape[0]

    def qk(cs, key_ref, mask):
        s = lax.dot_general(q_ref[0, 0, :, cs], key_ref[:, cs], nt, preferred_element_type=F32)
        return jnp.where(mask, s * scale, NEG)

    kc_ref, vc_ref = k_ref.at[0, 0], v_ref.at[0, 0]
    for g0 in range(0, heads, hpt):
        cols = [slice(h * hd, (h + 1) * hd) for h in range(g0, g0 + hpt)]
        s_c = [qk(cs, kc_ref, mask_c) for cs in cols]
        s_p = [qk(cs, kp_ref, mask_p) for cs in cols]
        m = [jnp.maximum(jnp.max(a, axis=-1, keepdims=True), jnp.max(b, axis=-1, keepdims=True))
             for a, b in zip(s_c, s_p)]
        p_c = [jnp.exp(a - mm) for a, mm in zip(s_c, m)]
        p_p = [jnp.exp(b - mm) for b, mm in zip(s_p, m)]
        den = [jnp.sum(a, axis=-1, keepdims=True) + jnp.sum(b, axis=-1, keepdims=True)
               for a, b in zip(p_c, p_p)]
        inv = [1.0 / d for d in den]
        lse_tile = jnp.zeros((blk, LANES), F32)
        for i, cs in enumerate(cols):
            o_ref[0, 0, :, cs] = (
                jnp.dot((p_c[i] * inv[i]).astype(BF16), vc_ref[:, cs], preferred_element_type=F32)
                + jnp.dot((p_p[i] * inv[i]).astype(BF16), vp_ref[:, cs], preferred_element_type=F32)
            ).astype(o_ref.dtype)
            lse_tile = jnp.where(lane == i, m[i] + jnp.log(den[i]), lse_tile)
        l_ref[g0 // hpt, 0, 0] = lse_tile

    kp_ref[...] = k_ref[0, 0]
    vp_ref[...] = v_ref[0, 0]

    @pl.when(step == last)
    def _():
        for cp in out_copies(last % 2, last // 2):
            cp.wait()


LSE_TILES = 4
COPY_STREAMS = 4


def _dilated_prompt(q, k, v, gi, *, shift_jobs, name):
    win, dil = DIL_PAIRS[gi]
    blk = win // dil
    H = DIL_HEADS
    batch, _, sd, width = q.shape
    hd = width // H
    assert q.shape[1] == dil and sd % blk == 0
    nb = sd // blk
    slab = pl.BlockSpec((1, 1, blk, width), lambda b, r, n: (b, r, n, 0))
    in_specs = [slab, slab, slab]
    args = [q, k, v]
    out_specs = [slab, pl.BlockSpec((LSE_TILES, 1, 1, blk, LANES), lambda b, r, n: (0, b, r, n, 0))]
    out_shape = [jax.ShapeDtypeStruct(q.shape, BF16),
                 jax.ShapeDtypeStruct((LSE_TILES, batch, dil, sd, LANES), F32)]

    steps = batch * dil * nb
    caches = [c for c, _ in shift_jobs]
    news = [a for _, a in shift_jobs]
    ts = news[0].shape[1]
    Bc = caches[0].shape[0]
    assert steps == 2 * Bc
    jobs, slot_rows = [], [0, 0]
    for c in caches:
        keep = c.shape[1] - ts
        first = keep // 2 if keep * c.shape[2] * c.shape[3] * 4 > (2 << 20) else 0
        halves = ((0, first, False), (first, keep - first, True))
        jobs.append(halves)
        for h in range(2):
            slot_rows[h] += halves[h][1] + (ts if halves[h][2] else 0)
    any_spec = pl.BlockSpec(memory_space=pl.ANY)
    in_specs += [any_spec] * len(caches) + [pl.BlockSpec(memory_space=pltpu.VMEM)] * len(news)
    args += caches + news
    out_specs = out_specs + [any_spec] * len(caches)
    out_shape = out_shape + [jax.ShapeDtypeStruct(c.shape, c.dtype) for c in caches]

    res = pl.pallas_call(
        functools.partial(_dilated_prompt_kernel, heads=H, hd=hd, jobs=tuple(jobs), ts=ts, steps=steps),
        grid=(batch, dil, nb),
        in_specs=in_specs,
        out_specs=out_specs,
        out_shape=out_shape,
        scratch_shapes=[pltpu.VMEM((blk, width), BF16), pltpu.VMEM((blk, width), BF16),
                        pltpu.VMEM((2, max(slot_rows)) + caches[0].shape[2:], caches[0].dtype),
                        pltpu.SemaphoreType.DMA((2,)), pltpu.SemaphoreType.DMA((2,))],
        compiler_params=_cparams("arbitrary", "arbitrary", "arbitrary"),
        name=name,
    )(*args)
    return res[0], res[1], list(res[2:])


def _merge_kernel(o0_ref, o1_ref, o2_ref, l0_ref, l1_ref, l2_ref, init_ref, out_ref,
                  s0_ref, s1_ref, nat_ref, *, hd):
    del init_ref
    d1, d2 = o1_ref.shape[1], o2_ref.shape[1]
    rows = o2_ref.shape[2]
    rep = d2 // d1
    nh = o2_ref.shape[3] // hd
    for hh in range(nh):
        cs = slice(hh * hd, (hh + 1) * hd)
        s0_ref[hh] = o0_ref[:, cs].astype(F32)
        s1_ref[hh] = o1_ref[0, :, :, cs].astype(F32)
    for r in range(d2):
        l2 = l2_ref[0, 0, r]
        l1 = l1_ref[0, 0, r % d1, pl.ds(r // d1, rows, stride=rep), :]
        l0 = l0_ref[0, pl.ds(r, rows, stride=d2), :]
        mx = jnp.maximum(jnp.maximum(l0, l1), l2)
        e0 = jnp.exp(l0 - mx)
        e1 = jnp.exp(l1 - mx)
        e2 = jnp.exp(l2 - mx)
        tot = 1.0 / (e0 + e1 + e2)
        w0, w1, w2 = e0 * tot, e1 * tot, e2 * tot
        for hh in range(nh):
            ls = slice(hh, hh + 1)
            o2 = o2_ref[0, r, :, hh * hd:(hh + 1) * hd].astype(F32)
            o1 = s1_ref[hh, r % d1, pl.ds(r // d1, rows, stride=rep), :]
            o0 = s0_ref[hh, pl.ds(r, rows, stride=d2), :]
            nat_ref[hh, pl.ds(r, rows, stride=d2), :] = w0[:, ls] * o0 + w1[:, ls] * o1 + w2[:, ls] * o2
    for hh in range(nh):
        out_ref[:, hh * hd:(hh + 1) * hd] = nat_ref[hh].astype(out_ref.dtype)


def _merge(outs, lses, init, *, name):
    (_, d0), (_, d1), (_, d2) = DIL_PAIRS
    batch, _, sd2, width = outs[2].shape
    seq = sd2 * d2
    rows = DIL_PAIRS[2][0] // d2
    span = rows * d2
    assert d0 == 1 and d2 % d1 == 0 and seq % span == 0
    nsp = seq // span
    cw = width // LSE_TILES
    hd = width // DIL_HEADS
    o0 = outs[0].reshape(batch * seq, width)
    l0 = lses[0].reshape(LSE_TILES, batch * seq, LANES)
    in_specs = [
        pl.BlockSpec((span, cw), lambda t, c: (t, c)),
        pl.BlockSpec((1, d1, span // d1, cw), lambda t, c: (t // nsp, 0, t % nsp, c)),
        pl.BlockSpec((1, d2, rows, cw), lambda t, c: (t // nsp, 0, t % nsp, c)),
        pl.BlockSpec((1, span, LANES), lambda t, c: (c, t, 0)),
        pl.BlockSpec((1, 1, d1, span // d1, LANES), lambda t, c: (c, t // nsp, 0, t % nsp, 0)),
        pl.BlockSpec((1, 1, d2, rows, LANES), lambda t, c: (c, t // nsp, 0, t % nsp, 0)),
        pl.BlockSpec(memory_space=pl.ANY),
    ]
    return pl.pallas_call(
        functools.partial(_merge_kernel, hd=hd),
        grid=(batch * nsp, LSE_TILES),
        in_specs=in_specs,
        out_specs=pl.BlockSpec((span, cw), lambda t, c: (t, c)),
        out_shape=jax.ShapeDtypeStruct(init.shape, init.dtype),
        scratch_shapes=[pltpu.VMEM((cw // hd, span, hd), F32),
                        pltpu.VMEM((cw // hd, d1, span // d1, hd), F32),
                        pltpu.VMEM((cw // hd, span, hd), F32)],
        input_output_aliases={6: 0},
        compiler_params=_cparams("parallel", "parallel"),
        name=name,
    )(o0, outs[1], outs[2], l0, lses[1], lses[2], init)


def _dilated_sample_kernel(q_ref, kn_ref, vn_ref, k0_ref, v0_ref, k1_ref, v1_ref, k2_ref, v2_ref,
                           o_ref, *, ts, hd):
    caches = ((k0_ref, v0_ref), (k1_ref, v1_ref), (k2_ref, v2_ref))
    scale = hd ** -0.5
    for t in range(ts):
        outs, lses = [], []
        for gi, (win, dil) in enumerate(DIL_PAIRS):
            kc_ref, vc_ref = caches[gi]
            q = q_ref[0, t, gi]
            r = t % dil if dil > 1 else 0
            kc = kc_ref[0, :, r]
            vc = vc_ref[0, :, r]
            s_c = jnp.sum(kc * q[None], axis=-1, keepdims=True) * scale
            if dil == 1:
                cidx = lax.broadcasted_iota(jnp.int32, s_c.shape, 0)
                s_c = jnp.where(cidx >= t, s_c, NEG)
                new_rows = list(range(t + 1))
            else:
                new_rows = [t]
            s_n = [jnp.sum(kn_ref[0, u, gi] * q, axis=-1, keepdims=True) * scale for u in new_rows]
            m = jnp.max(s_c, axis=0)
            for sn in s_n:
                m = jnp.maximum(m, sn)
            p_c = jnp.exp(s_c - m[None])
            p_n = [jnp.exp(sn - m) for sn in s_n]
            den = jnp.sum(p_c, axis=0)
            for pn in p_n:
                den = den + pn
            inv = 1.0 / den
            o = jnp.sum((p_c * inv[None]) * vc, axis=0)
            for u, pn in zip(new_rows, p_n):
                o = o + (pn * inv) * vn_ref[0, u, gi]
            outs.append(o)
            lses.append(m + jnp.log(den))
        mx = jnp.maximum(jnp.maximum(lses[0], lses[1]), lses[2])
        es = [jnp.exp(l - mx) for l in lses]
        tot = 1.0 / (es[0] + es[1] + es[2])
        o_ref[0, t] = (es[0] * tot) * outs[0] + (es[1] * tot) * outs[1] + (es[2] * tot) * outs[2]


def _dilated_sample(q, knew, vnew, caches_k, caches_v, *, name):
    B, ts, G, H, hd = q.shape
    small = pl.BlockSpec((1, ts, G, H, hd), lambda b: (b, 0, 0, 0, 0))
    in_specs = [small, small, small]
    args = [q, knew, vnew]
    for gi, (win, dil) in enumerate(DIL_PAIRS):
        L = caches_k[gi].shape[1]
        assert L == win and L % dil == 0 and (dil == 1 or ts <= dil)
        nres = min(dil, ts)
        spec = pl.BlockSpec((1, L // dil, nres, H, hd), lambda b: (b, 0, 0, 0, 0))
        for c in (caches_k[gi], caches_v[gi]):
            in_specs.append(spec)
            args.append(c.reshape(B, L // dil, dil, H, hd))
    return pl.pallas_call(
        functools.partial(_dilated_sample_kernel, ts=ts, hd=hd),
        grid=(B,),
        in_specs=in_specs,
        out_specs=pl.BlockSpec((1, ts, H, hd), lambda b: (b, 0, 0, 0)),
        out_shape=jax.ShapeDtypeStruct((B, ts, H, hd), F32),
        compiler_params=_cparams("parallel"),
        name=name,
    )(*args)


def kernel(x_prompt, x_sample, state_ret, cache_k_w128, cache_v_w128, cache_k_w512, cache_v_w512,
           cache_k_w2048, cache_v_w2048, norm_mix, norm_ffn, ret_w_in, ret_gn, ret_w_out,
           kv_norm, w_kv, dil_w_q, dil_w_o, ffn_w1, ffn_w3, ffn_w2, norm_final):
    Bp, Tp, D = x_prompt.shape
    Bs, Ts, _ = x_sample.shape
    H, hd = DIL_HEADS, D // DIL_HEADS
    caches_k = [cache_k_w128, cache_k_w512, cache_k_w2048]
    caches_v = [cache_v_w128, cache_v_w512, cache_v_w2048]
    MP, MS = Bp * Tp, Bs * SAMPLE_PAD
    MT = MP + MS
    NT = MT // 16

    h = jnp.concatenate([x_prompt.reshape(MP, D),
                         jnp.pad(x_sample, ((0, 0), (0, SAMPLE_PAD - Ts), (0, 0))).reshape(MS, D)])

    def ffn(h, layer):
        xn, = _norm(h, [norm_ffn[layer]], out_dtype=BF16, tm=NT, name=f"ffn_norm_{layer}")
        act = _linear(xn, [(ffn_w1, layer), (ffn_w3, layer)], swiglu=True, out_dtype=BF16, tn=512,
                      name=f"ffn_up_{layer}")
        return _linear(act, [(ffn_w2, layer)], residual=h, out_dtype=F32, tn=256, name=f"ffn_down_{layer}")

    chunk_p = RET_CHUNK if Tp % RET_CHUNK == 0 else Tp
    xn, = _norm(h, [norm_mix[0]], out_dtype=BF16, tm=NT, name="mix_norm_0")
    proj = _linear(xn, [ret_w_in[0]], out_dtype=BF16, tn=512, name="ret_in")
    s0_p = jnp.zeros((Bp,) + state_ret.shape[2:], F32)
    gated = jnp.zeros((MT, RET_HEADS * state_ret.shape[-1]), BF16)
    gated, sp = _retention(proj, s0_p, ret_gn[0], row0=0, batch=Bp, seq=Tp, chunk=chunk_p,
                           chunk_true=chunk_p, pos=jnp.arange(Tp), out_init=gated, name="retention_p")
    gated, ss = _retention(proj, state_ret[0], ret_gn[0], row0=MP, batch=Bs, seq=SAMPLE_PAD,
                           chunk=SAMPLE_PAD, chunk_true=Ts, pos=PAST_LEN + jnp.arange(SAMPLE_PAD),
                           out_init=gated, name="retention_s")
    h = _linear(gated, [ret_w_out[0]], residual=h, out_dtype=F32, tn=256, name="ret_out")
    h = ffn(h, 0)

    xkv, xq = _norm(h, [kv_norm, norm_mix[1]], out_dtype=BF16, tm=NT, name="kv_q_norm")
    qd, kd, vd, new_kp, new_vp, q_s, k_s, v_s = [], [], [], [], [], [], [], []
    for gi, (win, dil) in enumerate(DIL_PAIRS):
        assert Tp % win == 0
        common = dict(batch=Bp, seq=Tp, dil=dil, extra_rows=MS)
        qg, qs = _dilated_linear(xq, dil_w_q[0], gi * D, tn=1024, name=f"dil_q_{gi}", **common)
        kg, ks, kw = _dilated_linear(xkv, w_kv, 2 * gi * D, window=win, name=f"dil_k_{gi}", **common)
        vg, vs, vw = _dilated_linear(xkv, w_kv, (2 * gi + 1) * D, window=win, name=f"dil_v_{gi}", **common)
        qd.append(qg)
        kd.append(kg)
        vd.append(vg)
        new_kp.append(kw.reshape(Bp, win, H, hd))
        new_vp.append(vw.reshape(Bp, win, H, hd))
        for lst, a in ((q_s, qs), (k_s, ks), (v_s, vs)):
            lst.append(a.reshape(Bs, SAMPLE_PAD, H, hd)[:, :Ts])
    q_s5, knew, vnew = (jnp.stack(lst, axis=2) for lst in (q_s, k_s, v_s))

    job = lambda c, a, i: (c[i], a[:, :, i])
    o0, l0, (nk2,) = _dilated_prompt(qd[0], kd[0], vd[0], 0, shift_jobs=[job(caches_k, knew, 2)],
                                     name="dilated_p0")
    o1, l1, (nk0, nv0, nk1, nv1) = _dilated_prompt(
        qd[1], kd[1], vd[1], 1, name="dilated_p1",
        shift_jobs=[job(caches_k, knew, 0), job(caches_v, vnew, 0),
                    job(caches_k, knew, 1), job(caches_v, vnew, 1)])
    o2, l2, (nv2,) = _dilated_prompt(qd[2], kd[2], vd[2], 2, shift_jobs=[job(caches_v, vnew, 2)],
                                     name="dilated_p2")
    a_s = _dilated_sample(q_s5, knew, vnew, caches_k, caches_v, name="dilated_s")
    a_s = jnp.pad(a_s.reshape(Bs, Ts, D), ((0, 0), (0, SAMPLE_PAD - Ts), (0, 0))).reshape(MS, D)
    a = jnp.concatenate([jnp.zeros((MP, D), BF16), a_s.astype(BF16)])
    a = _merge([o0, o1, o2], [l0, l1, l2], a, name="dilated_merge")
    h = _linear(a, [dil_w_o[0]], residual=h, out_dtype=F32, tn=512, name="dil_o")
    h = ffn(h, 1)

    y_p, = _norm(h, [norm_final], out_dtype=F32, tm=MS, rows=MP, name="final_norm_p")
    y_s, = _norm(h, [norm_final], out_dtype=F32, tm=MS, row0=MP, rows=MS, name="final_norm_s")
    return (y_p.reshape(Bp, Tp, D), y_s.reshape(Bs, SAMPLE_PAD, D)[:, :Ts], sp[None], ss[None],
            new_kp[0], new_vp[0], new_kp[1], new_vp[1], new_kp[2], new_vp[2],
            nk0, nv0, nk1, nv1, nk2, nv2)
```

```python
import functools

import jax
import jax.numpy as jnp
from jax import lax
from jax.experimental import pallas as pl
from jax.experimental.pallas import tpu as pltpu

F32 = jnp.float32
BF16 = jnp.bfloat16

RET_HEADS = 8
RET_CHUNK = 128
ROPE_BASE = 10000.0
DIL_PAIRS = ((128, 1), (512, 4), (2048, 16))
N_GROUPS = len(DIL_PAIRS)
DIL_HEADS = 16
PAST_LEN = 8192
EPS = 1e-6
NEG = -1e30

LANES = 128
SAMPLE_PAD = 16
VMEM_LIMIT = 61 * 1024 * 1024
ROW_TILES = 4


def _cparams(*sem):
    return pltpu.CompilerParams(dimension_semantics=sem, vmem_limit_bytes=VMEM_LIMIT)


def _norm_kernel(x_ref, g_ref, *o_refs):
    xf = x_ref[...]
    ms = jnp.mean(xf * xf, axis=-1, keepdims=True)
    xs = xf * lax.rsqrt(ms + EPS)
    for k, o_ref in enumerate(o_refs):
        o_ref[...] = (xs * g_ref[k:k + 1, :]).astype(o_ref.dtype)


def _norm(x, gains, *, out_dtype, tm, row0=0, rows=None, name):
    M, K = x.shape
    rows = M if rows is None else rows
    assert rows % tm == 0 and row0 % tm == 0
    g = jnp.stack(gains).astype(F32)
    off = row0 // tm
    outs = pl.pallas_call(
        _norm_kernel,
        grid=(rows // tm,),
        in_specs=[pl.BlockSpec((tm, K), lambda i: (i + off, 0)),
                  pl.BlockSpec((len(gains), K), lambda i: (0, 0))],
        out_specs=[pl.BlockSpec((tm, K), lambda i: (i, 0))] * len(gains),
        out_shape=[jax.ShapeDtypeStruct((rows, K), out_dtype)] * len(gains),
        compiler_params=_cparams("parallel"),
        name=name,
    )(x, g)
    return outs


def _linear_kernel(*refs, n_w, has_res, swiglu):
    it = iter(refs)
    x_ref = next(it)
    w_refs = [next(it) for _ in range(n_w)]
    r_ref = next(it) if has_res else None
    o_ref = next(it)
    xn = x_ref[...]
    a = jnp.dot(xn, w_refs[0][...].astype(BF16), preferred_element_type=F32)
    if swiglu:
        b = jnp.dot(xn, w_refs[1][...].astype(BF16), preferred_element_type=F32)
        a = (a * jax.nn.sigmoid(a)) * b
    if has_res:
        a = a + r_ref[...]
    o_ref[...] = a.astype(o_ref.dtype)


def _linear(x, ws, *, residual=None, swiglu=False, out_dtype, tn, row0=0, rows=None,
            row_tiles=ROW_TILES, name):
    K = x.shape[1]
    M = x.shape[0] if rows is None else rows
    ws = [w if isinstance(w, tuple) else (w[None], 0) for w in ws]
    N = ws[0][0].shape[2]
    assert M % (row_tiles * 16) == 0 and N % tn == 0 and x.dtype == BF16
    tm = M // row_tiles
    assert row0 % tm == 0 and (residual is None or row0 == 0)
    off = row0 // tm
    has_res = residual is not None
    in_specs = [pl.BlockSpec((tm, K), lambda i, j: (i + off, 0), pipeline_mode=pl.Buffered(1))]
    args = [x]
    for w, layer in ws:
        in_specs.append(pl.BlockSpec((None, K, tn), lambda i, j, layer=layer: (layer, 0, j)))
        args.append(w)
    if has_res:
        in_specs.append(pl.BlockSpec((tm, tn), lambda i, j: (i, j)))
        args.append(residual)
    kern = functools.partial(_linear_kernel, n_w=len(ws), has_res=has_res, swiglu=swiglu)
    return pl.pallas_call(
        kern,
        grid=(row_tiles, N // tn),
        in_specs=in_specs,
        out_specs=pl.BlockSpec((tm, tn), lambda i, j: (i, j)),
        out_shape=jax.ShapeDtypeStruct((M, N), out_dtype),
        compiler_params=_cparams("parallel", "arbitrary"),
        name=name,
    )(*args)


MXU_COLS = 256


def _dilated_linear_kernel(x_ref, xs_ref, w_ref, *refs, dil, win_rows, row_parts):
    it = iter(refs)
    o_ref = next(it)
    os_ref = next(it)
    win_ref = next(it) if win_rows else None
    acc_ref = next(it) if dil > 1 else None
    tm, tn = x_ref.shape[0], w_ref.shape[1]
    rp = tm // row_parts
    w0 = tm - win_rows
    for c in range(tn // MXU_COLS):
        cs = slice(c * MXU_COLS, (c + 1) * MXU_COLS)
        wb = w_ref[:, cs].astype(BF16)
        for p in range(row_parts):
            lo = p * rp
            acc = jnp.dot(x_ref[lo:lo + rp, :], wb, preferred_element_type=F32)
            if win_rows and lo + rp > w0:
                a = max(w0, lo)
                win_ref[0, a - w0:lo + rp - w0, cs] = acc[a - lo:, :]
            if dil == 1:
                o_ref[0, 0, lo:lo + rp, cs] = acc.astype(o_ref.dtype)
            else:
                n = rp // dil
                for s in range(c * MXU_COLS // LANES, (c + 1) * MXU_COLS // LANES):
                    ls = slice(s * LANES, (s + 1) * LANES)
                    acc_ref[s, lo:lo + rp, :] = acc[:, ls.start - c * MXU_COLS:ls.stop - c * MXU_COLS]
                    for r in range(dil):
                        o_ref[0, r, p * n:(p + 1) * n, ls] = (
                            acc_ref[s, pl.ds(lo + r, n, stride=dil), :].astype(o_ref.dtype))

    @pl.when(pl.program_id(0) == 0)
    def _():
        os_ref[...] = jnp.dot(xs_ref[...], w_ref[...].astype(BF16), preferred_element_type=F32)


def _dilated_linear(x, w, col0, *, batch, seq, dil, extra_rows, window=None, tm=2048, tn=512, name):
    K = x.shape[1]
    width = DIL_HEADS * (K // DIL_HEADS)
    row_parts = 2
    assert seq % tm == 0 and tm % (row_parts * dil * 16) == 0 and col0 % tn == 0 and width % tn == 0
    assert (batch * seq) % extra_rows == 0
    tpb = seq // tm
    cb0 = col0 // tn
    ncol = width // tn
    xs_block = (batch * seq) // extra_rows
    in_specs = [pl.BlockSpec((tm, K), lambda i, j: (i, 0)),
                pl.BlockSpec((extra_rows, K), lambda i, j: (xs_block, 0)),
                pl.BlockSpec((K, tn), lambda i, j: (0, cb0 + j))]
    out_specs = [pl.BlockSpec((1, dil, tm // dil, tn), lambda i, j: (i // tpb, 0, i % tpb, j)),
                 pl.BlockSpec((extra_rows, tn), lambda i, j: (0, jnp.where(i == 0, j, ncol - 1)))]
    out_shape = [jax.ShapeDtypeStruct((batch, dil, seq // dil, width), BF16),
                 jax.ShapeDtypeStruct((extra_rows, width), F32)]
    win_rows = 0
    if window is not None:
        win_rows = min(window, tm)
        assert window % win_rows == 0
        t0 = tpb - window // win_rows

        def win_map(i, j):
            t = i % tpb
            inside = t >= t0
            return (i // tpb, jnp.where(inside, t - t0, 0), jnp.where(inside, j, 0))

        out_specs.append(pl.BlockSpec((1, win_rows, tn), win_map))
        out_shape.append(jax.ShapeDtypeStruct((batch, window, width), F32))
    scratch = [pltpu.VMEM((tn // LANES, tm, LANES), F32)] if dil > 1 else []
    return pl.pallas_call(
        functools.partial(_dilated_linear_kernel, dil=dil, win_rows=win_rows, row_parts=row_parts),
        grid=(batch * tpb, ncol),
        in_specs=in_specs,
        out_specs=out_specs,
        out_shape=out_shape,
        scratch_shapes=scratch,
        compiler_params=_cparams("arbitrary", "arbitrary"),
        name=name,
    )(x, x, w)


RET_UNROLL = 4
COPY_PARTS = 4


def _retention_kernel(*refs, dk, dv, hps, chunk, nc, n_in, copy_rows, ts, steps):
    (q_ref, k_ref, v_ref, g_ref, cos_ref, sin_ref, dmat_ref, qdec_ref, kdec_ref,
     cdec_ref, gn_ref, s0_ref) = refs[:12]
    nj = len(copy_rows)
    caches = refs[n_in - nj:n_in]
    o_ref, s_ref = refs[n_in:n_in + 2]
    shifted = refs[n_in + 2:n_in + 2 + nj]
    unroll = RET_UNROLL if nc % RET_UNROLL == 0 else 1
    trips = nc // unroll

    if nj:
        buf, in_sem, out_sem = refs[n_in + 2 + nj:]
        offs = [sum(q + ts for q in copy_rows[:j]) for j in range(nj)]
        total = steps * trips

        def in_copies(T):
            bc, part, slot = T // COPY_PARTS, T % COPY_PARTS, T % 2
            return [pltpu.make_async_copy(caches[j].at[bc, pl.ds(ts + part * q, q)],
                                          buf.at[slot, pl.ds(offs[j], q)], in_sem.at[slot])
                    for j, q in enumerate(copy_rows)]

        def out_copies(T, tail):
            bc, part, slot = T // COPY_PARTS, T % COPY_PARTS, T % 2
            return [pltpu.make_async_copy(buf.at[slot, pl.ds(offs[j], q + tail)],
                                          shifted[j].at[bc, pl.ds(part * q, q + tail)], out_sem.at[slot])
                    for j, q in enumerate(copy_rows)]

        def out_do(T, start):
            is_last = T % COPY_PARTS == COPY_PARTS - 1
            for cond, tail in ((is_last, ts), (jnp.logical_not(is_last), 0)):
                @pl.when(cond)
                def _(tail=tail):
                    for cp in out_copies(T, tail):
                        cp.start() if start else cp.wait()

        step = pl.program_id(0) * pl.num_programs(1) + pl.program_id(1)

        @pl.when(step == 0)
        def _():
            buf[...] = jnp.zeros_like(buf)
            for cp in in_copies(0):
                cp.start()

        def move(t):
            T = step * trips + t
            for cp in in_copies(T):
                cp.wait()
            out_do(T, True)

            @pl.when(T > 0)
            def _():
                out_do(T - 1, False)

            @pl.when(T < total - 1)
            def _():
                for cp in in_copies(T + 1):
                    cp.start()

            @pl.when(T == total - 1)
            def _():
                out_do(T, False)
    else:
        def move(t):
            del t

    s_ref[...] = s0_ref[...]
    half = dk // 2

    def one_chunk(c):
        rows = slice(None) if nc == 1 else pl.ds(pl.multiple_of(c * chunk, chunk), chunk)
        cos = cos_ref[rows, :]
        sin = sin_ref[rows, :]

        def rope(x):
            x1 = x[:, :half]
            x2 = x[:, half:]
            return jnp.concatenate([x1 * cos - x2 * sin, x1 * sin + x2 * cos], axis=-1)

        for j in range(hps):
            q = rope(q_ref[rows, j * dk:(j + 1) * dk].astype(F32))
            k = rope(k_ref[rows, j * dk:(j + 1) * dk].astype(F32)) * (dk ** -0.5)
            v = v_ref[rows, j * dv:(j + 1) * dv]
            s = s_ref[0, j]
            sc = lax.dot_general(q.astype(BF16), k.astype(BF16), (((1,), (1,)), ((), ())),
                                 preferred_element_type=F32) * dmat_ref[j]
            o = (jnp.dot(sc.astype(BF16), v, preferred_element_type=F32)
                 + jnp.dot((q * qdec_ref[j]).astype(BF16), s.astype(BF16), preferred_element_type=F32))
            kd = (k * kdec_ref[j]).astype(BF16)
            s_ref[0, j] = s * cdec_ref[j] + lax.dot_general(kd, v, (((0,), (0,)), ((), ())),
                                                           preferred_element_type=F32)
            mu = jnp.mean(o, axis=-1, keepdims=True)
            var = jnp.mean(jnp.square(o - mu), axis=-1, keepdims=True)
            on = (o - mu) * lax.rsqrt(var + EPS) * gn_ref[:, j * dv:(j + 1) * dv]
            gate = g_ref[rows, j * dv:(j + 1) * dv].astype(F32)
            o_ref[rows, j * dv:(j + 1) * dv] = ((gate * jax.nn.sigmoid(gate)) * on).astype(o_ref.dtype)

    if nc == 1:
        move(0)
        one_chunk(0)
    else:
        def body(t, carry):
            move(t)
            for u in range(unroll):
                one_chunk(t * unroll + u)
            return carry

        lax.fori_loop(0, trips, body, 0)


def _retention(proj, s0, gn_g, *, row0, batch, seq, chunk, chunk_true, pos, out_init=None,
               move_buffers=(), move_ts=0, name):
    H = RET_HEADS
    dk, dv = s0.shape[-2], s0.shape[-1]
    nc = seq // chunk
    assert row0 % chunk == 0
    lg = jnp.log1p(-jnp.exp2(-5.0 - jnp.arange(H, dtype=F32)))
    i = jnp.arange(chunk, dtype=F32)
    dist = i[:, None] - i[None, :]
    dmat = jnp.where(dist >= 0, jnp.exp(jnp.maximum(dist, 0.0)[None] * lg[:, None, None]), 0.0)
    qdec = jnp.exp((i + 1.0)[None, :, None] * lg[:, None, None])
    kdec = jnp.exp((chunk_true - 1.0 - i)[None, :, None] * lg[:, None, None])
    cdec = jnp.exp(chunk_true * lg).reshape(H, 1, 1)
    half = dk // 2
    inv = ROPE_BASE ** (-jnp.arange(half, dtype=F32) / half)
    ang = pos.astype(F32)[:, None] * inv[None, :]
    cos, sin = jnp.cos(ang), jnp.sin(ang)

    hps = H if seq * H * (2 * dk + 2 * dv) * 2 <= (1 << 20) else 1
    hg = H // hps
    kb, vb = hg, (2 * H * dk) // (hps * dv)
    gb = vb + hg
    assert row0 % seq == 0
    rb = row0 // seq
    in_specs = [
        pl.BlockSpec((seq, hps * dk), lambda b, h: (rb + b, h)),
        pl.BlockSpec((seq, hps * dk), lambda b, h: (rb + b, kb + h)),
        pl.BlockSpec((seq, hps * dv), lambda b, h: (rb + b, vb + h)),
        pl.BlockSpec((seq, hps * dv), lambda b, h: (rb + b, gb + h)),
        pl.BlockSpec((seq, half), lambda b, h: (0, 0)),
        pl.BlockSpec((seq, half), lambda b, h: (0, 0)),
        pl.BlockSpec((hps, chunk, chunk), lambda b, h: (h, 0, 0)),
        pl.BlockSpec((hps, chunk, 1), lambda b, h: (h, 0, 0)),
        pl.BlockSpec((hps, chunk, 1), lambda b, h: (h, 0, 0)),
        pl.BlockSpec((hps, 1, 1), lambda b, h: (h, 0, 0)),
        pl.BlockSpec((1, hps * dv), lambda b, h: (0, h)),
        pl.BlockSpec((1, hps, dk, dv), lambda b, h: (b, h, 0, 0)),
    ]
    args = [proj, proj, proj, proj, cos, sin, dmat, qdec, kdec, cdec, gn_g.reshape(1, H * dv), s0]
    aliases = {}
    if out_init is not None:
        in_specs.append(pl.BlockSpec(memory_space=pl.ANY))
        args.append(out_init)
        aliases = {len(args) - 1: 0}
    out_specs = [
        pl.BlockSpec((seq, hps * dv), lambda b, h: (rb + b, h)),
        pl.BlockSpec((1, hps, dk, dv), lambda b, h: (b, h, 0, 0)),
    ]
    out_shape = [jax.ShapeDtypeStruct((proj.shape[0], H * dv), BF16),
                 jax.ShapeDtypeStruct((batch, H, dk, dv), F32)]
    steps = batch * hg
    copy_rows, scratch = [], []
    if move_buffers:
        trips = nc // (RET_UNROLL if nc % RET_UNROLL == 0 else 1)
        any_spec = pl.BlockSpec(memory_space=pl.ANY)
        for c in move_buffers:
            keep = c.shape[1] - move_ts
            assert keep % COPY_PARTS == 0 and steps * trips == COPY_PARTS * c.shape[0]
            copy_rows.append(keep // COPY_PARTS)
            in_specs.append(any_spec)
            args.append(c)
            out_specs.append(any_spec)
            out_shape.append(jax.ShapeDtypeStruct(c.shape, c.dtype))
        slot_rows = sum(q + move_ts for q in copy_rows)
        scratch = [pltpu.VMEM((2, slot_rows) + move_buffers[0].shape[2:], move_buffers[0].dtype),
                   pltpu.SemaphoreType.DMA((2,)), pltpu.SemaphoreType.DMA((2,))]
    res = pl.pallas_call(
        functools.partial(_retention_kernel, dk=dk, dv=dv, hps=hps, chunk=chunk, nc=nc, n_in=len(args),
                          copy_rows=tuple(copy_rows), ts=move_ts, steps=steps),
        grid=(batch, hg),
        in_specs=in_specs,
        out_specs=out_specs,
        out_shape=out_shape,
        scratch_shapes=scratch,
        input_output_aliases=aliases,
        compiler_params=_cparams("arbitrary", "arbitrary"),
        name=name,
    )(*args)
    return res[0], res[1], list(res[2:])


def _dilated_prompt_kernel(*refs, heads, hd, jobs, ts, steps, tails_only):
    q_ref, k_ref, v_ref = refs[:3]
    nj = len(jobs)
    caches, news = refs[3:3 + nj], refs[3 + nj:3 + 2 * nj]
    o_ref, l_ref = refs[3 + 2 * nj:5 + 2 * nj]
    shifted = refs[5 + 2 * nj:5 + 3 * nj]
    kp_ref, vp_ref, buf, in_sem, out_sem = refs[5 + 3 * nj:]

    n = pl.program_id(2)
    blk = q_ref.shape[2]
    step = (pl.program_id(0) * pl.num_programs(1) + pl.program_id(1)) * pl.num_programs(2) + n
    last = steps - 1

    def plan(h):
        out, off = [], 0
        for j, (halves) in enumerate(jobs):
            r0, nr, tail = halves[h]
            parts = COPY_STREAMS if nr >= COPY_STREAMS * 64 else 1
            for p in range(parts):
                lo, hi = nr * p // parts, nr * (p + 1) // parts
                if hi > lo:
                    last = tail and p == parts - 1
                    out.append((j, r0 + lo, hi - lo, off, last))
                    off += hi - lo + (ts if last else 0)
        return out

    def in_copies(h, bc):
        return [pltpu.make_async_copy(caches[j].at[bc, pl.ds(ts + r0, nr)],
                                      buf.at[h, pl.ds(off, nr)], in_sem.at[h])
                for (j, r0, nr, off, tail) in plan(h)]

    def out_copies(h, bc):
        return [pltpu.make_async_copy(buf.at[h, pl.ds(off, nr + (ts if tail else 0))],
                                      shifted[j].at[bc, pl.ds(r0, nr + (ts if tail else 0))],
                                      out_sem.at[h])
                for (j, r0, nr, off, tail) in plan(h)]

    def tail_copies(bc):
        return [pltpu.make_async_copy(buf.at[0, pl.ds(j * ts, ts)],
                                      shifted[j].at[bc, pl.ds(shifted[j].shape[1] - ts, ts)],
                                      out_sem.at[0]) for j in range(nj)]

    if tails_only:
        @pl.when(jnp.logical_and(step % 2 == 0, step > 0))
        def _():
            for cp in tail_copies((step - 1) // 2):
                cp.wait()

        @pl.when(step % 2 == 1)
        def _():
            for j in range(nj):
                buf[0, pl.ds(j * ts, ts)] = news[j][step // 2]
            for cp in tail_copies(step // 2):
                cp.start()
    else:
        @pl.when(step == 0)
        def _():
            for cp in in_copies(0, 0):
                cp.start()

        for h in range(2):
            @pl.when(step % 2 == h)
            def _(h=h):
                bc = step // 2
                for cp in in_copies(h, bc):
                    cp.wait()
                for (j, r0, nr, off, tail) in plan(h):
                    if tail:
                        buf[h, pl.ds(off + nr, ts)] = news[j][bc]
                for cp in out_copies(h, bc):
                    cp.start()

                @pl.when(step > 0)
                def _():
                    for cp in out_copies(1 - h, (step - 1) // 2):
                        cp.wait()

                @pl.when(step < last)
                def _():
                    for cp in in_copies(1 - h, (step + 1) // 2):
                        cp.start()

    @pl.when(n == 0)
    def _():
        kp_ref[...] = jnp.zeros_like(kp_ref)
        vp_ref[...] = jnp.zeros_like(vp_ref)

    row = lax.broadcasted_iota(jnp.int32, (blk, blk), 0)
    col = lax.broadcasted_iota(jnp.int32, (blk, blk), 1)
    mask_c = col <= row
    mask_p = (col - row) >= jnp.where(n > 0, 0, blk)
    lane = lax.broadcasted_iota(jnp.int32, (blk, LANES), 1)
    scale = hd ** -0.5
    nt = (((1,), (1,)), ((), ()))
    hpt = heads // l_ref.shape[0]

    def qk(cs, key_ref, mask):
        s = lax.dot_general(q_ref[0, 0, :, cs], key_ref[:, cs], nt, preferred_element_type=F32)
        return jnp.where(mask, s * scale, NEG)

    kc_ref, vc_ref = k_ref.at[0, 0], v_ref.at[0, 0]
    for g0 in range(0, heads, hpt):
        cols = [slice(h * hd, (h + 1) * hd) for h in range(g0, g0 + hpt)]
        s_c = [qk(cs, kc_ref, mask_c) for cs in cols]
        s_p = [qk(cs, kp_ref, mask_p) for cs in cols]
        m = [jnp.maximum(jnp.max(a, axis=-1, keepdims=True), jnp.max(b, axis=-1, keepdims=True))
             for a, b in zip(s_c, s_p)]
        p_c = [jnp.exp(a - mm) for a, mm in zip(s_c, m)]
        p_p = [jnp.exp(b - mm) for b, mm in zip(s_p, m)]
        den = [jnp.sum(a, axis=-1, keepdims=True) + jnp.sum(b, axis=-1, keepdims=True)
               for a, b in zip(p_c, p_p)]
        inv = [1.0 / d for d in den]
        lse_tile = jnp.zeros((blk, LANES), F32)
        for i, cs in enumerate(cols):
            o_ref[0, 0, :, cs] = (
                jnp.dot((p_c[i] * inv[i]).astype(BF16), vc_ref[:, cs], preferred_element_type=F32)
                + jnp.dot((p_p[i] * inv[i]).astype(BF16), vp_ref[:, cs], preferred_element_type=F32)
            ).astype(o_ref.dtype)
            lse_tile = jnp.where(lane == i, m[i] + jnp.log(den[i]), lse_tile)
        l_ref[g0 // hpt, 0, 0] = lse_tile

    kp_ref[...] = k_ref[0, 0]
    vp_ref[...] = v_ref[0, 0]

    @pl.when(step == last)
    def _():
        for cp in (tail_copies(last // 2) if tails_only else out_copies(last % 2, last // 2)):
            cp.wait()


LSE_TILES = 4
COPY_STREAMS = 4


def _dilated_prompt(q, k, v, gi, *, shift_jobs, tails_only=False, name):
    win, dil = DIL_PAIRS[gi]
    blk = win // dil
    H = DIL_HEADS
    batch, _, sd, width = q.shape
    hd = width // H
    assert q.shape[1] == dil and sd % blk == 0
    nb = sd // blk
    slab = pl.BlockSpec((1, 1, blk, width), lambda b, r, n: (b, r, n, 0))
    in_specs = [slab, slab, slab]
    args = [q, k, v]
    out_specs = [slab, pl.BlockSpec((LSE_TILES, 1, 1, blk, LANES), lambda b, r, n: (0, b, r, n, 0))]
    out_shape = [jax.ShapeDtypeStruct(q.shape, BF16),
                 jax.ShapeDtypeStruct((LSE_TILES, batch, dil, sd, LANES), F32)]

    steps = batch * dil * nb
    caches = [c for c, _ in shift_jobs]
    news = [a for _, a in shift_jobs]
    ts = news[0].shape[1]
    Bc = caches[0].shape[0]
    assert steps == 2 * Bc
    jobs, slot_rows = [], [0, 0]
    for c in caches:
        keep = c.shape[1] - ts
        first = keep // 2 if keep * c.shape[2] * c.shape[3] * 4 > (2 << 20) else 0
        halves = ((0, first, False), (first, keep - first, True))
        jobs.append(halves)
        for h in range(2):
            slot_rows[h] += halves[h][1] + (ts if halves[h][2] else 0)
    any_spec = pl.BlockSpec(memory_space=pl.ANY)
    in_specs += [any_spec] * len(caches) + [pl.BlockSpec(memory_space=pltpu.VMEM)] * len(news)
    args += caches + news
    out_specs = out_specs + [any_spec] * len(caches)
    out_shape = out_shape + [jax.ShapeDtypeStruct(c.shape, c.dtype) for c in caches]

    res = pl.pallas_call(
        functools.partial(_dilated_prompt_kernel, heads=H, hd=hd, jobs=tuple(jobs), ts=ts, steps=steps,
                          tails_only=tails_only),
        grid=(batch, dil, nb),
        in_specs=in_specs,
        out_specs=out_specs,
        out_shape=out_shape,
        scratch_shapes=[pltpu.VMEM((blk, width), BF16), pltpu.VMEM((blk, width), BF16),
                        pltpu.VMEM((2, len(caches) * ts if tails_only else max(slot_rows))
                                   + caches[0].shape[2:], caches[0].dtype),
                        pltpu.SemaphoreType.DMA((2,)), pltpu.SemaphoreType.DMA((2,))],
        input_output_aliases={3 + j: 2 + j for j in range(len(caches))} if tails_only else {},
        compiler_params=_cparams("arbitrary", "arbitrary", "arbitrary"),
        name=name,
    )(*args)
    return res[0], res[1], list(res[2:])


def _merge_kernel(o0_ref, o1_ref, o2_ref, l0_ref, l1_ref, l2_ref, init_ref, out_ref,
                  s0_ref, s1_ref, nat_ref, *, hd):
    del init_ref
    d1, d2 = o1_ref.shape[1], o2_ref.shape[1]
    rows = o2_ref.shape[2]
    rep = d2 // d1
    nh = o2_ref.shape[3] // hd
    for hh in range(nh):
        cs = slice(hh * hd, (hh + 1) * hd)
        s0_ref[hh] = o0_ref[:, cs].astype(F32)
        s1_ref[hh] = o1_ref[0, :, :, cs].astype(F32)
    for r in range(d2):
        l2 = l2_ref[0, 0, r]
        l1 = l1_ref[0, 0, r % d1, pl.ds(r // d1, rows, stride=rep), :]
        l0 = l0_ref[0, pl.ds(r, rows, stride=d2), :]
        mx = jnp.maximum(jnp.maximum(l0, l1), l2)
        e0 = jnp.exp(l0 - mx)
        e1 = jnp.exp(l1 - mx)
        e2 = jnp.exp(l2 - mx)
        tot = 1.0 / (e0 + e1 + e2)
        w0, w1, w2 = e0 * tot, e1 * tot, e2 * tot
        for hh in range(nh):
            ls = slice(hh, hh + 1)
            o2 = o2_ref[0, r, :, hh * hd:(hh + 1) * hd].astype(F32)
            o1 = s1_ref[hh, r % d1, pl.ds(r // d1, rows, stride=rep), :]
            o0 = s0_ref[hh, pl.ds(r, rows, stride=d2), :]
            nat_ref[hh, pl.ds(r, rows, stride=d2), :] = w0[:, ls] * o0 + w1[:, ls] * o1 + w2[:, ls] * o2
    for hh in range(nh):
        out_ref[:, hh * hd:(hh + 1) * hd] = nat_ref[hh].astype(out_ref.dtype)


def _merge(outs, lses, init, *, name):
    (_, d0), (_, d1), (_, d2) = DIL_PAIRS
    batch, _, sd2, width = outs[2].shape
    seq = sd2 * d2
    rows = DIL_PAIRS[2][0] // d2
    span = rows * d2
    assert d0 == 1 and d2 % d1 == 0 and seq % span == 0
    nsp = seq // span
    cw = width // LSE_TILES
    hd = width // DIL_HEADS
    o0 = outs[0].reshape(batch * seq, width)
    l0 = lses[0].reshape(LSE_TILES, batch * seq, LANES)
    in_specs = [
        pl.BlockSpec((span, cw), lambda t, c: (t, c)),
        pl.BlockSpec((1, d1, span // d1, cw), lambda t, c: (t // nsp, 0, t % nsp, c)),
        pl.BlockSpec((1, d2, rows, cw), lambda t, c: (t // nsp, 0, t % nsp, c)),
        pl.BlockSpec((1, span, LANES), lambda t, c: (c, t, 0)),
        pl.BlockSpec((1, 1, d1, span // d1, LANES), lambda t, c: (c, t // nsp, 0, t % nsp, 0)),
        pl.BlockSpec((1, 1, d2, rows, LANES), lambda t, c: (c, t // nsp, 0, t % nsp, 0)),
        pl.BlockSpec(memory_space=pl.ANY),
    ]
    return pl.pallas_call(
        functools.partial(_merge_kernel, hd=hd),
        grid=(batch * nsp, LSE_TILES),
        in_specs=in_specs,
        out_specs=pl.BlockSpec((span, cw), lambda t, c: (t, c)),
        out_shape=jax.ShapeDtypeStruct(init.shape, init.dtype),
        scratch_shapes=[pltpu.VMEM((cw // hd, span, hd), F32),
                        pltpu.VMEM((cw // hd, d1, span // d1, hd), F32),
                        pltpu.VMEM((cw // hd, span, hd), F32)],
        input_output_aliases={6: 0},
        compiler_params=_cparams("parallel", "parallel"),
        name=name,
    )(o0, outs[1], outs[2], l0, lses[1], lses[2], init)


def _dilated_sample_kernel(q_ref, kn_ref, vn_ref, k0_ref, v0_ref, k1_ref, v1_ref, k2_ref, v2_ref,
                           o_ref, *, ts, hd):
    caches = ((k0_ref, v0_ref), (k1_ref, v1_ref), (k2_ref, v2_ref))
    scale = hd ** -0.5
    for t in range(ts):
        outs, lses = [], []
        for gi, (win, dil) in enumerate(DIL_PAIRS):
            kc_ref, vc_ref = caches[gi]
            q = q_ref[0, t, gi]
            r = t % dil if dil > 1 else 0
            kc = kc_ref[0, :, r]
            vc = vc_ref[0, :, r]
            s_c = jnp.sum(kc * q[None], axis=-1, keepdims=True) * scale
            if dil == 1:
                cidx = lax.broadcasted_iota(jnp.int32, s_c.shape, 0)
                s_c = jnp.where(cidx >= t, s_c, NEG)
                new_rows = list(range(t + 1))
            else:
                new_rows = [t]
            s_n = [jnp.sum(kn_ref[0, u, gi] * q, axis=-1, keepdims=True) * scale for u in new_rows]
            m = jnp.max(s_c, axis=0)
            for sn in s_n:
                m = jnp.maximum(m, sn)
            p_c = jnp.exp(s_c - m[None])
            p_n = [jnp.exp(sn - m) for sn in s_n]
            den = jnp.sum(p_c, axis=0)
            for pn in p_n:
                den = den + pn
            inv = 1.0 / den
            o = jnp.sum((p_c * inv[None]) * vc, axis=0)
            for u, pn in zip(new_rows, p_n):
                o = o + (pn * inv) * vn_ref[0, u, gi]
            outs.append(o)
            lses.append(m + jnp.log(den))
        mx = jnp.maximum(jnp.maximum(lses[0], lses[1]), lses[2])
        es = [jnp.exp(l - mx) for l in lses]
        tot = 1.0 / (es[0] + es[1] + es[2])
        o_ref[0, t] = (es[0] * tot) * outs[0] + (es[1] * tot) * outs[1] + (es[2] * tot) * outs[2]


def _dilated_sample(q, knew, vnew, caches_k, caches_v, *, name):
    B, ts, G, H, hd = q.shape
    small = pl.BlockSpec((1, ts, G, H, hd), lambda b: (b, 0, 0, 0, 0))
    in_specs = [small, small, small]
    args = [q, knew, vnew]
    for gi, (win, dil) in enumerate(DIL_PAIRS):
        L = caches_k[gi].shape[1]
        assert L == win and L % dil == 0 and (dil == 1 or ts <= dil)
        nres = min(dil, ts)
        spec = pl.BlockSpec((1, L // dil, nres, H, hd), lambda b: (b, 0, 0, 0, 0))
        for c in (caches_k[gi], caches_v[gi]):
            in_specs.append(spec)
            args.append(c.reshape(B, L // dil, dil, H, hd))
    return pl.pallas_call(
        functools.partial(_dilated_sample_kernel, ts=ts, hd=hd),
        grid=(B,),
        in_specs=in_specs,
        out_specs=pl.BlockSpec((1, ts, H, hd), lambda b: (b, 0, 0, 0)),
        out_shape=jax.ShapeDtypeStruct((B, ts, H, hd), F32),
        compiler_params=_cparams("parallel"),
        name=name,
    )(*args)


def kernel(x_prompt, x_sample, state_ret, cache_k_w128, cache_v_w128, cache_k_w512, cache_v_w512,
           cache_k_w2048, cache_v_w2048, norm_mix, norm_ffn, ret_w_in, ret_gn, ret_w_out,
           kv_norm, w_kv, dil_w_q, dil_w_o, ffn_w1, ffn_w3, ffn_w2, norm_final):
    Bp, Tp, D = x_prompt.shape
    Bs, Ts, _ = x_sample.shape
    H, hd = DIL_HEADS, D // DIL_HEADS
    caches_k = [cache_k_w128, cache_k_w512, cache_k_w2048]
    caches_v = [cache_v_w128, cache_v_w512, cache_v_w2048]
    MP, MS = Bp * Tp, Bs * SAMPLE_PAD
    MT = MP + MS
    NT = MT // 16

    h = jnp.concatenate([x_prompt.reshape(MP, D),
                         jnp.pad(x_sample, ((0, 0), (0, SAMPLE_PAD - Ts), (0, 0))).reshape(MS, D)])

    def ffn(h, layer):
        xn, = _norm(h, [norm_ffn[layer]], out_dtype=BF16, tm=NT, name=f"ffn_norm_{layer}")
        act = _linear(xn, [(ffn_w1, layer), (ffn_w3, layer)], swiglu=True, out_dtype=BF16, tn=512,
                      name=f"ffn_up_{layer}")
        return _linear(act, [(ffn_w2, layer)], residual=h, out_dtype=F32, tn=256, name=f"ffn_down_{layer}")

    chunk_p = RET_CHUNK if Tp % RET_CHUNK == 0 else Tp
    xn, = _norm(h, [norm_mix[0]], out_dtype=BF16, tm=NT, name="mix_norm_0")
    proj = _linear(xn, [ret_w_in[0]], out_dtype=BF16, tn=512, name="ret_in")
    s0_p = jnp.zeros((Bp,) + state_ret.shape[2:], F32)
    gated = jnp.zeros((MT, RET_HEADS * state_ret.shape[-1]), BF16)
    small = [caches_k[0], caches_v[0], caches_k[1], caches_v[1]]
    gated, sp, moved = _retention(proj, s0_p, ret_gn[0], row0=0, batch=Bp, seq=Tp, chunk=chunk_p,
                                  chunk_true=chunk_p, pos=jnp.arange(Tp), out_init=gated,
                                  move_buffers=small, move_ts=Ts, name="retention_p")
    gated, ss, _ = _retention(proj, state_ret[0], ret_gn[0], row0=MP, batch=Bs, seq=SAMPLE_PAD,
                              chunk=SAMPLE_PAD, chunk_true=Ts, pos=PAST_LEN + jnp.arange(SAMPLE_PAD),
                              out_init=gated, name="retention_s")
    h = _linear(gated, [ret_w_out[0]], residual=h, out_dtype=F32, tn=256, name="ret_out")
    h = ffn(h, 0)

    xkv, xq = _norm(h, [kv_norm, norm_mix[1]], out_dtype=BF16, tm=NT, name="kv_q_norm")
    qd, kd, vd, new_kp, new_vp, q_s, k_s, v_s = [], [], [], [], [], [], [], []
    for gi, (win, dil) in enumerate(DIL_PAIRS):
        assert Tp % win == 0
        common = dict(batch=Bp, seq=Tp, dil=dil, extra_rows=MS)
        qg, qs = _dilated_linear(xq, dil_w_q[0], gi * D, tn=1024, name=f"dil_q_{gi}", **common)
        kg, ks, kw = _dilated_linear(xkv, w_kv, 2 * gi * D, window=win, name=f"dil_k_{gi}", **common)
        vg, vs, vw = _dilated_linear(xkv, w_kv, (2 * gi + 1) * D, window=win, name=f"dil_v_{gi}", **common)
        qd.append(qg)
        kd.append(kg)
        vd.append(vg)
        new_kp.append(kw.reshape(Bp, win, H, hd))
        new_vp.append(vw.reshape(Bp, win, H, hd))
        for lst, a in ((q_s, qs), (k_s, ks), (v_s, vs)):
            lst.append(a.reshape(Bs, SAMPLE_PAD, H, hd)[:, :Ts])
    q_s5, knew, vnew = (jnp.stack(lst, axis=2) for lst in (q_s, k_s, v_s))

    job = lambda c, a, i: (c[i], a[:, :, i])
    o0, l0, (nk2,) = _dilated_prompt(qd[0], kd[0], vd[0], 0, shift_jobs=[job(caches_k, knew, 2)],
                                     name="dilated_p0")
    o1, l1, (nk0, nv0, nk1, nv1) = _dilated_prompt(
        qd[1], kd[1], vd[1], 1, name="dilated_p1", tails_only=True,
        shift_jobs=[(moved[0], knew[:, :, 0]), (moved[1], vnew[:, :, 0]),
                    (moved[2], knew[:, :, 1]), (moved[3], vnew[:, :, 1])])
    o2, l2, (nv2,) = _dilated_prompt(qd[2], kd[2], vd[2], 2, shift_jobs=[job(caches_v, vnew, 2)],
                                     name="dilated_p2")
    a_s = _dilated_sample(q_s5, knew, vnew, caches_k, caches_v, name="dilated_s")
    a_s = jnp.pad(a_s.reshape(Bs, Ts, D), ((0, 0), (0, SAMPLE_PAD - Ts), (0, 0))).reshape(MS, D)
    a = jnp.concatenate([jnp.zeros((MP, D), BF16), a_s.astype(BF16)])
    a = _merge([o0, o1, o2], [l0, l1, l2], a, name="dilated_merge")
    h = _linear(a, [dil_w_o[0]], residual=h, out_dtype=F32, tn=512, name="dil_o")
    h = ffn(h, 1)

    y_p, = _norm(h, [norm_final], out_dtype=F32, tm=MS, rows=MP, name="final_norm_p")
    y_s, = _norm(h, [norm_final], out_dtype=F32, tm=MS, row0=MP, rows=MS, name="final_norm_s")
    return (y_p.reshape(Bp, Tp, D), y_s.reshape(Bs, SAMPLE_PAD, D)[:, :Ts], sp[None], ss[None],
            new_kp[0], new_vp[0], new_kp[1], new_vp[1], new_kp[2], new_vp[2],
            nk0, nv0, nk1, nv1, nk2, nv2)
```

```python
import functools

import jax
import jax.numpy as jnp
from jax import lax
from jax.experimental import pallas as pl
from jax.experimental.pallas import tpu as pltpu

F32 = jnp.float32
BF16 = jnp.bfloat16

RET_HEADS = 8
RET_CHUNK = 128
ROPE_BASE = 10000.0
DIL_PAIRS = ((128, 1), (512, 4), (2048, 16))
N_GROUPS = len(DIL_PAIRS)
DIL_HEADS = 16
PAST_LEN = 8192
EPS = 1e-6
NEG = -1e30

LANES = 128
SAMPLE_PAD = 16
VMEM_LIMIT = 61 * 1024 * 1024
ROW_TILES = 4


def _cparams(*sem):
    return pltpu.CompilerParams(dimension_semantics=sem, vmem_limit_bytes=VMEM_LIMIT)


def _norm_kernel(x_ref, g_ref, *o_refs):
    xf = x_ref[...]
    ms = jnp.mean(xf * xf, axis=-1, keepdims=True)
    xs = xf * lax.rsqrt(ms + EPS)
    for k, o_ref in enumerate(o_refs):
        o_ref[...] = (xs * g_ref[k:k + 1, :]).astype(o_ref.dtype)


def _norm(x, gains, *, out_dtype, tm, row0=0, rows=None, name):
    M, K = x.shape
    rows = M if rows is None else rows
    assert rows % tm == 0 and row0 % tm == 0
    g = jnp.stack(gains).astype(F32)
    off = row0 // tm
    outs = pl.pallas_call(
        _norm_kernel,
        grid=(rows // tm,),
        in_specs=[pl.BlockSpec((tm, K), lambda i: (i + off, 0)),
                  pl.BlockSpec((len(gains), K), lambda i: (0, 0))],
        out_specs=[pl.BlockSpec((tm, K), lambda i: (i, 0))] * len(gains),
        out_shape=[jax.ShapeDtypeStruct((rows, K), out_dtype)] * len(gains),
        compiler_params=_cparams("parallel"),
        name=name,
    )(x, g)
    return outs


def _linear_kernel(*refs, n_w, has_res, swiglu):
    it = iter(refs)
    x_ref = next(it)
    w_refs = [next(it) for _ in range(n_w)]
    r_ref = next(it) if has_res else None
    o_ref = next(it)
    xn = x_ref[...]
    a = jnp.dot(xn, w_refs[0][...].astype(BF16), preferred_element_type=F32)
    if swiglu:
        b = jnp.dot(xn, w_refs[1][...].astype(BF16), preferred_element_type=F32)
        a = (a * jax.nn.sigmoid(a)) * b
    if has_res:
        a = a + r_ref[...]
    o_ref[...] = a.astype(o_ref.dtype)


def _linear(x, ws, *, residual=None, swiglu=False, out_dtype, tn, row0=0, rows=None,
            row_tiles=ROW_TILES, x_buffers=2, name):
    K = x.shape[1]
    M = x.shape[0] if rows is None else rows
    ws = [w if isinstance(w, tuple) else (w[None], 0) for w in ws]
    N = ws[0][0].shape[2]
    assert M % (row_tiles * 16) == 0 and N % tn == 0 and x.dtype == BF16
    tm = M // row_tiles
    assert row0 % tm == 0 and (residual is None or row0 == 0)
    off = row0 // tm
    has_res = residual is not None
    in_specs = [pl.BlockSpec((tm, K), lambda i, j: (i + off, 0), pipeline_mode=pl.Buffered(x_buffers))]
    args = [x]
    for w, layer in ws:
        in_specs.append(pl.BlockSpec((None, K, tn), lambda i, j, layer=layer: (layer, 0, j)))
        args.append(w)
    if has_res:
        in_specs.append(pl.BlockSpec((tm, tn), lambda i, j: (i, j)))
        args.append(residual)
    kern = functools.partial(_linear_kernel, n_w=len(ws), has_res=has_res, swiglu=swiglu)
    return pl.pallas_call(
        kern,
        grid=(row_tiles, N // tn),
        in_specs=in_specs,
        out_specs=pl.BlockSpec((tm, tn), lambda i, j: (i, j)),
        out_shape=jax.ShapeDtypeStruct((M, N), out_dtype),
        compiler_params=_cparams("parallel", "arbitrary"),
        name=name,
    )(*args)


MXU_COLS = 256


def _dilated_linear_kernel(x_ref, xs_ref, w_ref, *refs, dil, win_rows, row_parts):
    it = iter(refs)
    o_ref = next(it)
    os_ref = next(it)
    win_ref = next(it) if win_rows else None
    acc_ref = next(it) if dil > 1 else None
    tm, tn = x_ref.shape[0], w_ref.shape[1]
    rp = tm // row_parts
    w0 = tm - win_rows
    for c in range(tn // MXU_COLS):
        cs = slice(c * MXU_COLS, (c + 1) * MXU_COLS)
        wb = w_ref[:, cs].astype(BF16)
        for p in range(row_parts):
            lo = p * rp
            acc = jnp.dot(x_ref[lo:lo + rp, :], wb, preferred_element_type=F32)
            if win_rows and lo + rp > w0:
                a = max(w0, lo)
                win_ref[0, a - w0:lo + rp - w0, cs] = acc[a - lo:, :]
            if dil == 1:
                o_ref[0, 0, lo:lo + rp, cs] = acc.astype(o_ref.dtype)
            else:
                n = rp // dil
                for s in range(c * MXU_COLS // LANES, (c + 1) * MXU_COLS // LANES):
                    ls = slice(s * LANES, (s + 1) * LANES)
                    acc_ref[s, lo:lo + rp, :] = acc[:, ls.start - c * MXU_COLS:ls.stop - c * MXU_COLS]
                    for r in range(dil):
                        o_ref[0, r, p * n:(p + 1) * n, ls] = (
                            acc_ref[s, pl.ds(lo + r, n, stride=dil), :].astype(o_ref.dtype))

    @pl.when(pl.program_id(0) == 0)
    def _():
        os_ref[...] = jnp.dot(xs_ref[...], w_ref[...].astype(BF16), preferred_element_type=F32)


def _dilated_linear(x, w, col0, *, batch, seq, dil, extra_rows, window=None, tm=2048, tn=512, name):
    K = x.shape[1]
    width = DIL_HEADS * (K // DIL_HEADS)
    row_parts = 2
    assert seq % tm == 0 and tm % (row_parts * dil * 16) == 0 and col0 % tn == 0 and width % tn == 0
    assert (batch * seq) % extra_rows == 0
    tpb = seq // tm
    cb0 = col0 // tn
    ncol = width // tn
    xs_block = (batch * seq) // extra_rows
    in_specs = [pl.BlockSpec((tm, K), lambda i, j: (i, 0)),
                pl.BlockSpec((extra_rows, K), lambda i, j: (xs_block, 0)),
                pl.BlockSpec((K, tn), lambda i, j: (0, cb0 + j))]
    out_specs = [pl.BlockSpec((1, dil, tm // dil, tn), lambda i, j: (i // tpb, 0, i % tpb, j)),
                 pl.BlockSpec((extra_rows, tn), lambda i, j: (0, jnp.where(i == 0, j, ncol - 1)))]
    out_shape = [jax.ShapeDtypeStruct((batch, dil, seq // dil, width), BF16),
                 jax.ShapeDtypeStruct((extra_rows, width), F32)]
    win_rows = 0
    if window is not None:
        win_rows = min(window, tm)
        assert window % win_rows == 0
        t0 = tpb - window // win_rows

        def win_map(i, j):
            t = i % tpb
            inside = t >= t0
            return (i // tpb, jnp.where(inside, t - t0, 0), jnp.where(inside, j, 0))

        out_specs.append(pl.BlockSpec((1, win_rows, tn), win_map))
        out_shape.append(jax.ShapeDtypeStruct((batch, window, width), F32))
    scratch = [pltpu.VMEM((tn // LANES, tm, LANES), F32)] if dil > 1 else []
    return pl.pallas_call(
        functools.partial(_dilated_linear_kernel, dil=dil, win_rows=win_rows, row_parts=row_parts),
        grid=(batch * tpb, ncol),
        in_specs=in_specs,
        out_specs=out_specs,
        out_shape=out_shape,
        scratch_shapes=scratch,
        compiler_params=_cparams("arbitrary", "arbitrary"),
        name=name,
    )(x, x, w)


def _retention_kernel(*refs, dk, dv, hps, chunk, nc):
    (q_ref, k_ref, v_ref, g_ref, cos_ref, sin_ref, dmat_ref, qdec_ref, kdec_ref,
     cdec_ref, gn_ref, s0_ref) = refs[:12]
    o_ref, s_ref = refs[-2:]

    s_ref[...] = s0_ref[...]
    half = dk // 2

    def one_chunk(c):
        rows = slice(None) if nc == 1 else pl.ds(pl.multiple_of(c * chunk, chunk), chunk)
        cos = cos_ref[rows, :]
        sin = sin_ref[rows, :]

        def rope(x):
            x1 = x[:, :half]
            x2 = x[:, half:]
            return jnp.concatenate([x1 * cos - x2 * sin, x1 * sin + x2 * cos], axis=-1)

        for j in range(hps):
            q = rope(q_ref[rows, j * dk:(j + 1) * dk].astype(F32))
            k = rope(k_ref[rows, j * dk:(j + 1) * dk].astype(F32)) * (dk ** -0.5)
            v = v_ref[rows, j * dv:(j + 1) * dv]
            s = s_ref[0, j]
            sc = lax.dot_general(q.astype(BF16), k.astype(BF16), (((1,), (1,)), ((), ())),
                                 preferred_element_type=F32) * dmat_ref[j]
            o = (jnp.dot(sc.astype(BF16), v, preferred_element_type=F32)
                 + jnp.dot((q * qdec_ref[j]).astype(BF16), s.astype(BF16), preferred_element_type=F32))
            kd = (k * kdec_ref[j]).astype(BF16)
            s_ref[0, j] = s * cdec_ref[j] + lax.dot_general(kd, v, (((0,), (0,)), ((), ())),
                                                           preferred_element_type=F32)
            mu = jnp.mean(o, axis=-1, keepdims=True)
            var = jnp.mean(jnp.square(o - mu), axis=-1, keepdims=True)
            on = (o - mu) * lax.rsqrt(var + EPS) * gn_ref[:, j * dv:(j + 1) * dv]
            gate = g_ref[rows, j * dv:(j + 1) * dv].astype(F32)
            o_ref[rows, j * dv:(j + 1) * dv] = ((gate * jax.nn.sigmoid(gate)) * on).astype(o_ref.dtype)

    if nc == 1:
        one_chunk(0)
    else:
        def body(c, carry):
            one_chunk(c)
            return carry

        lax.fori_loop(0, nc, body, 0, unroll=4 if nc % 4 == 0 else 1)


def _retention(proj, s0, gn_g, *, row0, batch, seq, chunk, chunk_true, pos, out_init=None, name):
    H = RET_HEADS
    dk, dv = s0.shape[-2], s0.shape[-1]
    nc = seq // chunk
    assert row0 % chunk == 0
    lg = jnp.log1p(-jnp.exp2(-5.0 - jnp.arange(H, dtype=F32)))
    i = jnp.arange(chunk, dtype=F32)
    dist = i[:, None] - i[None, :]
    dmat = jnp.where(dist >= 0, jnp.exp(jnp.maximum(dist, 0.0)[None] * lg[:, None, None]), 0.0)
    qdec = jnp.exp((i + 1.0)[None, :, None] * lg[:, None, None])
    kdec = jnp.exp((chunk_true - 1.0 - i)[None, :, None] * lg[:, None, None])
    cdec = jnp.exp(chunk_true * lg).reshape(H, 1, 1)
    half = dk // 2
    inv = ROPE_BASE ** (-jnp.arange(half, dtype=F32) / half)
    ang = pos.astype(F32)[:, None] * inv[None, :]
    cos, sin = jnp.cos(ang), jnp.sin(ang)

    hps = H if seq * H * (2 * dk + 2 * dv) * 2 <= (1 << 20) else 1
    hg = H // hps
    kb, vb = hg, (2 * H * dk) // (hps * dv)
    gb = vb + hg
    assert row0 % seq == 0
    rb = row0 // seq
    in_specs = [
        pl.BlockSpec((seq, hps * dk), lambda b, h: (rb + b, h)),
        pl.BlockSpec((seq, hps * dk), lambda b, h: (rb + b, kb + h)),
        pl.BlockSpec((seq, hps * dv), lambda b, h: (rb + b, vb + h)),
        pl.BlockSpec((seq, hps * dv), lambda b, h: (rb + b, gb + h)),
        pl.BlockSpec((seq, half), lambda b, h: (0, 0)),
        pl.BlockSpec((seq, half), lambda b, h: (0, 0)),
        pl.BlockSpec((hps, chunk, chunk), lambda b, h: (h, 0, 0)),
        pl.BlockSpec((hps, chunk, 1), lambda b, h: (h, 0, 0)),
        pl.BlockSpec((hps, chunk, 1), lambda b, h: (h, 0, 0)),
        pl.BlockSpec((hps, 1, 1), lambda b, h: (h, 0, 0)),
        pl.BlockSpec((1, hps * dv), lambda b, h: (0, h)),
        pl.BlockSpec((1, hps, dk, dv), lambda b, h: (b, h, 0, 0)),
    ]
    args = [proj, proj, proj, proj, cos, sin, dmat, qdec, kdec, cdec, gn_g.reshape(1, H * dv), s0]
    aliases = {}
    if out_init is not None:
        in_specs.append(pl.BlockSpec(memory_space=pl.ANY))
        args.append(out_init)
        aliases = {len(args) - 1: 0}
    out_specs = [
        pl.BlockSpec((seq, hps * dv), lambda b, h: (rb + b, h)),
        pl.BlockSpec((1, hps, dk, dv), lambda b, h: (b, h, 0, 0)),
    ]
    out_shape = [jax.ShapeDtypeStruct((proj.shape[0], H * dv), BF16),
                 jax.ShapeDtypeStruct((batch, H, dk, dv), F32)]
    return pl.pallas_call(
        functools.partial(_retention_kernel, dk=dk, dv=dv, hps=hps, chunk=chunk, nc=nc),
        grid=(batch, hg),
        in_specs=in_specs,
        out_specs=out_specs,
        out_shape=out_shape,
        input_output_aliases=aliases,
        compiler_params=_cparams("parallel", "parallel"),
        name=name,
    )(*args)


def _dilated_prompt_kernel(*refs, heads, hd, jobs, ts, steps):
    q_ref, k_ref, v_ref = refs[:3]
    nj = len(jobs)
    caches, news = refs[3:3 + nj], refs[3 + nj:3 + 2 * nj]
    o_ref, l_ref = refs[3 + 2 * nj:5 + 2 * nj]
    shifted = refs[5 + 2 * nj:5 + 3 * nj]
    kp_ref, vp_ref, buf, in_sem, out_sem = refs[5 + 3 * nj:]

    n = pl.program_id(2)
    blk = q_ref.shape[2]
    step = (pl.program_id(0) * pl.num_programs(1) + pl.program_id(1)) * pl.num_programs(2) + n
    last = steps - 1

    def plan(h):
        out, off = [], 0
        for j, (halves) in enumerate(jobs):
            r0, nr, tail = halves[h]
            parts = COPY_STREAMS if nr >= COPY_STREAMS * 64 else 1
            for p in range(parts):
                lo, hi = nr * p // parts, nr * (p + 1) // parts
                if hi > lo:
                    has_tail = tail and p == parts - 1
                    out.append((j, r0 + lo, hi - lo, off, has_tail))
                    off += hi - lo + (ts if has_tail else 0)
        return out

    def in_copies(h, bc):
        return [pltpu.make_async_copy(caches[j].at[bc, pl.ds(ts + r0, nr)],
                                      buf.at[h, pl.ds(off, nr)], in_sem.at[h])
                for (j, r0, nr, off, tail) in plan(h)]

    def out_copies(h, bc):
        return [pltpu.make_async_copy(buf.at[h, pl.ds(off, nr + (ts if tail else 0))],
                                      shifted[j].at[bc, pl.ds(r0, nr + (ts if tail else 0))],
                                      out_sem.at[h])
                for (j, r0, nr, off, tail) in plan(h)]

    @pl.when(step == 0)
    def _():
        for cp in in_copies(0, 0):
            cp.start()

    for h in range(2):
        @pl.when(step % 2 == h)
        def _(h=h):
            bc = step // 2
            for cp in in_copies(h, bc):
                cp.wait()
            for (j, r0, nr, off, tail) in plan(h):
                if tail:
                    buf[h, pl.ds(off + nr, ts)] = news[j][bc]
            for cp in out_copies(h, bc):
                cp.start()

            @pl.when(step > 0)
            def _():
                for cp in out_copies(1 - h, (step - 1) // 2):
                    cp.wait()

            @pl.when(step < last)
            def _():
                for cp in in_copies(1 - h, (step + 1) // 2):
                    cp.start()

    @pl.when(n == 0)
    def _():
        kp_ref[...] = jnp.zeros_like(kp_ref)
        vp_ref[...] = jnp.zeros_like(vp_ref)

    row = lax.broadcasted_iota(jnp.int32, (blk, blk), 0)
    col = lax.broadcasted_iota(jnp.int32, (blk, blk), 1)
    mask_c = col <= row
    mask_p = (col - row) >= jnp.where(n > 0, 0, blk)
    lane = lax.broadcasted_iota(jnp.int32, (blk, LANES), 1)
    scale = hd ** -0.5
    nt = (((1,), (1,)), ((), ()))
    hpt = heads // l_ref.shape[0]

    def qk(cs, key_ref, mask):
        s = lax.dot_general(q_ref[0, 0, :, cs], key_ref[:, cs], nt, preferred_element_type=F32)
        return jnp.where(mask, s * scale, NEG)

    kc_ref, vc_ref = k_ref.at[0, 0], v_ref.at[0, 0]
    for g0 in range(0, heads, hpt):
        cols = [slice(h * hd, (h + 1) * hd) for h in range(g0, g0 + hpt)]
        s_c = [qk(cs, kc_ref, mask_c) for cs in cols]
        s_p = [qk(cs, kp_ref, mask_p) for cs in cols]
        m = [jnp.maximum(jnp.max(a, axis=-1, keepdims=True), jnp.max(b, axis=-1, keepdims=True))
             for a, b in zip(s_c, s_p)]
        p_c = [jnp.exp(a - mm) for a, mm in zip(s_c, m)]
        p_p = [jnp.exp(b - mm) for b, mm in zip(s_p, m)]
        den = [jnp.sum(a, axis=-1, keepdims=True) + jnp.sum(b, axis=-1, keepdims=True)
               for a, b in zip(p_c, p_p)]
        inv = [1.0 / d for d in den]
        lse_tile = jnp.zeros((blk, LANES), F32)
        for i, cs in enumerate(cols):
            o_ref[0, 0, :, cs] = (
                jnp.dot((p_c[i] * inv[i]).astype(BF16), vc_ref[:, cs], preferred_element_type=F32)
                + jnp.dot((p_p[i] * inv[i]).astype(BF16), vp_ref[:, cs], preferred_element_type=F32)
            ).astype(o_ref.dtype)
            lse_tile = jnp.where(lane == i, m[i] + jnp.log(den[i]), lse_tile)
        l_ref[g0 // hpt, 0, 0] = lse_tile

    kp_ref[...] = k_ref[0, 0]
    vp_ref[...] = v_ref[0, 0]

    @pl.when(step == last)
    def _():
        for cp in out_copies(last % 2, last // 2):
            cp.wait()


LSE_TILES = 4
COPY_STREAMS = 4


def _dilated_prompt(q, k, v, gi, *, shift_jobs, name):
    win, dil = DIL_PAIRS[gi]
    blk = win // dil
    H = DIL_HEADS
    batch, _, sd, width = q.shape
    hd = width // H
    assert q.shape[1] == dil and sd % blk == 0
    nb = sd // blk
    slab = pl.BlockSpec((1, 1, blk, width), lambda b, r, n: (b, r, n, 0))
    in_specs = [slab, slab, slab]
    args = [q, k, v]
    out_specs = [slab, pl.BlockSpec((LSE_TILES, 1, 1, blk, LANES), lambda b, r, n: (0, b, r, n, 0))]
    out_shape = [jax.ShapeDtypeStruct(q.shape, BF16),
                 jax.ShapeDtypeStruct((LSE_TILES, batch, dil, sd, LANES), F32)]

    steps = batch * dil * nb
    caches = [c for c, _ in shift_jobs]
    news = [a for _, a in shift_jobs]
    ts = news[0].shape[1]
    Bc = caches[0].shape[0]
    assert steps == 2 * Bc
    jobs, slot_rows = [], [0, 0]
    for c in caches:
        keep = c.shape[1] - ts
        first = keep // 2 if keep * c.shape[2] * c.shape[3] * 4 > (2 << 20) else 0
        halves = ((0, first, False), (first, keep - first, True))
        jobs.append(halves)
        for h in range(2):
            slot_rows[h] += halves[h][1] + (ts if halves[h][2] else 0)
    any_spec = pl.BlockSpec(memory_space=pl.ANY)
    in_specs += [any_spec] * len(caches) + [pl.BlockSpec(memory_space=pltpu.VMEM)] * len(news)
    args += caches + news
    out_specs = out_specs + [any_spec] * len(caches)
    out_shape = out_shape + [jax.ShapeDtypeStruct(c.shape, c.dtype) for c in caches]

    res = pl.pallas_call(
        functools.partial(_dilated_prompt_kernel, heads=H, hd=hd, jobs=tuple(jobs), ts=ts, steps=steps),
        grid=(batch, dil, nb),
        in_specs=in_specs,
        out_specs=out_specs,
        out_shape=out_shape,
        scratch_shapes=[pltpu.VMEM((blk, width), BF16), pltpu.VMEM((blk, width), BF16),
                        pltpu.VMEM((2, max(slot_rows)) + caches[0].shape[2:], caches[0].dtype),
                        pltpu.SemaphoreType.DMA((2,)), pltpu.SemaphoreType.DMA((2,))],
        compiler_params=_cparams("arbitrary", "arbitrary", "arbitrary"),
        name=name,
    )(*args)
    return res[0], res[1], list(res[2:])


def _merge_kernel(o0_ref, o1_ref, o2_ref, l0_ref, l1_ref, l2_ref, init_ref, out_ref,
                  s0_ref, s1_ref, nat_ref, *, hd):
    del init_ref
    d1, d2 = o1_ref.shape[1], o2_ref.shape[1]
    rows = o2_ref.shape[2]
    rep = d2 // d1
    nh = o2_ref.shape[3] // hd
    for hh in range(nh):
        cs = slice(hh * hd, (hh + 1) * hd)
        s0_ref[hh] = o0_ref[:, cs].astype(F32)
        s1_ref[hh] = o1_ref[0, :, :, cs].astype(F32)
    for r in range(d2):
        l2 = l2_ref[0, 0, r]
        l1 = l1_ref[0, 0, r % d1, pl.ds(r // d1, rows, stride=rep), :]
        l0 = l0_ref[0, pl.ds(r, rows, stride=d2), :]
        mx = jnp.maximum(jnp.maximum(l0, l1), l2)
        e0 = jnp.exp(l0 - mx)
        e1 = jnp.exp(l1 - mx)
        e2 = jnp.exp(l2 - mx)
        tot = 1.0 / (e0 + e1 + e2)
        w0, w1, w2 = e0 * tot, e1 * tot, e2 * tot
        for hh in range(nh):
            ls = slice(hh, hh + 1)
            o2 = o2_ref[0, r, :, hh * hd:(hh + 1) * hd].astype(F32)
            o1 = s1_ref[hh, r % d1, pl.ds(r // d1, rows, stride=rep), :]
            o0 = s0_ref[hh, pl.ds(r, rows, stride=d2), :]
            nat_ref[hh, pl.ds(r, rows, stride=d2), :] = w0[:, ls] * o0 + w1[:, ls] * o1 + w2[:, ls] * o2
    for hh in range(nh):
        out_ref[:, hh * hd:(hh + 1) * hd] = nat_ref[hh].astype(out_ref.dtype)


def _merge(outs, lses, init, *, name):
    (_, d0), (_, d1), (_, d2) = DIL_PAIRS
    batch, _, sd2, width = outs[2].shape
    seq = sd2 * d2
    rows = DIL_PAIRS[2][0] // d2
    span = rows * d2
    assert d0 == 1 and d2 % d1 == 0 and seq % span == 0
    nsp = seq // span
    cw = width // LSE_TILES
    hd = width // DIL_HEADS
    o0 = outs[0].reshape(batch * seq, width)
    l0 = lses[0].reshape(LSE_TILES, batch * seq, LANES)
    in_specs = [
        pl.BlockSpec((span, cw), lambda t, c: (t, c)),
        pl.BlockSpec((1, d1, span // d1, cw), lambda t, c: (t // nsp, 0, t % nsp, c)),
        pl.BlockSpec((1, d2, rows, cw), lambda t, c: (t // nsp, 0, t % nsp, c)),
        pl.BlockSpec((1, span, LANES), lambda t, c: (c, t, 0)),
        pl.BlockSpec((1, 1, d1, span // d1, LANES), lambda t, c: (c, t // nsp, 0, t % nsp, 0)),
        pl.BlockSpec((1, 1, d2, rows, LANES), lambda t, c: (c, t // nsp, 0, t % nsp, 0)),
        pl.BlockSpec(memory_space=pl.ANY),
    ]
    return pl.pallas_call(
        functools.partial(_merge_kernel, hd=hd),
        grid=(batch * nsp, LSE_TILES),
        in_specs=in_specs,
        out_specs=pl.BlockSpec((span, cw), lambda t, c: (t, c)),
        out_shape=jax.ShapeDtypeStruct(init.shape, init.dtype),
        scratch_shapes=[pltpu.VMEM((cw // hd, span, hd), F32),
                        pltpu.VMEM((cw // hd, d1, span // d1, hd), F32),
                        pltpu.VMEM((cw // hd, span, hd), F32)],
        input_output_aliases={6: 0},
        compiler_params=_cparams("parallel", "parallel"),
        name=name,
    )(o0, outs[1], outs[2], l0, lses[1], lses[2], init)


def _dilated_sample_kernel(q_ref, kn_ref, vn_ref, k0_ref, v0_ref, k1_ref, v1_ref, k2_ref, v2_ref,
                           o_ref, *, ts, hd):
    caches = ((k0_ref, v0_ref), (k1_ref, v1_ref), (k2_ref, v2_ref))
    scale = hd ** -0.5
    for t in range(ts):
        outs, lses = [], []
        for gi, (win, dil) in enumerate(DIL_PAIRS):
            kc_ref, vc_ref = caches[gi]
            q = q_ref[0, t, gi]
            r = t % dil if dil > 1 else 0
            kc = kc_ref[0, :, r]
            vc = vc_ref[0, :, r]
            s_c = jnp.sum(kc * q[None], axis=-1, keepdims=True) * scale
            if dil == 1:
                cidx = lax.broadcasted_iota(jnp.int32, s_c.shape, 0)
                s_c = jnp.where(cidx >= t, s_c, NEG)
                new_rows = list(range(t + 1))
            else:
                new_rows = [t]
            s_n = [jnp.sum(kn_ref[0, u, gi] * q, axis=-1, keepdims=True) * scale for u in new_rows]
            m = jnp.max(s_c, axis=0)
            for sn in s_n:
                m = jnp.maximum(m, sn)
            p_c = jnp.exp(s_c - m[None])
            p_n = [jnp.exp(sn - m) for sn in s_n]
            den = jnp.sum(p_c, axis=0)
            for pn in p_n:
                den = den + pn
            inv = 1.0 / den
            o = jnp.sum((p_c * inv[None]) * vc, axis=0)
            for u, pn in zip(new_rows, p_n):
                o = o + (pn * inv) * vn_ref[0, u, gi]
            outs.append(o)
            lses.append(m + jnp.log(den))
        mx = jnp.maximum(jnp.maximum(lses[0], lses[1]), lses[2])
        es = [jnp.exp(l - mx) for l in lses]
        tot = 1.0 / (es[0] + es[1] + es[2])
        o_ref[0, t] = (es[0] * tot) * outs[0] + (es[1] * tot) * outs[1] + (es[2] * tot) * outs[2]


def _dilated_sample(q, knew, vnew, caches_k, caches_v, *, name):
    B, ts, G, H, hd = q.shape
    small = pl.BlockSpec((1, ts, G, H, hd), lambda b: (b, 0, 0, 0, 0))
    in_specs = [small, small, small]
    args = [q, knew, vnew]
    for gi, (win, dil) in enumerate(DIL_PAIRS):
        L = caches_k[gi].shape[1]
        assert L == win and L % dil == 0 and (dil == 1 or ts <= dil)
        nres = min(dil, ts)
        spec = pl.BlockSpec((1, L // dil, nres, H, hd), lambda b: (b, 0, 0, 0, 0))
        for c in (caches_k[gi], caches_v[gi]):
            in_specs.append(spec)
            args.append(c.reshape(B, L // dil, dil, H, hd))
    return pl.pallas_call(
        functools.partial(_dilated_sample_kernel, ts=ts, hd=hd),
        grid=(B,),
        in_specs=in_specs,
        out_specs=pl.BlockSpec((1, ts, H, hd), lambda b: (b, 0, 0, 0)),
        out_shape=jax.ShapeDtypeStruct((B, ts, H, hd), F32),
        compiler_params=_cparams("parallel"),
        name=name,
    )(*args)


def kernel(x_prompt, x_sample, state_ret, cache_k_w128, cache_v_w128, cache_k_w512, cache_v_w512,
           cache_k_w2048, cache_v_w2048, norm_mix, norm_ffn, ret_w_in, ret_gn, ret_w_out,
           kv_norm, w_kv, dil_w_q, dil_w_o, ffn_w1, ffn_w3, ffn_w2, norm_final):
    Bp, Tp, D = x_prompt.shape
    Bs, Ts, _ = x_sample.shape
    H, hd = DIL_HEADS, D // DIL_HEADS
    caches_k = [cache_k_w128, cache_k_w512, cache_k_w2048]
    caches_v = [cache_v_w128, cache_v_w512, cache_v_w2048]
    MP, MS = Bp * Tp, Bs * SAMPLE_PAD
    MT = MP + MS
    NT = MT // 16

    h = jnp.concatenate([x_prompt.reshape(MP, D),
                         jnp.pad(x_sample, ((0, 0), (0, SAMPLE_PAD - Ts), (0, 0))).reshape(MS, D)])

    def ffn(h, layer):
        xn, = _norm(h, [norm_ffn[layer]], out_dtype=BF16, tm=NT, name=f"ffn_norm_{layer}")
        act = _linear(xn, [(ffn_w1, layer), (ffn_w3, layer)], swiglu=True, out_dtype=BF16, tn=512,
                      name=f"ffn_up_{layer}")
        return _linear(act, [(ffn_w2, layer)], residual=h, out_dtype=F32, tn=256, x_buffers=1,
                       name=f"ffn_down_{layer}")

    chunk_p = RET_CHUNK if Tp % RET_CHUNK == 0 else Tp
    xn, = _norm(h, [norm_mix[0]], out_dtype=BF16, tm=NT, name="mix_norm_0")
    proj = _linear(xn, [ret_w_in[0]], out_dtype=BF16, tn=512, name="ret_in")
    s0_p = jnp.zeros((Bp,) + state_ret.shape[2:], F32)
    gated = jnp.zeros((MT, RET_HEADS * state_ret.shape[-1]), BF16)
    gated, sp = _retention(proj, s0_p, ret_gn[0], row0=0, batch=Bp, seq=Tp, chunk=chunk_p,
                           chunk_true=chunk_p, pos=jnp.arange(Tp), out_init=gated, name="retention_p")
    gated, ss = _retention(proj, state_ret[0], ret_gn[0], row0=MP, batch=Bs, seq=SAMPLE_PAD,
                           chunk=SAMPLE_PAD, chunk_true=Ts, pos=PAST_LEN + jnp.arange(SAMPLE_PAD),
                           out_init=gated, name="retention_s")
    h = _linear(gated, [ret_w_out[0]], residual=h, out_dtype=F32, tn=256, x_buffers=1, name="ret_out")
    h = ffn(h, 0)

    xkv, xq = _norm(h, [kv_norm, norm_mix[1]], out_dtype=BF16, tm=NT, name="kv_q_norm")
    qd, kd, vd, new_kp, new_vp, q_s, k_s, v_s = [], [], [], [], [], [], [], []
    for gi, (win, dil) in enumerate(DIL_PAIRS):
        assert Tp % win == 0
        common = dict(batch=Bp, seq=Tp, dil=dil, extra_rows=MS)
        qg, qs = _dilated_linear(xq, dil_w_q[0], gi * D, tn=1024, name=f"dil_q_{gi}", **common)
        kg, ks, kw = _dilated_linear(xkv, w_kv, 2 * gi * D, window=win, name=f"dil_k_{gi}", **common)
        vg, vs, vw = _dilated_linear(xkv, w_kv, (2 * gi + 1) * D, window=win, name=f"dil_v_{gi}", **common)
        qd.append(qg)
        kd.append(kg)
        vd.append(vg)
        new_kp.append(kw.reshape(Bp, win, H, hd))
        new_vp.append(vw.reshape(Bp, win, H, hd))
        for lst, a in ((q_s, qs), (k_s, ks), (v_s, vs)):
            lst.append(a.reshape(Bs, SAMPLE_PAD, H, hd)[:, :Ts])
    q_s5, knew, vnew = (jnp.stack(lst, axis=2) for lst in (q_s, k_s, v_s))

    job = lambda c, a, i: (c[i], a[:, :, i])
    o0, l0, (nk2,) = _dilated_prompt(qd[0], kd[0], vd[0], 0, shift_jobs=[job(caches_k, knew, 2)],
                                     name="dilated_p0")
    o1, l1, (nk0, nv0, nk1, nv1) = _dilated_prompt(
        qd[1], kd[1], vd[1], 1, name="dilated_p1",
        shift_jobs=[job(caches_k, knew, 0), job(caches_v, vnew, 0),
                    job(caches_k, knew, 1), job(caches_v, vnew, 1)])
    o2, l2, (nv2,) = _dilated_prompt(qd[2], kd[2], vd[2], 2, shift_jobs=[job(caches_v, vnew, 2)],
                                     name="dilated_p2")
    a_s = _dilated_sample(q_s5, knew, vnew, caches_k, caches_v, name="dilated_s")
    a_s = jnp.pad(a_s.reshape(Bs, Ts, D), ((0, 0), (0, SAMPLE_PAD - Ts), (0, 0))).reshape(MS, D)
    a = jnp.concatenate([jnp.zeros((MP, D), BF16), a_s.astype(BF16)])
    a = _merge([o0, o1, o2], [l0, l1, l2], a, name="dilated_merge")
    h = _linear(a, [dil_w_o[0]], residual=h, out_dtype=F32, tn=512, name="dil_o")
    h = ffn(h, 1)

    y_p, = _norm(h, [norm_final], out_dtype=F32, tm=MS, rows=MP, name="final_norm_p")
    y_s, = _norm(h, [norm_final], out_dtype=F32, tm=MS, row0=MP, rows=MS, name="final_norm_s")
    return (y_p.reshape(Bp, Tp, D), y_s.reshape(Bs, SAMPLE_PAD, D)[:, :Ts], sp[None], ss[None],
            new_kp[0], new_vp[0], new_kp[1], new_vp[1], new_kp[2], new_vp[2],
            nk0, nv0, nk1, nv1, nk2, nv2)
```

```python
import functools

import jax
import jax.numpy as jnp
from jax import lax
from jax.experimental import pallas as pl
from jax.experimental.pallas import tpu as pltpu

F32 = jnp.float32
BF16 = jnp.bfloat16

RET_HEADS = 8
RET_CHUNK = 128
ROPE_BASE = 10000.0
DIL_PAIRS = ((128, 1), (512, 4), (2048, 16))
N_GROUPS = len(DIL_PAIRS)
DIL_HEADS = 16
PAST_LEN = 8192
EPS = 1e-6
NEG = -1e30

LANES = 128
SAMPLE_PAD = 16
VMEM_LIMIT = 61 * 1024 * 1024
ROW_TILES = 4


def _cparams(*sem):
    return pltpu.CompilerParams(dimension_semantics=sem, vmem_limit_bytes=VMEM_LIMIT)


def _norm_kernel(x_ref, g_ref, *o_refs):
    xf = x_ref[...]
    ms = jnp.mean(xf * xf, axis=-1, keepdims=True)
    xs = xf * lax.rsqrt(ms + EPS)
    for k, o_ref in enumerate(o_refs):
        o_ref[...] = (xs * g_ref[k:k + 1, :]).astype(o_ref.dtype)


def _norm(x, gains, *, out_dtype, tm, row0=0, rows=None, name):
    M, K = x.shape
    rows = M if rows is None else rows
    assert rows % tm == 0 and row0 % tm == 0
    g = jnp.stack(gains).astype(F32)
    off = row0 // tm
    outs = pl.pallas_call(
        _norm_kernel,
        grid=(rows // tm,),
        in_specs=[pl.BlockSpec((tm, K), lambda i: (i + off, 0)),
                  pl.BlockSpec((len(gains), K), lambda i: (0, 0))],
        out_specs=[pl.BlockSpec((tm, K), lambda i: (i, 0))] * len(gains),
        out_shape=[jax.ShapeDtypeStruct((rows, K), out_dtype)] * len(gains),
        compiler_params=_cparams("parallel"),
        name=name,
    )(x, g)
    return outs


def _linear_kernel(*refs, n_w, has_res, swiglu):
    it = iter(refs)
    x_ref = next(it)
    w_refs = [next(it) for _ in range(n_w)]
    r_ref = next(it) if has_res else None
    o_ref = next(it)
    xn = x_ref[...]
    a = jnp.dot(xn, w_refs[0][...].astype(BF16), preferred_element_type=F32)
    if swiglu:
        b = jnp.dot(xn, w_refs[1][...].astype(BF16), preferred_element_type=F32)
        a = (a * jax.nn.sigmoid(a)) * b
    if has_res:
        a = a + r_ref[...]
    o_ref[...] = a.astype(o_ref.dtype)


def _linear(x, ws, *, residual=None, swiglu=False, out_dtype, tn, row0=0, rows=None,
            row_tiles=ROW_TILES, x_buffers=2, name):
    K = x.shape[1]
    M = x.shape[0] if rows is None else rows
    ws = [w if isinstance(w, tuple) else (w[None], 0) for w in ws]
    N = ws[0][0].shape[2]
    assert M % (row_tiles * 16) == 0 and N % tn == 0 and x.dtype == BF16
    tm = M // row_tiles
    assert row0 % tm == 0 and (residual is None or row0 == 0)
    off = row0 // tm
    has_res = residual is not None
    in_specs = [pl.BlockSpec((tm, K), lambda i, j: (i + off, 0), pipeline_mode=pl.Buffered(x_buffers))]
    args = [x]
    for w, layer in ws:
        in_specs.append(pl.BlockSpec((None, K, tn), lambda i, j, layer=layer: (layer, 0, j)))
        args.append(w)
    if has_res:
        in_specs.append(pl.BlockSpec((tm, tn), lambda i, j: (i, j)))
        args.append(residual)
    kern = functools.partial(_linear_kernel, n_w=len(ws), has_res=has_res, swiglu=swiglu)
    return pl.pallas_call(
        kern,
        grid=(row_tiles, N // tn),
        in_specs=in_specs,
        out_specs=pl.BlockSpec((tm, tn), lambda i, j: (i, j)),
        out_shape=jax.ShapeDtypeStruct((M, N), out_dtype),
        compiler_params=_cparams("parallel", "arbitrary"),
        name=name,
    )(*args)


MXU_COLS = 256


def _dilated_linear_kernel(x_ref, xs_ref, w_ref, *refs, dil, win_rows, row_parts):
    it = iter(refs)
    o_ref = next(it)
    os_ref = next(it)
    win_ref = next(it) if win_rows else None
    acc_ref = next(it) if dil > 1 else None
    tm, tn = x_ref.shape[0], w_ref.shape[1]
    rp = tm // row_parts
    w0 = tm - win_rows
    for c in range(tn // MXU_COLS):
        cs = slice(c * MXU_COLS, (c + 1) * MXU_COLS)
        wb = w_ref[:, cs].astype(BF16)
        for p in range(row_parts):
            lo = p * rp
            acc = jnp.dot(x_ref[lo:lo + rp, :], wb, preferred_element_type=F32)
            if win_rows and lo + rp > w0:
                a = max(w0, lo)
                win_ref[0, a - w0:lo + rp - w0, cs] = acc[a - lo:, :]
            if dil == 1:
                o_ref[0, 0, lo:lo + rp, cs] = acc.astype(o_ref.dtype)
            else:
                n = rp // dil
                for s in range(c * MXU_COLS // LANES, (c + 1) * MXU_COLS // LANES):
                    ls = slice(s * LANES, (s + 1) * LANES)
                    acc_ref[s, lo:lo + rp, :] = acc[:, ls.start - c * MXU_COLS:ls.stop - c * MXU_COLS]
                    for r in range(dil):
                        o_ref[0, r, p * n:(p + 1) * n, ls] = (
                            acc_ref[s, pl.ds(lo + r, n, stride=dil), :].astype(o_ref.dtype))

    @pl.when(pl.program_id(0) == 0)
    def _():
        os_ref[...] = jnp.dot(xs_ref[...], w_ref[...].astype(BF16), preferred_element_type=F32)


def _dilated_linear(x, w, col0, *, batch, seq, dil, extra_rows, window=None, tm=2048, tn=512, name):
    K = x.shape[1]
    width = DIL_HEADS * (K // DIL_HEADS)
    row_parts = 2
    assert seq % tm == 0 and tm % (row_parts * dil * 16) == 0 and col0 % tn == 0 and width % tn == 0
    assert (batch * seq) % extra_rows == 0
    tpb = seq // tm
    cb0 = col0 // tn
    ncol = width // tn
    xs_block = (batch * seq) // extra_rows
    in_specs = [pl.BlockSpec((tm, K), lambda i, j: (i, 0)),
                pl.BlockSpec((extra_rows, K), lambda i, j: (xs_block, 0)),
                pl.BlockSpec((K, tn), lambda i, j: (0, cb0 + j))]
    out_specs = [pl.BlockSpec((1, dil, tm // dil, tn), lambda i, j: (i // tpb, 0, i % tpb, j)),
                 pl.BlockSpec((extra_rows, tn), lambda i, j: (0, jnp.where(i == 0, j, ncol - 1)))]
    out_shape = [jax.ShapeDtypeStruct((batch, dil, seq // dil, width), BF16),
                 jax.ShapeDtypeStruct((extra_rows, width), F32)]
    win_rows = 0
    if window is not None:
        win_rows = min(window, tm)
        assert window % win_rows == 0
        t0 = tpb - window // win_rows

        def win_map(i, j):
            t = i % tpb
            inside = t >= t0
            return (i // tpb, jnp.where(inside, t - t0, 0), jnp.where(inside, j, 0))

        out_specs.append(pl.BlockSpec((1, win_rows, tn), win_map))
        out_shape.append(jax.ShapeDtypeStruct((batch, window, width), F32))
    scratch = [pltpu.VMEM((tn // LANES, tm, LANES), F32)] if dil > 1 else []
    return pl.pallas_call(
        functools.partial(_dilated_linear_kernel, dil=dil, win_rows=win_rows, row_parts=row_parts),
        grid=(batch * tpb, ncol),
        in_specs=in_specs,
        out_specs=out_specs,
        out_shape=out_shape,
        scratch_shapes=scratch,
        compiler_params=_cparams("arbitrary", "arbitrary"),
        name=name,
    )(x, x, w)


def _retention_kernel(*refs, dk, dv, hps, chunk, nc):
    (q_ref, k_ref, v_ref, g_ref, cos_ref, sin_ref, dmat_ref, qdec_ref, kdec_ref,
     cdec_ref, gn_ref, s0_ref) = refs[:12]
    o_ref, s_ref = refs[-2:]

    s_ref[...] = s0_ref[...]
    half = dk // 2

    def one_chunk(c):
        rows = slice(None) if nc == 1 else pl.ds(pl.multiple_of(c * chunk, chunk), chunk)
        cos = cos_ref[rows, :]
        sin = sin_ref[rows, :]

        def rope(x):
            x1 = x[:, :half]
            x2 = x[:, half:]
            return jnp.concatenate([x1 * cos - x2 * sin, x1 * sin + x2 * cos], axis=-1)

        for j in range(hps):
            q = rope(q_ref[rows, j * dk:(j + 1) * dk].astype(F32))
            k = rope(k_ref[rows, j * dk:(j + 1) * dk].astype(F32)) * (dk ** -0.5)
            v = v_ref[rows, j * dv:(j + 1) * dv]
            s = s_ref[0, j]
            sc = lax.dot_general(q.astype(BF16), k.astype(BF16), (((1,), (1,)), ((), ())),
                                 preferred_element_type=F32) * dmat_ref[j]
            o = (jnp.dot(sc.astype(BF16), v, preferred_element_type=F32)
                 + jnp.dot((q * qdec_ref[j]).astype(BF16), s.astype(BF16), preferred_element_type=F32))
            kd = (k * kdec_ref[j]).astype(BF16)
            s_ref[0, j] = s * cdec_ref[j] + lax.dot_general(kd, v, (((0,), (0,)), ((), ())),
                                                           preferred_element_type=F32)
            mu = jnp.mean(o, axis=-1, keepdims=True)
            var = jnp.mean(jnp.square(o - mu), axis=-1, keepdims=True)
            on = (o - mu) * lax.rsqrt(var + EPS) * gn_ref[:, j * dv:(j + 1) * dv]
            gate = g_ref[rows, j * dv:(j + 1) * dv].astype(F32)
            o_ref[rows, j * dv:(j + 1) * dv] = ((gate * jax.nn.sigmoid(gate)) * on).astype(o_ref.dtype)

    if nc == 1:
        one_chunk(0)
    else:
        def body(c, carry):
            one_chunk(c)
            return carry

        lax.fori_loop(0, nc, body, 0, unroll=8 if nc % 8 == 0 else 1)


def _retention(proj, s0, gn_g, *, row0, batch, seq, chunk, chunk_true, pos, out_init=None, name):
    H = RET_HEADS
    dk, dv = s0.shape[-2], s0.shape[-1]
    nc = seq // chunk
    assert row0 % chunk == 0
    lg = jnp.log1p(-jnp.exp2(-5.0 - jnp.arange(H, dtype=F32)))
    i = jnp.arange(chunk, dtype=F32)
    dist = i[:, None] - i[None, :]
    dmat = jnp.where(dist >= 0, jnp.exp(jnp.maximum(dist, 0.0)[None] * lg[:, None, None]), 0.0)
    qdec = jnp.exp((i + 1.0)[None, :, None] * lg[:, None, None])
    kdec = jnp.exp((chunk_true - 1.0 - i)[None, :, None] * lg[:, None, None])
    cdec = jnp.exp(chunk_true * lg).reshape(H, 1, 1)
    half = dk // 2
    inv = ROPE_BASE ** (-jnp.arange(half, dtype=F32) / half)
    ang = pos.astype(F32)[:, None] * inv[None, :]
    cos, sin = jnp.cos(ang), jnp.sin(ang)

    hps = H if seq * H * (2 * dk + 2 * dv) * 2 <= (1 << 20) else 1
    hg = H // hps
    kb, vb = hg, (2 * H * dk) // (hps * dv)
    gb = vb + hg
    assert row0 % seq == 0
    rb = row0 // seq
    in_specs = [
        pl.BlockSpec((seq, hps * dk), lambda b, h: (rb + b, h)),
        pl.BlockSpec((seq, hps * dk), lambda b, h: (rb + b, kb + h)),
        pl.BlockSpec((seq, hps * dv), lambda b, h: (rb + b, vb + h)),
        pl.BlockSpec((seq, hps * dv), lambda b, h: (rb + b, gb + h)),
        pl.BlockSpec((seq, half), lambda b, h: (0, 0)),
        pl.BlockSpec((seq, half), lambda b, h: (0, 0)),
        pl.BlockSpec((hps, chunk, chunk), lambda b, h: (h, 0, 0)),
        pl.BlockSpec((hps, chunk, 1), lambda b, h: (h, 0, 0)),
        pl.BlockSpec((hps, chunk, 1), lambda b, h: (h, 0, 0)),
        pl.BlockSpec((hps, 1, 1), lambda b, h: (h, 0, 0)),
        pl.BlockSpec((1, hps * dv), lambda b, h: (0, h)),
        pl.BlockSpec((1, hps, dk, dv), lambda b, h: (b, h, 0, 0)),
    ]
    args = [proj, proj, proj, proj, cos, sin, dmat, qdec, kdec, cdec, gn_g.reshape(1, H * dv), s0]
    aliases = {}
    if out_init is not None:
        in_specs.append(pl.BlockSpec(memory_space=pl.ANY))
        args.append(out_init)
        aliases = {len(args) - 1: 0}
    out_specs = [
        pl.BlockSpec((seq, hps * dv), lambda b, h: (rb + b, h)),
        pl.BlockSpec((1, hps, dk, dv), lambda b, h: (b, h, 0, 0)),
    ]
    out_shape = [jax.ShapeDtypeStruct((proj.shape[0], H * dv), BF16),
                 jax.ShapeDtypeStruct((batch, H, dk, dv), F32)]
    return pl.pallas_call(
        functools.partial(_retention_kernel, dk=dk, dv=dv, hps=hps, chunk=chunk, nc=nc),
        grid=(batch, hg),
        in_specs=in_specs,
        out_specs=out_specs,
        out_shape=out_shape,
        input_output_aliases=aliases,
        compiler_params=_cparams("parallel", "parallel"),
        name=name,
    )(*args)


def _dilated_prompt_kernel(*refs, heads, hd, jobs, ts, steps):
    q_ref, k_ref, v_ref = refs[:3]
    nj = len(jobs)
    caches, news = refs[3:3 + nj], refs[3 + nj:3 + 2 * nj]
    o_ref, l_ref = refs[3 + 2 * nj:5 + 2 * nj]
    shifted = refs[5 + 2 * nj:5 + 3 * nj]
    kp_ref, vp_ref, buf, in_sem, out_sem = refs[5 + 3 * nj:]

    n = pl.program_id(2)
    blk = q_ref.shape[2]
    step = (pl.program_id(0) * pl.num_programs(1) + pl.program_id(1)) * pl.num_programs(2) + n
    last = steps - 1

    def plan(h):
        out, off = [], 0
        for j, (halves) in enumerate(jobs):
            r0, nr, tail = halves[h]
            parts = COPY_STREAMS if nr >= COPY_STREAMS * 64 else 1
            for p in range(parts):
                lo, hi = nr * p // parts, nr * (p + 1) // parts
                if hi > lo:
                    has_tail = tail and p == parts - 1
                    out.append((j, r0 + lo, hi - lo, off, has_tail))
                    off += hi - lo + (ts if has_tail else 0)
        return out

    def in_copies(h, bc):
        return [pltpu.make_async_copy(caches[j].at[bc, pl.ds(ts + r0, nr)],
                                      buf.at[h, pl.ds(off, nr)], in_sem.at[h])
                for (j, r0, nr, off, tail) in plan(h)]

    def out_copies(h, bc):
        return [pltpu.make_async_copy(buf.at[h, pl.ds(off, nr + (ts if tail else 0))],
                                      shifted[j].at[bc, pl.ds(r0, nr + (ts if tail else 0))],
                                      out_sem.at[h])
                for (j, r0, nr, off, tail) in plan(h)]

    @pl.when(step == 0)
    def _():
        for cp in in_copies(0, 0):
            cp.start()

    for h in range(2):
        @pl.when(step % 2 == h)
        def _(h=h):
            bc = step // 2
            for cp in in_copies(h, bc):
                cp.wait()
            for (j, r0, nr, off, tail) in plan(h):
                if tail:
                    buf[h, pl.ds(off + nr, ts)] = news[j][bc]
            for cp in out_copies(h, bc):
                cp.start()

            @pl.when(step > 0)
            def _():
                for cp in out_copies(1 - h, (step - 1) // 2):
                    cp.wait()

            @pl.when(step < last)
            def _():
                for cp in in_copies(1 - h, (step + 1) // 2):
                    cp.start()

    @pl.when(n == 0)
    def _():
        kp_ref[...] = jnp.zeros_like(kp_ref)
        vp_ref[...] = jnp.zeros_like(vp_ref)

    row = lax.broadcasted_iota(jnp.int32, (blk, blk), 0)
    col = lax.broadcasted_iota(jnp.int32, (blk, blk), 1)
    mask_c = col <= row
    mask_p = (col - row) >= jnp.where(n > 0, 0, blk)
    lane = lax.broadcasted_iota(jnp.int32, (blk, LANES), 1)
    scale = hd ** -0.5
    nt = (((1,), (1,)), ((), ()))
    hpt = heads // l_ref.shape[0]

    def qk(cs, key_ref, mask):
        s = lax.dot_general(q_ref[0, 0, :, cs], key_ref[:, cs], nt, preferred_element_type=F32)
        return jnp.where(mask, s * scale, NEG)

    kc_ref, vc_ref = k_ref.at[0, 0], v_ref.at[0, 0]
    for g0 in range(0, heads, hpt):
        cols = [slice(h * hd, (h + 1) * hd) for h in range(g0, g0 + hpt)]
        s_c = [qk(cs, kc_ref, mask_c) for cs in cols]
        s_p = [qk(cs, kp_ref, mask_p) for cs in cols]
        m = [jnp.maximum(jnp.max(a, axis=-1, keepdims=True), jnp.max(b, axis=-1, keepdims=True))
             for a, b in zip(s_c, s_p)]
        p_c = [jnp.exp(a - mm) for a, mm in zip(s_c, m)]
        p_p = [jnp.exp(b - mm) for b, mm in zip(s_p, m)]
        den = [jnp.sum(a, axis=-1, keepdims=True) + jnp.sum(b, axis=-1, keepdims=True)
               for a, b in zip(p_c, p_p)]
        inv = [1.0 / d for d in den]
        lse_tile = jnp.zeros((blk, LANES), F32)
        for i, cs in enumerate(cols):
            o_ref[0, 0, :, cs] = (
                jnp.dot((p_c[i] * inv[i]).astype(BF16), vc_ref[:, cs], preferred_element_type=F32)
                + jnp.dot((p_p[i] * inv[i]).astype(BF16), vp_ref[:, cs], preferred_element_type=F32)
            ).astype(o_ref.dtype)
            lse_tile = jnp.where(lane == i, m[i] + jnp.log(den[i]), lse_tile)
        l_ref[g0 // hpt, 0, 0] = lse_tile

    kp_ref[...] = k_ref[0, 0]
    vp_ref[...] = v_ref[0, 0]

    @pl.when(step == last)
    def _():
        for cp in out_copies(last % 2, last // 2):
            cp.wait()


LSE_TILES = 4
COPY_STREAMS = 4


def _dilated_prompt(q, k, v, gi, *, shift_jobs, name):
    win, dil = DIL_PAIRS[gi]
    blk = win // dil
    H = DIL_HEADS
    batch, _, sd, width = q.shape
    hd = width // H
    assert q.shape[1] == dil and sd % blk == 0
    nb = sd // blk
    slab = pl.BlockSpec((1, 1, blk, width), lambda b, r, n: (b, r, n, 0))
    in_specs = [slab, slab, slab]
    args = [q, k, v]
    out_specs = [slab, pl.BlockSpec((LSE_TILES, 1, 1, blk, LANES), lambda b, r, n: (0, b, r, n, 0))]
    out_shape = [jax.ShapeDtypeStruct(q.shape, BF16),
                 jax.ShapeDtypeStruct((LSE_TILES, batch, dil, sd, LANES), F32)]

    steps = batch * dil * nb
    caches = [c for c, _ in shift_jobs]
    news = [a for _, a in shift_jobs]
    ts = news[0].shape[1]
    Bc = caches[0].shape[0]
    assert steps == 2 * Bc
    jobs, slot_rows = [], [0, 0]
    for c in caches:
        keep = c.shape[1] - ts
        first = keep // 2 if keep * c.shape[2] * c.shape[3] * 4 > (2 << 20) else 0
        halves = ((0, first, False), (first, keep - first, True))
        jobs.append(halves)
        for h in range(2):
            slot_rows[h] += halves[h][1] + (ts if halves[h][2] else 0)
    any_spec = pl.BlockSpec(memory_space=pl.ANY)
    in_specs += [any_spec] * len(caches) + [pl.BlockSpec(memory_space=pltpu.VMEM)] * len(news)
    args += caches + news
    out_specs = out_specs + [any_spec] * len(caches)
    out_shape = out_shape + [jax.ShapeDtypeStruct(c.shape, c.dtype) for c in caches]

    res = pl.pallas_call(
        functools.partial(_dilated_prompt_kernel, heads=H, hd=hd, jobs=tuple(jobs), ts=ts, steps=steps),
        grid=(batch, dil, nb),
        in_specs=in_specs,
        out_specs=out_specs,
        out_shape=out_shape,
        scratch_shapes=[pltpu.VMEM((blk, width), BF16), pltpu.VMEM((blk, width), BF16),
                        pltpu.VMEM((2, max(slot_rows)) + caches[0].shape[2:], caches[0].dtype),
                        pltpu.SemaphoreType.DMA((2,)), pltpu.SemaphoreType.DMA((2,))],
        compiler_params=_cparams("arbitrary", "arbitrary", "arbitrary"),
        name=name,
    )(*args)
    return res[0], res[1], list(res[2:])


def _merge_kernel(o0_ref, o1_ref, o2_ref, l0_ref, l1_ref, l2_ref, init_ref, out_ref,
                  s0_ref, s1_ref, nat_ref, *, hd):
    del init_ref
    d1, d2 = o1_ref.shape[1], o2_ref.shape[1]
    rows = o2_ref.shape[2]
    rep = d2 // d1
    nh = o2_ref.shape[3] // hd
    for hh in range(nh):
        cs = slice(hh * hd, (hh + 1) * hd)
        s0_ref[hh] = o0_ref[:, cs].astype(F32)
        s1_ref[hh] = o1_ref[0, :, :, cs].astype(F32)
    for r in range(d2):
        l2 = l2_ref[0, 0, r]
        l1 = l1_ref[0, 0, r % d1, pl.ds(r // d1, rows, stride=rep), :]
        l0 = l0_ref[0, pl.ds(r, rows, stride=d2), :]
        mx = jnp.maximum(jnp.maximum(l0, l1), l2)
        e0 = jnp.exp(l0 - mx)
        e1 = jnp.exp(l1 - mx)
        e2 = jnp.exp(l2 - mx)
        tot = 1.0 / (e0 + e1 + e2)
        w0, w1, w2 = e0 * tot, e1 * tot, e2 * tot
        for hh in range(nh):
            ls = slice(hh, hh + 1)
            o2 = o2_ref[0, r, :, hh * hd:(hh + 1) * hd].astype(F32)
            o1 = s1_ref[hh, r % d1, pl.ds(r // d1, rows, stride=rep), :]
            o0 = s0_ref[hh, pl.ds(r, rows, stride=d2), :]
            nat_ref[hh, pl.ds(r, rows, stride=d2), :] = w0[:, ls] * o0 + w1[:, ls] * o1 + w2[:, ls] * o2
    for hh in range(nh):
        out_ref[:, hh * hd:(hh + 1) * hd] = nat_ref[hh].astype(out_ref.dtype)


def _merge(outs, lses, init, *, name):
    (_, d0), (_, d1), (_, d2) = DIL_PAIRS
    batch, _, sd2, width = outs[2].shape
    seq = sd2 * d2
    rows = DIL_PAIRS[2][0] // d2
    span = rows * d2
    assert d0 == 1 and d2 % d1 == 0 and seq % span == 0
    nsp = seq // span
    cw = width // LSE_TILES
    hd = width // DIL_HEADS
    o0 = outs[0].reshape(batch * seq, width)
    l0 = lses[0].reshape(LSE_TILES, batch * seq, LANES)
    in_specs = [
        pl.BlockSpec((span, cw), lambda t, c: (t, c)),
        pl.BlockSpec((1, d1, span // d1, cw), lambda t, c: (t // nsp, 0, t % nsp, c)),
        pl.BlockSpec((1, d2, rows, cw), lambda t, c: (t // nsp, 0, t % nsp, c)),
        pl.BlockSpec((1, span, LANES), lambda t, c: (c, t, 0)),
        pl.BlockSpec((1, 1, d1, span // d1, LANES), lambda t, c: (c, t // nsp, 0, t % nsp, 0)),
        pl.BlockSpec((1, 1, d2, rows, LANES), lambda t, c: (c, t // nsp, 0, t % nsp, 0)),
        pl.BlockSpec(memory_space=pl.ANY),
    ]
    return pl.pallas_call(
        functools.partial(_merge_kernel, hd=hd),
        grid=(batch * nsp, LSE_TILES),
        in_specs=in_specs,
        out_specs=pl.BlockSpec((span, cw), lambda t, c: (t, c)),
        out_shape=jax.ShapeDtypeStruct(init.shape, init.dtype),
        scratch_shapes=[pltpu.VMEM((cw // hd, span, hd), F32),
                        pltpu.VMEM((cw // hd, d1, span // d1, hd), F32),
                        pltpu.VMEM((cw // hd, span, hd), F32)],
        input_output_aliases={6: 0},
        compiler_params=_cparams("parallel", "parallel"),
        name=name,
    )(o0, outs[1], outs[2], l0, lses[1], lses[2], init)


def _dilated_sample_kernel(q_ref, kn_ref, vn_ref, k0_ref, v0_ref, k1_ref, v1_ref, k2_ref, v2_ref,
                           o_ref, *, ts, hd):
    caches = ((k0_ref, v0_ref), (k1_ref, v1_ref), (k2_ref, v2_ref))
    scale = hd ** -0.5
    for t in range(ts):
        outs, lses = [], []
        for gi, (win, dil) in enumerate(DIL_PAIRS):
            kc_ref, vc_ref = caches[gi]
            q = q_ref[0, t, gi]
            r = t % dil if dil > 1 else 0
            kc = kc_ref[0, :, r]
            vc = vc_ref[0, :, r]
            s_c = jnp.sum(kc * q[None], axis=-1, keepdims=True) * scale
            if dil == 1:
                cidx = lax.broadcasted_iota(jnp.int32, s_c.shape, 0)
                s_c = jnp.where(cidx >= t, s_c, NEG)
                new_rows = list(range(t + 1))
            else:
                new_rows = [t]
            s_n = [jnp.sum(kn_ref[0, u, gi] * q, axis=-1, keepdims=True) * scale for u in new_rows]
            m = jnp.max(s_c, axis=0)
            for sn in s_n:
                m = jnp.maximum(m, sn)
            p_c = jnp.exp(s_c - m[None])
            p_n = [jnp.exp(sn - m) for sn in s_n]
            den = jnp.sum(p_c, axis=0)
            for pn in p_n:
                den = den + pn
            inv = 1.0 / den
            o = jnp.sum((p_c * inv[None]) * vc, axis=0)
            for u, pn in zip(new_rows, p_n):
                o = o + (pn * inv) * vn_ref[0, u, gi]
            outs.append(o)
            lses.append(m + jnp.log(den))
        mx = jnp.maximum(jnp.maximum(lses[0], lses[1]), lses[2])
        es = [jnp.exp(l - mx) for l in lses]
        tot = 1.0 / (es[0] + es[1] + es[2])
        o_ref[0, t] = (es[0] * tot) * outs[0] + (es[1] * tot) * outs[1] + (es[2] * tot) * outs[2]


def _dilated_sample(q, knew, vnew, caches_k, caches_v, *, name):
    B, ts, G, H, hd = q.shape
    small = pl.BlockSpec((1, ts, G, H, hd), lambda b: (b, 0, 0, 0, 0))
    in_specs = [small, small, small]
    args = [q, knew, vnew]
    for gi, (win, dil) in enumerate(DIL_PAIRS):
        L = caches_k[gi].shape[1]
        assert L == win and L % dil == 0 and (dil == 1 or ts <= dil)
        nres = min(dil, ts)
        spec = pl.BlockSpec((1, L // dil, nres, H, hd), lambda b: (b, 0, 0, 0, 0))
        for c in (caches_k[gi], caches_v[gi]):
            in_specs.append(spec)
            args.append(c.reshape(B, L // dil, dil, H, hd))
    return pl.pallas_call(
        functools.partial(_dilated_sample_kernel, ts=ts, hd=hd),
        grid=(B,),
        in_specs=in_specs,
        out_specs=pl.BlockSpec((1, ts, H, hd), lambda b: (b, 0, 0, 0)),
        out_shape=jax.ShapeDtypeStruct((B, ts, H, hd), F32),
        compiler_params=_cparams("parallel"),
        name=name,
    )(*args)


def kernel(x_prompt, x_sample, state_ret, cache_k_w128, cache_v_w128, cache_k_w512, cache_v_w512,
           cache_k_w2048, cache_v_w2048, norm_mix, norm_ffn, ret_w_in, ret_gn, ret_w_out,
           kv_norm, w_kv, dil_w_q, dil_w_o, ffn_w1, ffn_w3, ffn_w2, norm_final):
    Bp, Tp, D = x_prompt.shape
    Bs, Ts, _ = x_sample.shape
    H, hd = DIL_HEADS, D // DIL_HEADS
    caches_k = [cache_k_w128, cache_k_w512, cache_k_w2048]
    caches_v = [cache_v_w128, cache_v_w512, cache_v_w2048]
    MP, MS = Bp * Tp, Bs * SAMPLE_PAD
    MT = MP + MS
    NT = MT // 16

    h = jnp.concatenate([x_prompt.reshape(MP, D),
                         jnp.pad(x_sample, ((0, 0), (0, SAMPLE_PAD - Ts), (0, 0))).reshape(MS, D)])

    def ffn(h, layer):
        xn, = _norm(h, [norm_ffn[layer]], out_dtype=BF16, tm=NT, name=f"ffn_norm_{layer}")
        act = _linear(xn, [(ffn_w1, layer), (ffn_w3, layer)], swiglu=True, out_dtype=BF16, tn=512,
                      name=f"ffn_up_{layer}")
        return _linear(act, [(ffn_w2, layer)], residual=h, out_dtype=F32, tn=256, x_buffers=1,
                       name=f"ffn_down_{layer}")

    chunk_p = RET_CHUNK if Tp % RET_CHUNK == 0 else Tp
    xn, = _norm(h, [norm_mix[0]], out_dtype=BF16, tm=NT, name="mix_norm_0")
    proj = _linear(xn, [ret_w_in[0]], out_dtype=BF16, tn=1024, name="ret_in")
    s0_p = jnp.zeros((Bp,) + state_ret.shape[2:], F32)
    gated = jnp.zeros((MT, RET_HEADS * state_ret.shape[-1]), BF16)
    gated, sp = _retention(proj, s0_p, ret_gn[0], row0=0, batch=Bp, seq=Tp, chunk=chunk_p,
                           chunk_true=chunk_p, pos=jnp.arange(Tp), out_init=gated, name="retention_p")
    gated, ss = _retention(proj, state_ret[0], ret_gn[0], row0=MP, batch=Bs, seq=SAMPLE_PAD,
                           chunk=SAMPLE_PAD, chunk_true=Ts, pos=PAST_LEN + jnp.arange(SAMPLE_PAD),
                           out_init=gated, name="retention_s")
    h = _linear(gated, [ret_w_out[0]], residual=h, out_dtype=F32, tn=256, x_buffers=1, name="ret_out")
    h = ffn(h, 0)

    xkv, xq = _norm(h, [kv_norm, norm_mix[1]], out_dtype=BF16, tm=NT, name="kv_q_norm")
    qd, kd, vd, new_kp, new_vp, q_s, k_s, v_s = [], [], [], [], [], [], [], []
    for gi, (win, dil) in enumerate(DIL_PAIRS):
        assert Tp % win == 0
        common = dict(batch=Bp, seq=Tp, dil=dil, extra_rows=MS)
        qg, qs = _dilated_linear(xq, dil_w_q[0], gi * D, tn=1024, name=f"dil_q_{gi}", **common)
        kg, ks, kw = _dilated_linear(xkv, w_kv, 2 * gi * D, window=win, name=f"dil_k_{gi}", **common)
        vg, vs, vw = _dilated_linear(xkv, w_kv, (2 * gi + 1) * D, window=win, name=f"dil_v_{gi}", **common)
        qd.append(qg)
        kd.append(kg)
        vd.append(vg)
        new_kp.append(kw.reshape(Bp, win, H, hd))
        new_vp.append(vw.reshape(Bp, win, H, hd))
        for lst, a in ((q_s, qs), (k_s, ks), (v_s, vs)):
            lst.append(a.reshape(Bs, SAMPLE_PAD, H, hd)[:, :Ts])
    q_s5, knew, vnew = (jnp.stack(lst, axis=2) for lst in (q_s, k_s, v_s))

    job = lambda c, a, i: (c[i], a[:, :, i])
    o0, l0, (nk2,) = _dilated_prompt(qd[0], kd[0], vd[0], 0, shift_jobs=[job(caches_k, knew, 2)],
                                     name="dilated_p0")
    o1, l1, (nk0, nv0, nk1, nv1) = _dilated_prompt(
        qd[1], kd[1], vd[1], 1, name="dilated_p1",
        shift_jobs=[job(caches_k, knew, 0), job(caches_v, vnew, 0),
                    job(caches_k, knew, 1), job(caches_v, vnew, 1)])
    o2, l2, (nv2,) = _dilated_prompt(qd[2], kd[2], vd[2], 2, shift_jobs=[job(caches_v, vnew, 2)],
                                     name="dilated_p2")
    a_s = _dilated_sample(q_s5, knew, vnew, caches_k, caches_v, name="dilated_s")
    a_s = jnp.pad(a_s.reshape(Bs, Ts, D), ((0, 0), (0, SAMPLE_PAD - Ts), (0, 0))).reshape(MS, D)
    a = jnp.concatenate([jnp.zeros((MP, D), BF16), a_s.astype(BF16)])
    a = _merge([o0, o1, o2], [l0, l1, l2], a, name="dilated_merge")
    h = _linear(a, [dil_w_o[0]], residual=h, out_dtype=F32, tn=512, name="dil_o")
    h = ffn(h, 1)

    y_p, = _norm(h, [norm_final], out_dtype=F32, tm=MS, rows=MP, name="final_norm_p")
    y_s, = _norm(h, [norm_final], out_dtype=F32, tm=MS, row0=MP, rows=MS, name="final_norm_s")
    return (y_p.reshape(Bp, Tp, D), y_s.reshape(Bs, SAMPLE_PAD, D)[:, :Ts], sp[None], ss[None],
            new_kp[0], new_vp[0], new_kp[1], new_vp[1], new_kp[2], new_vp[2],
            nk0, nv0, nk1, nv1, nk2, nv2)
```

```python
import functools

import jax
import jax.numpy as jnp
from jax import lax
from jax.experimental import pallas as pl
from jax.experimental.pallas import tpu as pltpu

F32 = jnp.float32
BF16 = jnp.bfloat16

RET_HEADS = 8
RET_CHUNK = 128
ROPE_BASE = 10000.0
DIL_PAIRS = ((128, 1), (512, 4), (2048, 16))
N_GROUPS = len(DIL_PAIRS)
DIL_HEADS = 16
PAST_LEN = 8192
EPS = 1e-6
NEG = -1e30

LANES = 128
SAMPLE_PAD = 16
VMEM_LIMIT = 61 * 1024 * 1024
ROW_TILES = 4


def _cparams(*sem):
    return pltpu.CompilerParams(dimension_semantics=sem, vmem_limit_bytes=VMEM_LIMIT)


def _norm_kernel(x_ref, g_ref, *o_refs):
    xf = x_ref[...]
    ms = jnp.mean(xf * xf, axis=-1, keepdims=True)
    xs = xf * lax.rsqrt(ms + EPS)
    for k, o_ref in enumerate(o_refs):
        o_ref[...] = (xs * g_ref[k:k + 1, :]).astype(o_ref.dtype)


def _norm(x, gains, *, out_dtype, tm, row0=0, rows=None, name):
    M, K = x.shape
    rows = M if rows is None else rows
    assert rows % tm == 0 and row0 % tm == 0
    g = jnp.stack(gains).astype(F32)
    off = row0 // tm
    outs = pl.pallas_call(
        _norm_kernel,
        grid=(rows // tm,),
        in_specs=[pl.BlockSpec((tm, K), lambda i: (i + off, 0)),
                  pl.BlockSpec((len(gains), K), lambda i: (0, 0))],
        out_specs=[pl.BlockSpec((tm, K), lambda i: (i, 0))] * len(gains),
        out_shape=[jax.ShapeDtypeStruct((rows, K), out_dtype)] * len(gains),
        compiler_params=_cparams("parallel"),
        name=name,
    )(x, g)
    return outs


def _linear_kernel(*refs, n_w, has_res, swiglu):
    it = iter(refs)
    x_ref = next(it)
    w_refs = [next(it) for _ in range(n_w)]
    r_ref = next(it) if has_res else None
    o_ref = next(it)
    xn = x_ref[...]
    a = jnp.dot(xn, w_refs[0][...].astype(BF16), preferred_element_type=F32)
    if swiglu:
        b = jnp.dot(xn, w_refs[1][...].astype(BF16), preferred_element_type=F32)
        a = (a * jax.nn.sigmoid(a)) * b
    if has_res:
        a = a + r_ref[...]
    o_ref[...] = a.astype(o_ref.dtype)


def _linear(x, ws, *, residual=None, swiglu=False, out_dtype, tn, row0=0, rows=None,
            row_tiles=ROW_TILES, x_buffers=2, name):
    K = x.shape[1]
    M = x.shape[0] if rows is None else rows
    ws = [w if isinstance(w, tuple) else (w[None], 0) for w in ws]
    N = ws[0][0].shape[2]
    assert M % (row_tiles * 16) == 0 and N % tn == 0 and x.dtype == BF16
    tm = M // row_tiles
    assert row0 % tm == 0 and (residual is None or row0 == 0)
    off = row0 // tm
    has_res = residual is not None
    in_specs = [pl.BlockSpec((tm, K), lambda i, j: (i + off, 0), pipeline_mode=pl.Buffered(x_buffers))]
    args = [x]
    for w, layer in ws:
        in_specs.append(pl.BlockSpec((None, K, tn), lambda i, j, layer=layer: (layer, 0, j)))
        args.append(w)
    if has_res:
        in_specs.append(pl.BlockSpec((tm, tn), lambda i, j: (i, j)))
        args.append(residual)
    kern = functools.partial(_linear_kernel, n_w=len(ws), has_res=has_res, swiglu=swiglu)
    return pl.pallas_call(
        kern,
        grid=(row_tiles, N // tn),
        in_specs=in_specs,
        out_specs=pl.BlockSpec((tm, tn), lambda i, j: (i, j)),
        out_shape=jax.ShapeDtypeStruct((M, N), out_dtype),
        compiler_params=_cparams("parallel", "arbitrary"),
        name=name,
    )(*args)


MXU_COLS = 256


def _dilated_linear_kernel(x_ref, xs_ref, w_ref, *refs, dil, win_rows, row_parts, parts, npp, tpb, t0):
    it = iter(refs)
    o_ref = next(it)
    os_ref = next(it)
    win_refs = [next(it) for _ in range(parts)] if win_rows else []
    in_window = pl.program_id(0) % tpb >= t0
    part = pl.program_id(1) // npp
    acc_ref = next(it) if dil > 1 else None
    tm, tn = x_ref.shape[0], w_ref.shape[1]
    rp = tm // row_parts
    w0 = tm - win_rows
    for c in range(tn // MXU_COLS):
        cs = slice(c * MXU_COLS, (c + 1) * MXU_COLS)
        wb = w_ref[:, cs].astype(BF16)
        for p in range(row_parts):
            lo = p * rp
            acc = jnp.dot(x_ref[lo:lo + rp, :], wb, preferred_element_type=F32)
            if win_rows and lo + rp > w0:
                a = max(w0, lo)
                for wp, win_ref in enumerate(win_refs):
                    @pl.when(jnp.logical_and(in_window, part == wp))
                    def _(win_ref=win_ref, acc=acc, a=a, lo=lo, cs=cs):
                        win_ref[0, a - w0:lo + rp - w0, cs] = acc[a - lo:, :]
            if dil == 1:
                o_ref[0, 0, lo:lo + rp, cs] = acc.astype(o_ref.dtype)
            else:
                n = rp // dil
                for s in range(c * MXU_COLS // LANES, (c + 1) * MXU_COLS // LANES):
                    ls = slice(s * LANES, (s + 1) * LANES)
                    acc_ref[s, lo:lo + rp, :] = acc[:, ls.start - c * MXU_COLS:ls.stop - c * MXU_COLS]
                    for r in range(dil):
                        o_ref[0, r, p * n:(p + 1) * n, ls] = (
                            acc_ref[s, pl.ds(lo + r, n, stride=dil), :].astype(o_ref.dtype))

    @pl.when(pl.program_id(0) == 0)
    def _():
        os_ref[...] = jnp.dot(xs_ref[...], w_ref[...].astype(BF16), preferred_element_type=F32)


def _dilated_linear(x, w, col0, *, batch, seq, dil, extra_rows, parts=1, window=None, tm=2048, tn=512,
                    name):
    K = x.shape[1]
    width = DIL_HEADS * (K // DIL_HEADS)
    total = parts * width
    row_parts = 2
    assert seq % tm == 0 and tm % (row_parts * dil * 16) == 0 and col0 % tn == 0 and width % tn == 0
    assert (batch * seq) % extra_rows == 0
    tpb = seq // tm
    cb0 = col0 // tn
    npp = width // tn
    ncol = parts * npp
    xs_block = (batch * seq) // extra_rows
    in_specs = [pl.BlockSpec((tm, K), lambda i, j: (i, 0), pipeline_mode=pl.Buffered(2 if parts == 1 else 1)),
                pl.BlockSpec((extra_rows, K), lambda i, j: (xs_block, 0)),
                pl.BlockSpec((K, tn), lambda i, j: (0, cb0 + j))]
    out_specs = [pl.BlockSpec((1, dil, tm // dil, tn), lambda i, j: (i // tpb, 0, i % tpb, j)),
                 pl.BlockSpec((extra_rows, tn), lambda i, j: (0, jnp.where(i == 0, j, ncol - 1)))]
    out_shape = [jax.ShapeDtypeStruct((batch, dil, seq // dil, total), BF16),
                 jax.ShapeDtypeStruct((extra_rows, total), F32)]
    win_rows, t0 = 0, tpb
    if window is not None:
        win_rows = min(window, tm)
        assert window % win_rows == 0
        t0 = tpb - window // win_rows

        def win_map(wp, i, j):
            t = i % tpb
            own = jnp.logical_and(t >= t0, j // npp == wp)
            pinned = jnp.where(jnp.logical_and(t >= t0, j // npp > wp), npp - 1, 0)
            return (i // tpb, jnp.where(t >= t0, t - t0, 0), jnp.where(own, j % npp, pinned))

        for wp in range(parts):
            out_specs.append(pl.BlockSpec((1, win_rows, tn), functools.partial(win_map, wp)))
            out_shape.append(jax.ShapeDtypeStruct((batch, window, width), F32))
    scratch = [pltpu.VMEM((tn // LANES, tm, LANES), F32)] if dil > 1 else []
    return pl.pallas_call(
        functools.partial(_dilated_linear_kernel, dil=dil, win_rows=win_rows, row_parts=row_parts,
                          parts=parts, npp=npp, tpb=tpb, t0=t0),
        grid=(batch * tpb, ncol),
        in_specs=in_specs,
        out_specs=out_specs,
        out_shape=out_shape,
        scratch_shapes=scratch,
        compiler_params=_cparams("arbitrary", "arbitrary"),
        name=name,
    )(x, x, w)


def _retention_kernel(*refs, dk, dv, hps, chunk, nc):
    (q_ref, k_ref, v_ref, g_ref, cos_ref, sin_ref, dmat_ref, qdec_ref, kdec_ref,
     cdec_ref, gn_ref, s0_ref) = refs[:12]
    o_ref, s_ref = refs[-2:]

    s_ref[...] = s0_ref[...]
    half = dk // 2

    def one_chunk(c):
        rows = slice(None) if nc == 1 else pl.ds(pl.multiple_of(c * chunk, chunk), chunk)
        cos = cos_ref[rows, :]
        sin = sin_ref[rows, :]

        def rope(x):
            x1 = x[:, :half]
            x2 = x[:, half:]
            return jnp.concatenate([x1 * cos - x2 * sin, x1 * sin + x2 * cos], axis=-1)

        for j in range(hps):
            q = rope(q_ref[rows, j * dk:(j + 1) * dk].astype(F32))
            k = rope(k_ref[rows, j * dk:(j + 1) * dk].astype(F32)) * (dk ** -0.5)
            v = v_ref[rows, j * dv:(j + 1) * dv]
            s = s_ref[0, j]
            sc = lax.dot_general(q.astype(BF16), k.astype(BF16), (((1,), (1,)), ((), ())),
                                 preferred_element_type=F32) * dmat_ref[j]
            o = (jnp.dot(sc.astype(BF16), v, preferred_element_type=F32)
                 + jnp.dot((q * qdec_ref[j]).astype(BF16), s.astype(BF16), preferred_element_type=F32))
            kd = (k * kdec_ref[j]).astype(BF16)
            s_ref[0, j] = s * cdec_ref[j] + lax.dot_general(kd, v, (((0,), (0,)), ((), ())),
                                                           preferred_element_type=F32)
            mu = jnp.mean(o, axis=-1, keepdims=True)
            var = jnp.mean(jnp.square(o - mu), axis=-1, keepdims=True)
            on = (o - mu) * lax.rsqrt(var + EPS) * gn_ref[:, j * dv:(j + 1) * dv]
            gate = g_ref[rows, j * dv:(j + 1) * dv].astype(F32)
            o_ref[rows, j * dv:(j + 1) * dv] = ((gate * jax.nn.sigmoid(gate)) * on).astype(o_ref.dtype)

    if nc == 1:
        one_chunk(0)
    else:
        def body(c, carry):
            one_chunk(c)
            return carry

        lax.fori_loop(0, nc, body, 0, unroll=8 if nc % 8 == 0 else 1)


def _retention(proj, s0, gn_g, *, row0, batch, seq, chunk, chunk_true, pos, out_init=None, name):
    H = RET_HEADS
    dk, dv = s0.shape[-2], s0.shape[-1]
    nc = seq // chunk
    assert row0 % chunk == 0
    lg = jnp.log1p(-jnp.exp2(-5.0 - jnp.arange(H, dtype=F32)))
    i = jnp.arange(chunk, dtype=F32)
    dist = i[:, None] - i[None, :]
    dmat = jnp.where(dist >= 0, jnp.exp(jnp.maximum(dist, 0.0)[None] * lg[:, None, None]), 0.0)
    qdec = jnp.exp((i + 1.0)[None, :, None] * lg[:, None, None])
    kdec = jnp.exp((chunk_true - 1.0 - i)[None, :, None] * lg[:, None, None])
    cdec = jnp.exp(chunk_true * lg).reshape(H, 1, 1)
    half = dk // 2
    inv = ROPE_BASE ** (-jnp.arange(half, dtype=F32) / half)
    ang = pos.astype(F32)[:, None] * inv[None, :]
    cos, sin = jnp.cos(ang), jnp.sin(ang)

    hps = H if seq * H * (2 * dk + 2 * dv) * 2 <= (1 << 20) else 1
    hg = H // hps
    kb, vb = hg, (2 * H * dk) // (hps * dv)
    gb = vb + hg
    assert row0 % seq == 0
    rb = row0 // seq
    in_specs = [
        pl.BlockSpec((seq, hps * dk), lambda b, h: (rb + b, h)),
        pl.BlockSpec((seq, hps * dk), lambda b, h: (rb + b, kb + h)),
        pl.BlockSpec((seq, hps * dv), lambda b, h: (rb + b, vb + h)),
        pl.BlockSpec((seq, hps * dv), lambda b, h: (rb + b, gb + h)),
        pl.BlockSpec((seq, half), lambda b, h: (0, 0)),
        pl.BlockSpec((seq, half), lambda b, h: (0, 0)),
        pl.BlockSpec((hps, chunk, chunk), lambda b, h: (h, 0, 0)),
        pl.BlockSpec((hps, chunk, 1), lambda b, h: (h, 0, 0)),
        pl.BlockSpec((hps, chunk, 1), lambda b, h: (h, 0, 0)),
        pl.BlockSpec((hps, 1, 1), lambda b, h: (h, 0, 0)),
        pl.BlockSpec((1, hps * dv), lambda b, h: (0, h)),
        pl.BlockSpec((1, hps, dk, dv), lambda b, h: (b, h, 0, 0)),
    ]
    args = [proj, proj, proj, proj, cos, sin, dmat, qdec, kdec, cdec, gn_g.reshape(1, H * dv), s0]
    aliases = {}
    if out_init is not None:
        in_specs.append(pl.BlockSpec(memory_space=pl.ANY))
        args.append(out_init)
        aliases = {len(args) - 1: 0}
    out_specs = [
        pl.BlockSpec((seq, hps * dv), lambda b, h: (rb + b, h)),
        pl.BlockSpec((1, hps, dk, dv), lambda b, h: (b, h, 0, 0)),
    ]
    out_shape = [jax.ShapeDtypeStruct((proj.shape[0], H * dv), BF16),
                 jax.ShapeDtypeStruct((batch, H, dk, dv), F32)]
    return pl.pallas_call(
        functools.partial(_retention_kernel, dk=dk, dv=dv, hps=hps, chunk=chunk, nc=nc),
        grid=(batch, hg),
        in_specs=in_specs,
        out_specs=out_specs,
        out_shape=out_shape,
        input_output_aliases=aliases,
        compiler_params=_cparams("parallel", "parallel"),
        name=name,
    )(*args)


def _dilated_prompt_kernel(*refs, heads, hd, jobs, ts, steps):
    q_ref, k_ref, v_ref = refs[:3]
    nj = len(jobs)
    caches, news = refs[3:3 + nj], refs[3 + nj:3 + 2 * nj]
    o_ref, l_ref = refs[3 + 2 * nj:5 + 2 * nj]
    shifted = refs[5 + 2 * nj:5 + 3 * nj]
    kp_ref, vp_ref, buf, in_sem, out_sem = refs[5 + 3 * nj:]

    n = pl.program_id(2)
    blk = q_ref.shape[2]
    step = (pl.program_id(0) * pl.num_programs(1) + pl.program_id(1)) * pl.num_programs(2) + n
    last = steps - 1

    def plan(h):
        out, off = [], 0
        for j, (halves) in enumerate(jobs):
            r0, nr, tail = halves[h]
            parts = COPY_STREAMS if nr >= COPY_STREAMS * 64 else 1
            for p in range(parts):
                lo, hi = nr * p // parts, nr * (p + 1) // parts
                if hi > lo:
                    has_tail = tail and p == parts - 1
                    out.append((j, r0 + lo, hi - lo, off, has_tail))
                    off += hi - lo + (ts if has_tail else 0)
        return out

    def in_copies(h, bc):
        return [pltpu.make_async_copy(caches[j].at[bc, pl.ds(ts + r0, nr)],
                                      buf.at[h, pl.ds(off, nr)], in_sem.at[h])
                for (j, r0, nr, off, tail) in plan(h)]

    def out_copies(h, bc):
        return [pltpu.make_async_copy(buf.at[h, pl.ds(off, nr + (ts if tail else 0))],
                                      shifted[j].at[bc, pl.ds(r0, nr + (ts if tail else 0))],
                                      out_sem.at[h])
                for (j, r0, nr, off, tail) in plan(h)]

    @pl.when(step == 0)
    def _():
        for cp in in_copies(0, 0):
            cp.start()

    for h in range(2):
        @pl.when(step % 2 == h)
        def _(h=h):
            bc = step // 2
            for cp in in_copies(h, bc):
                cp.wait()
            for (j, r0, nr, off, tail) in plan(h):
                if tail:
                    buf[h, pl.ds(off + nr, ts)] = news[j][bc]
            for cp in out_copies(h, bc):
                cp.start()

            @pl.when(step > 0)
            def _():
                for cp in out_copies(1 - h, (step - 1) // 2):
                    cp.wait()

            @pl.when(step < last)
            def _():
                for cp in in_copies(1 - h, (step + 1) // 2):
                    cp.start()

    @pl.when(n == 0)
    def _():
        kp_ref[...] = jnp.zeros_like(kp_ref)
        vp_ref[...] = jnp.zeros_like(vp_ref)

    row = lax.broadcasted_iota(jnp.int32, (blk, blk), 0)
    col = lax.broadcasted_iota(jnp.int32, (blk, blk), 1)
    mask_c = col <= row
    mask_p = (col - row) >= jnp.where(n > 0, 0, blk)
    lane = lax.broadcasted_iota(jnp.int32, (blk, LANES), 1)
    scale = hd ** -0.5
    nt = (((1,), (1,)), ((), ()))
    hpt = heads // l_ref.shape[0]

    def qk(cs, key_ref, mask):
        s = lax.dot_general(q_ref[0, 0, :, cs], key_ref[:, cs], nt, preferred_element_type=F32)
        return jnp.where(mask, s * scale, NEG)

    kc_ref, vc_ref = k_ref.at[0, 0], v_ref.at[0, 0]
    for g0 in range(0, heads, hpt):
        cols = [slice(h * hd, (h + 1) * hd) for h in range(g0, g0 + hpt)]
        s_c = [qk(cs, kc_ref, mask_c) for cs in cols]
        s_p = [qk(cs, kp_ref, mask_p) for cs in cols]
        m = [jnp.maximum(jnp.max(a, axis=-1, keepdims=True), jnp.max(b, axis=-1, keepdims=True))
             for a, b in zip(s_c, s_p)]
        p_c = [jnp.exp(a - mm) for a, mm in zip(s_c, m)]
        p_p = [jnp.exp(b - mm) for b, mm in zip(s_p, m)]
        den = [jnp.sum(a, axis=-1, keepdims=True) + jnp.sum(b, axis=-1, keepdims=True)
               for a, b in zip(p_c, p_p)]
        inv = [1.0 / d for d in den]
        lse_tile = jnp.zeros((blk, LANES), F32)
        for i, cs in enumerate(cols):
            o_ref[0, 0, :, cs] = (
                jnp.dot((p_c[i] * inv[i]).astype(BF16), vc_ref[:, cs], preferred_element_type=F32)
                + jnp.dot((p_p[i] * inv[i]).astype(BF16), vp_ref[:, cs], preferred_element_type=F32)
            ).astype(o_ref.dtype)
            lse_tile = jnp.where(lane == i, m[i] + jnp.log(den[i]), lse_tile)
        l_ref[g0 // hpt, 0, 0] = lse_tile

    kp_ref[...] = k_ref[0, 0]
    vp_ref[...] = v_ref[0, 0]

    @pl.when(step == last)
    def _():
        for cp in out_copies(last % 2, last // 2):
            cp.wait()


LSE_TILES = 4
COPY_STREAMS = 4


def _dilated_prompt(q, k, v, gi, *, shift_jobs, k_col=0, v_col=0, name):
    win, dil = DIL_PAIRS[gi]
    blk = win // dil
    H = DIL_HEADS
    batch, _, sd, width = q.shape
    hd = width // H
    assert q.shape[1] == dil and sd % blk == 0
    nb = sd // blk
    slab = pl.BlockSpec((1, 1, blk, width), lambda b, r, n: (b, r, n, 0))
    in_specs = [slab, pl.BlockSpec((1, 1, blk, width), lambda b, r, n: (b, r, n, k_col)),
                pl.BlockSpec((1, 1, blk, width), lambda b, r, n: (b, r, n, v_col))]
    args = [q, k, v]
    out_specs = [slab, pl.BlockSpec((LSE_TILES, 1, 1, blk, LANES), lambda b, r, n: (0, b, r, n, 0))]
    out_shape = [jax.ShapeDtypeStruct(q.shape, BF16),
                 jax.ShapeDtypeStruct((LSE_TILES, batch, dil, sd, LANES), F32)]

    steps = batch * dil * nb
    caches = [c for c, _ in shift_jobs]
    news = [a for _, a in shift_jobs]
    ts = news[0].shape[1]
    Bc = caches[0].shape[0]
    assert steps == 2 * Bc
    jobs, slot_rows = [], [0, 0]
    for c in caches:
        keep = c.shape[1] - ts
        first = keep // 2 if keep * c.shape[2] * c.shape[3] * 4 > (2 << 20) else 0
        halves = ((0, first, False), (first, keep - first, True))
        jobs.append(halves)
        for h in range(2):
            slot_rows[h] += halves[h][1] + (ts if halves[h][2] else 0)
    any_spec = pl.BlockSpec(memory_space=pl.ANY)
    in_specs += [any_spec] * len(caches) + [pl.BlockSpec(memory_space=pltpu.VMEM)] * len(news)
    args += caches + news
    out_specs = out_specs + [any_spec] * len(caches)
    out_shape = out_shape + [jax.ShapeDtypeStruct(c.shape, c.dtype) for c in caches]

    res = pl.pallas_call(
        functools.partial(_dilated_prompt_kernel, heads=H, hd=hd, jobs=tuple(jobs), ts=ts, steps=steps),
        grid=(batch, dil, nb),
        in_specs=in_specs,
        out_specs=out_specs,
        out_shape=out_shape,
        scratch_shapes=[pltpu.VMEM((blk, width), BF16), pltpu.VMEM((blk, width), BF16),
                        pltpu.VMEM((2, max(slot_rows)) + caches[0].shape[2:], caches[0].dtype),
                        pltpu.SemaphoreType.DMA((2,)), pltpu.SemaphoreType.DMA((2,))],
        compiler_params=_cparams("arbitrary", "arbitrary", "arbitrary"),
        name=name,
    )(*args)
    return res[0], res[1], list(res[2:])


def _merge_kernel(o0_ref, o1_ref, o2_ref, l0_ref, l1_ref, l2_ref, init_ref, out_ref,
                  s0_ref, s1_ref, nat_ref, *, hd):
    del init_ref
    d1, d2 = o1_ref.shape[1], o2_ref.shape[1]
    rows = o2_ref.shape[2]
    rep = d2 // d1
    nh = o2_ref.shape[3] // hd
    for hh in range(nh):
        cs = slice(hh * hd, (hh + 1) * hd)
        s0_ref[hh] = o0_ref[:, cs].astype(F32)
        s1_ref[hh] = o1_ref[0, :, :, cs].astype(F32)
    for r in range(d2):
        l2 = l2_ref[0, 0, r]
        l1 = l1_ref[0, 0, r % d1, pl.ds(r // d1, rows, stride=rep), :]
        l0 = l0_ref[0, pl.ds(r, rows, stride=d2), :]
        mx = jnp.maximum(jnp.maximum(l0, l1), l2)
        e0 = jnp.exp(l0 - mx)
        e1 = jnp.exp(l1 - mx)
        e2 = jnp.exp(l2 - mx)
        tot = 1.0 / (e0 + e1 + e2)
        w0, w1, w2 = e0 * tot, e1 * tot, e2 * tot
        for hh in range(nh):
            ls = slice(hh, hh + 1)
            o2 = o2_ref[0, r, :, hh * hd:(hh + 1) * hd].astype(F32)
            o1 = s1_ref[hh, r % d1, pl.ds(r // d1, rows, stride=rep), :]
            o0 = s0_ref[hh, pl.ds(r, rows, stride=d2), :]
            nat_ref[hh, pl.ds(r, rows, stride=d2), :] = w0[:, ls] * o0 + w1[:, ls] * o1 + w2[:, ls] * o2
    for hh in range(nh):
        out_ref[:, hh * hd:(hh + 1) * hd] = nat_ref[hh].astype(out_ref.dtype)


def _merge(outs, lses, init, *, name):
    (_, d0), (_, d1), (_, d2) = DIL_PAIRS
    batch, _, sd2, width = outs[2].shape
    seq = sd2 * d2
    rows = DIL_PAIRS[2][0] // d2
    span = rows * d2
    assert d0 == 1 and d2 % d1 == 0 and seq % span == 0
    nsp = seq // span
    cw = width // LSE_TILES
    hd = width // DIL_HEADS
    o0 = outs[0].reshape(batch * seq, width)
    l0 = lses[0].reshape(LSE_TILES, batch * seq, LANES)
    in_specs = [
        pl.BlockSpec((span, cw), lambda t, c: (t, c)),
        pl.BlockSpec((1, d1, span // d1, cw), lambda t, c: (t // nsp, 0, t % nsp, c)),
        pl.BlockSpec((1, d2, rows, cw), lambda t, c: (t // nsp, 0, t % nsp, c)),
        pl.BlockSpec((1, span, LANES), lambda t, c: (c, t, 0)),
        pl.BlockSpec((1, 1, d1, span // d1, LANES), lambda t, c: (c, t // nsp, 0, t % nsp, 0)),
        pl.BlockSpec((1, 1, d2, rows, LANES), lambda t, c: (c, t // nsp, 0, t % nsp, 0)),
        pl.BlockSpec(memory_space=pl.ANY),
    ]
    return pl.pallas_call(
        functools.partial(_merge_kernel, hd=hd),
        grid=(batch * nsp, LSE_TILES),
        in_specs=in_specs,
        out_specs=pl.BlockSpec((span, cw), lambda t, c: (t, c)),
        out_shape=jax.ShapeDtypeStruct(init.shape, init.dtype),
        scratch_shapes=[pltpu.VMEM((cw // hd, span, hd), F32),
                        pltpu.VMEM((cw // hd, d1, span // d1, hd), F32),
                        pltpu.VMEM((cw // hd, span, hd), F32)],
        input_output_aliases={6: 0},
        compiler_params=_cparams("parallel", "parallel"),
        name=name,
    )(o0, outs[1], outs[2], l0, lses[1], lses[2], init)


def _dilated_sample_kernel(q_ref, kn_ref, vn_ref, k0_ref, v0_ref, k1_ref, v1_ref, k2_ref, v2_ref,
                           o_ref, *, ts, hd):
    caches = ((k0_ref, v0_ref), (k1_ref, v1_ref), (k2_ref, v2_ref))
    scale = hd ** -0.5
    for t in range(ts):
        outs, lses = [], []
        for gi, (win, dil) in enumerate(DIL_PAIRS):
            kc_ref, vc_ref = caches[gi]
            q = q_ref[0, t, gi]
            r = t % dil if dil > 1 else 0
            kc = kc_ref[0, :, r]
            vc = vc_ref[0, :, r]
            s_c = jnp.sum(kc * q[None], axis=-1, keepdims=True) * scale
            if dil == 1:
                cidx = lax.broadcasted_iota(jnp.int32, s_c.shape, 0)
                s_c = jnp.where(cidx >= t, s_c, NEG)
                new_rows = list(range(t + 1))
            else:
                new_rows = [t]
            s_n = [jnp.sum(kn_ref[0, u, gi] * q, axis=-1, keepdims=True) * scale for u in new_rows]
            m = jnp.max(s_c, axis=0)
            for sn in s_n:
                m = jnp.maximum(m, sn)
            p_c = jnp.exp(s_c - m[None])
            p_n = [jnp.exp(sn - m) for sn in s_n]
            den = jnp.sum(p_c, axis=0)
            for pn in p_n:
                den = den + pn
            inv = 1.0 / den
            o = jnp.sum((p_c * inv[None]) * vc, axis=0)
            for u, pn in zip(new_rows, p_n):
                o = o + (pn * inv) * vn_ref[0, u, gi]
            outs.append(o)
            lses.append(m + jnp.log(den))
        mx = jnp.maximum(jnp.maximum(lses[0], lses[1]), lses[2])
        es = [jnp.exp(l - mx) for l in lses]
        tot = 1.0 / (es[0] + es[1] + es[2])
        o_ref[0, t] = (es[0] * tot) * outs[0] + (es[1] * tot) * outs[1] + (es[2] * tot) * outs[2]


def _dilated_sample(q, knew, vnew, caches_k, caches_v, *, name):
    B, ts, G, H, hd = q.shape
    small = pl.BlockSpec((1, ts, G, H, hd), lambda b: (b, 0, 0, 0, 0))
    in_specs = [small, small, small]
    args = [q, knew, vnew]
    for gi, (win, dil) in enumerate(DIL_PAIRS):
        L = caches_k[gi].shape[1]
        assert L == win and L % dil == 0 and (dil == 1 or ts <= dil)
        nres = min(dil, ts)
        spec = pl.BlockSpec((1, L // dil, nres, H, hd), lambda b: (b, 0, 0, 0, 0))
        for c in (caches_k[gi], caches_v[gi]):
            in_specs.append(spec)
            args.append(c.reshape(B, L // dil, dil, H, hd))
    return pl.pallas_call(
        functools.partial(_dilated_sample_kernel, ts=ts, hd=hd),
        grid=(B,),
        in_specs=in_specs,
        out_specs=pl.BlockSpec((1, ts, H, hd), lambda b: (b, 0, 0, 0)),
        out_shape=jax.ShapeDtypeStruct((B, ts, H, hd), F32),
        compiler_params=_cparams("parallel"),
        name=name,
    )(*args)


def kernel(x_prompt, x_sample, state_ret, cache_k_w128, cache_v_w128, cache_k_w512, cache_v_w512,
           cache_k_w2048, cache_v_w2048, norm_mix, norm_ffn, ret_w_in, ret_gn, ret_w_out,
           kv_norm, w_kv, dil_w_q, dil_w_o, ffn_w1, ffn_w3, ffn_w2, norm_final):
    Bp, Tp, D = x_prompt.shape
    Bs, Ts, _ = x_sample.shape
    H, hd = DIL_HEADS, D // DIL_HEADS
    caches_k = [cache_k_w128, cache_k_w512, cache_k_w2048]
    caches_v = [cache_v_w128, cache_v_w512, cache_v_w2048]
    MP, MS = Bp * Tp, Bs * SAMPLE_PAD
    MT = MP + MS
    NT = MT // 16

    h = jnp.concatenate([x_prompt.reshape(MP, D),
                         jnp.pad(x_sample, ((0, 0), (0, SAMPLE_PAD - Ts), (0, 0))).reshape(MS, D)])

    def ffn(h, layer):
        xn, = _norm(h, [norm_ffn[layer]], out_dtype=BF16, tm=NT, name=f"ffn_norm_{layer}")
        act = _linear(xn, [(ffn_w1, layer), (ffn_w3, layer)], swiglu=True, out_dtype=BF16, tn=512,
                      name=f"ffn_up_{layer}")
        return _linear(act, [(ffn_w2, layer)], residual=h, out_dtype=F32, tn=256, x_buffers=1,
                       name=f"ffn_down_{layer}")

    chunk_p = RET_CHUNK if Tp % RET_CHUNK == 0 else Tp
    xn, = _norm(h, [norm_mix[0]], out_dtype=BF16, tm=NT, name="mix_norm_0")
    proj = _linear(xn, [ret_w_in[0]], out_dtype=BF16, tn=1024, name="ret_in")
    s0_p = jnp.zeros((Bp,) + state_ret.shape[2:], F32)
    gated = jnp.zeros((MT, RET_HEADS * state_ret.shape[-1]), BF16)
    gated, sp = _retention(proj, s0_p, ret_gn[0], row0=0, batch=Bp, seq=Tp, chunk=chunk_p,
                           chunk_true=chunk_p, pos=jnp.arange(Tp), out_init=gated, name="retention_p")
    gated, ss = _retention(proj, state_ret[0], ret_gn[0], row0=MP, batch=Bs, seq=SAMPLE_PAD,
                           chunk=SAMPLE_PAD, chunk_true=Ts, pos=PAST_LEN + jnp.arange(SAMPLE_PAD),
                           out_init=gated, name="retention_s")
    h = _linear(gated, [ret_w_out[0]], residual=h, out_dtype=F32, tn=256, x_buffers=1, name="ret_out")
    h = ffn(h, 0)

    xkv, xq = _norm(h, [kv_norm, norm_mix[1]], out_dtype=BF16, tm=NT, name="kv_q_norm")
    qd, kd, vd, new_kp, new_vp, q_s, k_s, v_s = [], [], [], [], [], [], [], []
    for gi, (win, dil) in enumerate(DIL_PAIRS):
        assert Tp % win == 0
        common = dict(batch=Bp, seq=Tp, dil=dil, extra_rows=MS)
        qg, qs = _dilated_linear(xq, dil_w_q[0], gi * D, tn=1024, name=f"dil_q_{gi}", **common)
        kvg, kvs, kw, vw = _dilated_linear(xkv, w_kv, 2 * gi * D, parts=2, window=win,
                                           name=f"dil_kv_{gi}", **common)
        ks, vs = kvs[:, :D], kvs[:, D:]
        qd.append(qg)
        kd.append(kvg)
        vd.append(kvg)
        new_kp.append(kw.reshape(Bp, win, H, hd))
        new_vp.append(vw.reshape(Bp, win, H, hd))
        for lst, a in ((q_s, qs), (k_s, ks), (v_s, vs)):
            lst.append(a.reshape(Bs, SAMPLE_PAD, H, hd)[:, :Ts])
    q_s5, knew, vnew = (jnp.stack(lst, axis=2) for lst in (q_s, k_s, v_s))

    job = lambda c, a, i: (c[i], a[:, :, i])
    o0, l0, (nk2,) = _dilated_prompt(qd[0], kd[0], vd[0], 0, shift_jobs=[job(caches_k, knew, 2)],
                                     v_col=1, name="dilated_p0")
    o1, l1, (nk0, nv0, nk1, nv1) = _dilated_prompt(
        qd[1], kd[1], vd[1], 1, v_col=1, name="dilated_p1",
        shift_jobs=[job(caches_k, knew, 0), job(caches_v, vnew, 0),
                    job(caches_k, knew, 1), job(caches_v, vnew, 1)])
    o2, l2, (nv2,) = _dilated_prompt(qd[2], kd[2], vd[2], 2, shift_jobs=[job(caches_v, vnew, 2)],
                                     v_col=1, name="dilated_p2")
    a_s = _dilated_sample(q_s5, knew, vnew, caches_k, caches_v, name="dilated_s")
    a_s = jnp.pad(a_s.reshape(Bs, Ts, D), ((0, 0), (0, SAMPLE_PAD - Ts), (0, 0))).reshape(MS, D)
    a = jnp.concatenate([jnp.zeros((MP, D), BF16), a_s.astype(BF16)])
    a = _merge([o0, o1, o2], [l0, l1, l2], a, name="dilated_merge")
    h = _linear(a, [dil_w_o[0]], residual=h, out_dtype=F32, tn=512, name="dil_o")
    h = ffn(h, 1)

    y_p, = _norm(h, [norm_final], out_dtype=F32, tm=MS, rows=MP, name="final_norm_p")
    y_s, = _norm(h, [norm_final], out_dtype=F32, tm=MS, row0=MP, rows=MS, name="final_norm_s")
    return (y_p.reshape(Bp, Tp, D), y_s.reshape(Bs, SAMPLE_PAD, D)[:, :Ts], sp[None], ss[None],
            new_kp[0], new_vp[0], new_kp[1], new_vp[1], new_kp[2], new_vp[2],
            nk0, nv0, nk1, nv1, nk2, nv2)
```
